```python
import jax, jax.numpy as jnp
from jax import lax
import numpy as np

D_MODEL = 1024
BATCH = 16
SEQ = 4096
DEPTH = 1

HEAD_DIM = 64
NSA_HEADS = 8
NSA_KV_HEADS = 2
NSA_WIDTH = NSA_HEADS * HEAD_DIM
KV_WIDTH = NSA_KV_HEADS * HEAD_DIM
N_GATES = 3 * NSA_HEADS
GMLP_WIDTH = D_MODEL - NSA_WIDTH
GMLP_GROUPS = 8
GMLP_GROUP_DIM = GMLP_WIDTH // GMLP_GROUPS
GMLP_CHUNK = 128
CMP_BLOCK = 32
CMP_STRIDE = 16
CMP_HIDDEN = 128
SEL_BLOCK = 64
SEL_TOP_N = 16
WINDOW = 512
Q_BLOCK = 64
NORM_EPS = 1e-6
NEG_INF = -1e30
FORCE_BONUS = 1e4
SPLIT_SIZES = (NSA_WIDTH,
               KV_WIDTH, KV_WIDTH,
               KV_WIDTH, KV_WIDTH,
               KV_WIDTH, KV_WIDTH,
               N_GATES,
               NSA_WIDTH,
               GMLP_WIDTH, GMLP_WIDTH,
               GMLP_WIDTH)
IN_WIDTH = sum(SPLIT_SIZES)

kernel_name = "hybrid_nsa_gmlp_parallel_heads"


def rms_norm(x, g):
    xf = x.astype(jnp.float32)
    y = xf * lax.rsqrt(jnp.mean(xf * xf, axis=-1, keepdims=True) + NORM_EPS)
    return (y * g.astype(jnp.float32)).astype(x.dtype)


def split_columns(proj):
    parts, off = [], 0
    for size in SPLIT_SIZES:
        parts.append(proj[..., off:off + size])
        off += size
    return parts


def compress_blocks(k, w1, w2, pe):
    T = k.shape[1]
    n_cmp = (T - CMP_BLOCK) // CMP_STRIDE + 1
    idx = jnp.arange(n_cmp)[:, None] * CMP_STRIDE + jnp.arange(CMP_BLOCK)[None, :]
    blocks = k[:, idx] + pe[:, None, :]
    w1r = w1.reshape(CMP_BLOCK, HEAD_DIM, CMP_HIDDEN)
    hid = jax.nn.gelu(jnp.einsum('bnlgd,lde->bnge', blocks, w1r))
    return jnp.einsum('bnge,ef->bngf', hid, w2)


def nsa_attention(q, k_cmp, v_cmp, k_sel, v_sel, k_win, v_win, gates):
    f32 = jnp.float32
    B, T, H, dh = q.shape
    G = k_sel.shape[2]
    Hg = H // G
    n_cmp = k_cmp.shape[1]
    n_blk = T // SEL_BLOCK
    top_n = min(SEL_TOP_N, n_blk)
    n_chunks = T // Q_BLOCK
    n_win = WINDOW + Q_BLOCK

    qf = q.astype(f32) * (dh ** -0.5)
    kcf, vcf = k_cmp.astype(f32), v_cmp.astype(f32)
    gf = gates.astype(f32)
    cmp_start = jnp.arange(n_cmp) * CMP_STRIDE
    cmp_end = cmp_start + CMP_BLOCK - 1
    blk_ids = jnp.arange(n_blk)
    overlap = ((cmp_start[:, None] < (blk_ids[None, :] + 1) * SEL_BLOCK)
               & (cmp_end[:, None] >= blk_ids[None, :] * SEL_BLOCK)).astype(f32)
    kb = k_sel.reshape(B, n_blk, SEL_BLOCK, G, dh).transpose(0, 3, 1, 2, 4).reshape(B, G, n_blk, SEL_BLOCK * dh)
    vb = v_sel.reshape(B, n_blk, SEL_BLOCK, G, dh).transpose(0, 3, 1, 2, 4).reshape(B, G, n_blk, SEL_BLOCK * dh)
    pad = ((0, 0), (WINDOW, 0), (0, 0), (0, 0))
    kw = jnp.pad(k_win.astype(f32), pad)
    vw = jnp.pad(v_win.astype(f32), pad)

    def chunk(c):
        start = c * Q_BLOCK
        t = start + jnp.arange(Q_BLOCK)
        qc = lax.dynamic_slice_in_dim(qf, start, Q_BLOCK, axis=1).reshape(B, Q_BLOCK, G, Hg, dh)
        valid_c = cmp_end[None, :] <= t[:, None]
        s_c = jnp.where(valid_c, jnp.einsum('bqghd,bngd->bghqn', qc, kcf), NEG_INF)
        has_c = jnp.any(valid_c, axis=-1).astype(f32)[:, None]
        p_c = jax.nn.softmax(s_c, axis=-1) * has_c
        o_c = jnp.einsum('bghqn,bngd->bqghd', p_c, vcf)
        imp = jnp.einsum('bghqn,nj->bgqj', p_c, overlap)
        cur = t // SEL_BLOCK
        forced = (blk_ids[None, :] == 0) | (blk_ids[None, :] == cur[:, None]) | (blk_ids[None, :] == cur[:, None] - 1)
        valid_b = blk_ids[None, :] <= cur[:, None]
        score = jnp.where(valid_b, jnp.where(forced, imp + FORCE_BONUS, imp), -1.0)
        top_vals, top_idx = lax.top_k(score, top_n)
        sel_ok = top_vals >= 0.0
        gidx = top_idx.reshape(B, G, Q_BLOCK * top_n)[..., None]
        kg = jnp.take_along_axis(kb, gidx, axis=2).astype(f32).reshape(B, G, Q_BLOCK, top_n, SEL_BLOCK, dh)
        vg = jnp.take_along_axis(vb, gidx, axis=2).astype(f32).reshape(B, G, Q_BLOCK, top_n, SEL_BLOCK, dh)
        tok_pos = top_idx[..., None] * SEL_BLOCK + jnp.arange(SEL_BLOCK)
        m_s = sel_ok[..., None] & (tok_pos <= t[:, None, None])
        s_s = jnp.where(m_s[:, :, None], jnp.einsum('bqghd,bgqnsd->bghqns', qc, kg), NEG_INF)
        p_s = jax.nn.softmax(s_s.reshape(B, G, Hg, Q_BLOCK, top_n * SEL_BLOCK), axis=-1)
        p_s = p_s.reshape(B, G, Hg, Q_BLOCK, top_n, SEL_BLOCK)
        o_s = jnp.einsum('bghqns,bgqnsd->bqghd', p_s, vg)
        kwc = lax.dynamic_slice_in_dim(kw, start, n_win, axis=1)
        vwc = lax.dynamic_slice_in_dim(vw, start, n_win, axis=1)
        pos = start - WINDOW + jnp.arange(n_win)
        m_w = (pos[None, :] >= 0) & (pos[None, :] <= t[:, None]) & (pos[None, :] > t[:, None] - WINDOW)
        s_w = jnp.where(m_w, jnp.einsum('bqghd,bkgd->bghqk', qc, kwc), NEG_INF)
        p_w = jax.nn.softmax(s_w, axis=-1)
        o_w = jnp.einsum('bghqk,bkgd->bqghd', p_w, vwc)
        gc = lax.dynamic_slice_in_dim(gf, start, Q_BLOCK, axis=1).reshape(B, Q_BLOCK, G, Hg, 3)
        o = gc[..., 0:1] * o_c + gc[..., 1:2] * o_s + gc[..., 2:3] * o_w
        return o.reshape(B, Q_BLOCK, H * dh)

    outs = lax.map(chunk, jnp.arange(n_chunks))
    return outs.transpose(1, 0, 2, 3).reshape(B, T, H * dh).astype(q.dtype)


def spatial_gating(u, v, ln_g, ln_b, w_s, b_s):
    f32 = jnp.float32
    B, T, _ = u.shape
    C = T // GMLP_CHUNK
    shp = (B, C, GMLP_CHUNK, GMLP_GROUPS, GMLP_GROUP_DIM)
    vf = v.astype(f32).reshape(shp)
    mu = jnp.mean(vf, axis=-1, keepdims=True)
    var = jnp.mean(jnp.square(vf - mu), axis=-1, keepdims=True)
    vn = (vf - mu) * lax.rsqrt(var + NORM_EPS) * ln_g.astype(f32) + ln_b.astype(f32)
    causal = jnp.tril(jnp.ones((GMLP_CHUNK, GMLP_CHUNK), dtype=bool))
    w = jnp.where(causal, w_s.astype(f32), 0.0)
    mixed = jnp.einsum('gts,bcsgd->bctgd', w, vn) + b_s.astype(f32).T[:, :, None]
    return (u.astype(f32).reshape(shp) * mixed).reshape(B, T, GMLP_WIDTH).astype(u.dtype)


def setup_inputs(seed: int = 0) -> dict:
    key = jax.random.key(seed)
    ks = jax.random.split(key, 16)
    f32 = jnp.float32
    nrm = lambda k, s, sc: jax.random.normal(k, s, f32) * sc
    return {
        "x": jax.random.normal(ks[0], (BATCH, SEQ, D_MODEL), f32),
        "norm_in_g": 1.0 + nrm(ks[1], (DEPTH, D_MODEL), 0.01),
        "w_in": nrm(ks[2], (DEPTH, D_MODEL, IN_WIDTH), D_MODEL ** -0.5),
        "w_cmp_k1": nrm(ks[3], (DEPTH, CMP_BLOCK * HEAD_DIM, CMP_HIDDEN), (CMP_BLOCK * HEAD_DIM) ** -0.5),
        "w_cmp_k2": nrm(ks[4], (DEPTH, CMP_HIDDEN, HEAD_DIM), CMP_HIDDEN ** -0.5),
        "pe_cmp_k": nrm(ks[5], (DEPTH, CMP_BLOCK, HEAD_DIM), 0.1),
        "w_cmp_v1": nrm(ks[6], (DEPTH, CMP_BLOCK * HEAD_DIM, CMP_HIDDEN), (CMP_BLOCK * HEAD_DIM) ** -0.5),
        "w_cmp_v2": nrm(ks[7], (DEPTH, CMP_HIDDEN, HEAD_DIM), CMP_HIDDEN ** -0.5),
        "pe_cmp_v": nrm(ks[8], (DEPTH, CMP_BLOCK, HEAD_DIM), 0.1),
        "gmlp_ln_g": 1.0 + nrm(ks[9], (DEPTH, GMLP_GROUPS, GMLP_GROUP_DIM), 0.01),
        "gmlp_ln_b": nrm(ks[10], (DEPTH, GMLP_GROUPS, GMLP_GROUP_DIM), 0.01),
        "w_spatial": nrm(ks[11], (DEPTH, GMLP_GROUPS, GMLP_CHUNK, GMLP_CHUNK), GMLP_CHUNK ** -0.5),
        "b_spatial": 1.0 + nrm(ks[12], (DEPTH, GMLP_GROUPS, GMLP_CHUNK), 0.01),
        "w_out": nrm(ks[13], (DEPTH, D_MODEL, D_MODEL), D_MODEL ** -0.5),
        "norm_f_g": 1.0 + nrm(ks[14], (D_MODEL,), 0.01),
    }


def reference(x, norm_in_g, w_in, w_cmp_k1, w_cmp_k2, pe_cmp_k, w_cmp_v1, w_cmp_v2, pe_cmp_v,
              gmlp_ln_g, gmlp_ln_b, w_spatial, b_spatial, w_out, norm_f_g):
    B, T, _ = x.shape
    G, dh = NSA_KV_HEADS, HEAD_DIM
    h = x
    for l in range(DEPTH):
        hn = rms_norm(h, norm_in_g[l])
        proj = hn @ w_in[l]
        (q, kc, vc, ksl, vsl, kwn, vwn, gate_logits, z_nsa, u, v, z_gmlp) = split_columns(proj)
        k_cmp = compress_blocks(kc.reshape(B, T, G, dh), w_cmp_k1[l], w_cmp_k2[l], pe_cmp_k[l])
        v_cmp = compress_blocks(vc.reshape(B, T, G, dh), w_cmp_v1[l], w_cmp_v2[l], pe_cmp_v[l])
        gates = jax.nn.sigmoid(gate_logits).reshape(B, T, NSA_HEADS, 3)
        o_nsa = nsa_attention(q.reshape(B, T, NSA_HEADS, dh), k_cmp, v_cmp,
                              ksl.reshape(B, T, G, dh), vsl.reshape(B, T, G, dh),
                              kwn.reshape(B, T, G, dh), vwn.reshape(B, T, G, dh), gates)
        o_nsa = o_nsa * jax.nn.silu(z_nsa)
        o_gmlp = spatial_gating(jax.nn.gelu(u), jax.nn.gelu(v), gmlp_ln_g[l], gmlp_ln_b[l],
                                w_spatial[l], b_spatial[l]) * jax.nn.silu(z_gmlp)
        mix = jnp.concatenate([o_nsa, o_gmlp], axis=-1)
        h = h + mix @ w_out[l]
    return rms_norm(h, norm_f_g)
```

```python
import functools
import math

import jax
import jax.numpy as jnp
from jax import lax
from jax.experimental import pallas as pl
from jax.experimental.pallas import tpu as pltpu

F32 = jnp.float32
BF16 = jnp.bfloat16

HEAD_DIM = 64
NSA_HEADS = 8
NSA_KV_HEADS = 2
HEADS_PER_GROUP = NSA_HEADS // NSA_KV_HEADS
NSA_WIDTH = NSA_HEADS * HEAD_DIM
KV_WIDTH = NSA_KV_HEADS * HEAD_DIM
N_GATES = 3 * NSA_HEADS
GMLP_GROUPS = 8
GMLP_GROUP_DIM = 64
GMLP_WIDTH = GMLP_GROUPS * GMLP_GROUP_DIM
GMLP_CHUNK = 128
CMP_BLOCK = 32
CMP_STRIDE = 16
CMP_HIDDEN = 128
SEL_BLOCK = 64
SEL_TOP_N = 16
WINDOW = 512
NORM_EPS = 1e-6
NEG_INF = -1e30
FORCE_BONUS = 1e4

Q_TILE = 128
KEY_CHUNK = 128
LANES = HEADS_PER_GROUP * Q_TILE
GATE_ROWS = 16
PROJ_TILE = 512
VMEM_LIMIT = 48 * 1024 * 1024


def _sigmoid(x):
    return 1.0 / (1.0 + jnp.exp(-x))


def _gelu_tanh(x):
    c = math.sqrt(2.0 / math.pi)
    return x * (0.5 * (1.0 + jnp.tanh(c * (x + 0.044715 * (x * x * x)))))


def _proj_kernel(x_ref, g_ref, wn_ref, wt_ref, lng_ref, lnb_ref, ws_ref, bs_ref,
                 q_ref, kc_ref, vc_ref, ksw_ref, zn_ref, vt_ref, gt_ref, og_ref):
    tm = x_ref.shape[1]
    n_ck = tm // GMLP_CHUNK
    x = x_ref[0]
    ms = jnp.mean(x * x, axis=-1, keepdims=True)
    hn = ((x * lax.rsqrt(ms + NORM_EPS)) * g_ref[...]).astype(BF16)
    pn = jnp.dot(hn, wn_ref[...], preferred_element_type=F32)
    q_ref[0] = (pn[:, 0:512] * (HEAD_DIM ** -0.5)).astype(BF16)
    kc_ref[0] = pn[:, 512:640].astype(BF16)
    vc_ref[0] = pn[:, 640:768].astype(BF16)
    ksw_ref[0] = pn[:, 768:1024].astype(BF16)
    z = pn[:, 1024:1536]
    zn_ref[0] = (z * _sigmoid(z)).astype(BF16)
    pt = lax.dot_general(wt_ref[...], hn, (((1,), (1,)), ((), ())),
                         preferred_element_type=F32)
    vt = pt[0:256].astype(BF16)
    gt = _sigmoid(pt[256:288])
    for ck in range(n_ck):
        sl = slice(ck * GMLP_CHUNK, (ck + 1) * GMLP_CHUNK)
        vt_ref[0, ck] = vt[:, sl]
        gt_ref[0, ck] = gt[:, sl]
    gu = _gelu_tanh(pt[288:800])
    gv = _gelu_tanh(pt[800:1312]).reshape(GMLP_GROUPS, GMLP_GROUP_DIM, tm)
    zg = pt[1312:1824]
    mu = jnp.mean(gv, axis=1, keepdims=True)
    dv = gv - mu
    var = jnp.mean(dv * dv, axis=1, keepdims=True)
    vn = (dv * lax.rsqrt(var + NORM_EPS)) * lng_ref[...] + lnb_ref[...]
    s_i = lax.broadcasted_iota(jnp.int32, (GMLP_CHUNK, GMLP_CHUNK), 0)
    t_i = lax.broadcasted_iota(jnp.int32, (GMLP_CHUNK, GMLP_CHUNK), 1)
    causal = s_i <= t_i
    outs = []
    for g in range(GMLP_GROUPS):
        a = vn[g].astype(BF16)
        a_st = jnp.concatenate(
            [a[:, ck * GMLP_CHUNK:(ck + 1) * GMLP_CHUNK] for ck in range(n_ck)], axis=0)
        w = jnp.where(causal, ws_ref[g], 0.0).astype(BF16)
        r = jnp.dot(a_st, w, preferred_element_type=F32) + bs_ref[g]
        outs.append(jnp.concatenate(
            [r[ck * GMLP_GROUP_DIM:(ck + 1) * GMLP_GROUP_DIM] for ck in range(n_ck)], axis=1))
    mixed = jnp.concatenate(outs, axis=0)
    og_t = (gu * mixed) * (zg * _sigmoid(zg))
    og_ref[0] = og_t.T.astype(BF16)


def _proj_call(x, g_in, wn, wt, lng, lnb, ws_t, bs, tm):
    B, T, D = x.shape
    n_ck = tm // GMLP_CHUNK
    grid = (B, T // tm)
    const = lambda *shape: pl.BlockSpec(shape, lambda b, i: (0,) * len(shape))
    row = lambda w: pl.BlockSpec((1, tm, w), lambda b, i: (b, i, 0))
    out_shape = (
        jax.ShapeDtypeStruct((B, T, NSA_WIDTH), BF16),
        jax.ShapeDtypeStruct((B, T, KV_WIDTH), BF16),
        jax.ShapeDtypeStruct((B, T, KV_WIDTH), BF16),
        jax.ShapeDtypeStruct((B, T, 2 * KV_WIDTH), BF16),
        jax.ShapeDtypeStruct((B, T, NSA_WIDTH), BF16),
        jax.ShapeDtypeStruct((B, T // GMLP_CHUNK, 256, GMLP_CHUNK), BF16),
        jax.ShapeDtypeStruct((B, T // GMLP_CHUNK, 2 * GATE_ROWS, GMLP_CHUNK), F32),
        jax.ShapeDtypeStruct((B, T, GMLP_WIDTH), BF16),
    )
    out_specs = (
        row(NSA_WIDTH), row(KV_WIDTH), row(KV_WIDTH), row(2 * KV_WIDTH), row(NSA_WIDTH),
        pl.BlockSpec((1, n_ck, 256, GMLP_CHUNK), lambda b, i: (b, i, 0, 0)),
        pl.BlockSpec((1, n_ck, 2 * GATE_ROWS, GMLP_CHUNK), lambda b, i: (b, i, 0, 0)),
        row(GMLP_WIDTH),
    )
    in_specs = [
        pl.BlockSpec((1, tm, D), lambda b, i: (b, i, 0)),
        const(1, D), const(*wn.shape), const(*wt.shape),
        const(*lng.shape), const(*lnb.shape), const(*ws_t.shape), const(*bs.shape),
    ]
    return pl.pallas_call(
        _proj_kernel, out_shape=out_shape, grid=grid, in_specs=in_specs, out_specs=out_specs,
        name="proj",
        compiler_params=pltpu.CompilerParams(
            dimension_semantics=("arbitrary", "arbitrary"), vmem_limit_bytes=VMEM_LIMIT),
    )(x, g_in, wn, wt, lng, lnb, ws_t, bs)


def _compress_kernel(kc_ref, vc_ref, pek_ref, pev_ref, wk1_ref, wv1_ref, wk2_ref, wv2t_ref,
                     kcmp_ref, vcmpt_ref):
    n_c = kc_ref.shape[1]
    half = 2 * CMP_HIDDEN

    def hidden(src_ref, pe_ref, w1_ref):
        xs = src_ref[0].astype(F32)
        xa = (xs + pe_ref[0:1, :]).astype(BF16)
        xb = (xs + pe_ref[1:2, :]).astype(BF16)
        ha = jnp.dot(xa, w1_ref[:, 0:half], preferred_element_type=F32)
        hb = jnp.dot(xb, w1_ref[:, half:2 * half], preferred_element_type=F32)
        hb_next = pltpu.roll(hb, shift=n_c - 1, axis=0)
        return _gelu_tanh(ha + hb_next).astype(BF16)

    hk = hidden(kc_ref, pek_ref, wk1_ref)
    kcmp_ref[0] = jnp.dot(hk, wk2_ref[...], preferred_element_type=F32).astype(BF16)
    hv = hidden(vc_ref, pev_ref, wv1_ref)
    vcmpt_ref[0] = lax.dot_general(wv2t_ref[...], hv, (((1,), (1,)), ((), ())),
                                   preferred_element_type=F32).astype(BF16)


def _compress_call(kc2, vc2, pek, pev, wk1, wv1, wk2, wv2t):
    B, n_c, W = kc2.shape
    const = lambda a: pl.BlockSpec(a.shape, lambda b: (0,) * a.ndim)
    return pl.pallas_call(
        _compress_kernel,
        out_shape=(jax.ShapeDtypeStruct((B, n_c, KV_WIDTH), BF16),
                   jax.ShapeDtypeStruct((B, KV_WIDTH, n_c), BF16)),
        grid=(B,),
        in_specs=[pl.BlockSpec((1, n_c, W), lambda b: (b, 0, 0)),
                  pl.BlockSpec((1, n_c, W), lambda b: (b, 0, 0)),
                  const(pek), const(pev), const(wk1), const(wv1), const(wk2), const(wv2t)],
        out_specs=(pl.BlockSpec((1, n_c, KV_WIDTH), lambda b: (b, 0, 0)),
                   pl.BlockSpec((1, KV_WIDTH, n_c), lambda b: (b, 0, 0))),
        name="compress",
        compiler_params=pltpu.CompilerParams(
            dimension_semantics=("arbitrary",), vmem_limit_bytes=VMEM_LIMIT),
    )(kc2, vc2, pek, pev, wk1, wv1, wk2, wv2t)


def _attn_kernel(q_ref, kcmp_ref, vcmpt_ref, ks_ref, kw_ref, vst_ref, vwt_ref, gt_ref, zn_ref,
                 o_ref, bias_scr):
    g = pl.program_id(1)
    T = q_ref.shape[1]
    n_c = kcmp_ref.shape[1]
    n_blk = T // SEL_BLOCK
    n_tiles = T // Q_TILE
    win_chunks = WINDOW // KEY_CHUNK

    kcmp = kcmp_ref[0]
    vct = vcmpt_ref[0]
    jb_c = lax.broadcasted_iota(jnp.int32, (n_blk, n_c), 0)
    ic_c = lax.broadcasted_iota(jnp.int32, (n_blk, n_c), 1)
    overlap_t = jnp.where(ic_c * CMP_STRIDE < (jb_c + 1) * SEL_BLOCK,
                          jnp.where(ic_c * CMP_STRIDE + (CMP_BLOCK - 1) >= jb_c * SEL_BLOCK, 1.0, 0.0),
                          0.0).astype(BF16)
    row_group = lax.broadcasted_iota(jnp.int32, (KV_WIDTH, LANES), 0) >> (HEAD_DIM.bit_length() - 1)
    lane_q = lax.broadcasted_iota(jnp.int32, (1, LANES), 1) & (Q_TILE - 1)
    key_row = lax.broadcasted_iota(jnp.int32, (KEY_CHUNK, LANES), 0)
    cmp_end = lax.broadcasted_iota(jnp.int32, (n_c, LANES), 0) * CMP_STRIDE + (CMP_BLOCK - 1)
    blk_row = lax.broadcasted_iota(jnp.int32, (n_blk, Q_TILE), 0)

    def tile(c, _):
        t0 = pl.multiple_of(c * Q_TILE, Q_TILE)
        tq = t0 + lane_q
        qt = q_ref[0, pl.ds(t0, Q_TILE), :].astype(F32).T
        b64 = jnp.concatenate(
            [qt[h * HEAD_DIM:(h + 1) * HEAD_DIM] for h in range(HEADS_PER_GROUP)], axis=1)
        qmat = jnp.where(row_group == g, jnp.concatenate([b64, b64], axis=0), 0.0).astype(BF16)

        s = jnp.dot(kcmp, qmat, preferred_element_type=F32)
        valid_c = cmp_end <= tq
        s = jnp.where(valid_c, s, NEG_INF)
        m_c = jnp.max(s, axis=0, keepdims=True)
        e = jnp.where(valid_c, jnp.exp(s - m_c), 0.0)
        l_c = jnp.sum(e, axis=0, keepdims=True)
        linv_c = jnp.where(l_c > 0.0, 1.0 / l_c, 0.0)
        e16 = e.astype(BF16)
        o_c = jnp.dot(vct, e16, preferred_element_type=F32) * linv_c
        imp4 = jnp.dot(overlap_t, e16, preferred_element_type=F32) * linv_c
        imp = imp4[:, 0:Q_TILE]
        for h in range(1, HEADS_PER_GROUP):
            imp = imp + imp4[:, h * Q_TILE:(h + 1) * Q_TILE]

        cur = (t0 + lax.broadcasted_iota(jnp.int32, (1, Q_TILE), 1)) >> (SEL_BLOCK.bit_length() - 1)
        valid_b = blk_row <= cur
        bonus = jnp.where(blk_row == 0, FORCE_BONUS,
                          jnp.where(blk_row == cur, FORCE_BONUS,
                                    jnp.where(blk_row == cur - 1, FORCE_BONUS, 0.0)))
        score = jnp.where(valid_b, imp + bonus, -1.0)
        n_rb = n_blk // 8
        sblk = [score[8 * r:8 * r + 8] for r in range(n_rb)]
        cnt = [jnp.zeros((8, Q_TILE), F32) for _ in range(n_rb)]
        sub_row = lax.broadcasted_iota(jnp.int32, (8, Q_TILE), 0)
        for j in range(n_blk):
            rj = jnp.broadcast_to(score[j:j + 1], (8, Q_TILE))
            for r in range(n_rb):
                if 8 * r + 7 <= j:
                    cnt[r] = cnt[r] + jnp.where(rj > sblk[r], 1.0, 0.0)
                elif 8 * r > j:
                    cnt[r] = cnt[r] + jnp.where(rj >= sblk[r], 1.0, 0.0)
                else:
                    cnt[r] = cnt[r] + jnp.where(sub_row + 8 * r > j,
                                                jnp.where(rj >= sblk[r], 1.0, 0.0),
                                                jnp.where(rj > sblk[r], 1.0, 0.0))
        for r in range(n_rb):
            keep = jnp.where(valid_b[8 * r:8 * r + 8],
                             jnp.where(cnt[r] < float(SEL_TOP_N), 0.0, NEG_INF), NEG_INF)
            keep4 = jnp.concatenate([keep] * HEADS_PER_GROUP, axis=1)
            for jj in range(8):
                bias_scr[8 * r + jj] = jnp.broadcast_to(keep4[jj:jj + 1], (8, LANES))

        def attend(k_ref, vt_ref, i, carry, mask_fn):
            m, l, acc = carry
            k0 = pl.multiple_of(i * KEY_CHUNK, KEY_CHUNK)
            kk = k_ref[0, pl.ds(k0, KEY_CHUNK), :]
            sc = mask_fn(jnp.dot(kk, qmat, preferred_element_type=F32), i, k0)
            m_new = jnp.maximum(m, jnp.max(sc, axis=0, keepdims=True))
            alpha = jnp.exp(m - m_new)
            p = jnp.exp(sc - m_new)
            l = alpha * l + jnp.sum(p, axis=0, keepdims=True)
            acc = alpha * acc + jnp.dot(vt_ref[0, i], p.astype(BF16), preferred_element_type=F32)
            return m_new, l, acc

        init = (jnp.full((1, LANES), NEG_INF, F32), jnp.zeros((1, LANES), F32),
                jnp.zeros((HEAD_DIM, LANES), F32))

        def sel_bias(sc, i):
            rows = SEL_BLOCK // 8
            b0 = jnp.concatenate([bias_scr[2 * i]] * rows, axis=0)
            b1 = jnp.concatenate([bias_scr[2 * i + 1]] * rows, axis=0)
            return sc + jnp.concatenate([b0, b1], axis=0)

        def sel_mask_inner(sc, i, k0):
            return sel_bias(sc, i)

        def sel_mask_diag(sc, i, k0):
            return jnp.where(k0 + key_row <= tq, sel_bias(sc, i), NEG_INF)

        carry = lax.fori_loop(
            0, c, lambda i, cr: attend(ks_ref, vst_ref, i, cr, sel_mask_inner), init)
        m_s, l_s, acc_s = attend(ks_ref, vst_ref, c, carry, sel_mask_diag)

        def win_mask(sc, i, k0):
            pos = k0 + key_row
            return jnp.where(pos <= tq, jnp.where(pos > tq - WINDOW, sc, NEG_INF), NEG_INF)

        m_w, l_w, acc_w = lax.fori_loop(
            jnp.maximum(c - win_chunks, 0), c + 1,
            lambda i, cr: attend(kw_ref, vwt_ref, i, cr, win_mask), init)

        gt = gt_ref[0, c]

        def gate_row(br):
            return jnp.concatenate(
                [gt[4 * br + h:4 * br + h + 1] for h in range(HEADS_PER_GROUP)], axis=1)

        o_t = (gate_row(0) * o_c + gate_row(1) * (acc_s / l_s) + gate_row(2) * (acc_w / l_w))
        stacked = jnp.concatenate(
            [o_t[:, h * Q_TILE:(h + 1) * Q_TILE] for h in range(HEADS_PER_GROUP)], axis=0)
        o = stacked.T * zn_ref[0, pl.ds(t0, Q_TILE), :].astype(F32)
        o_ref[0, pl.ds(t0, Q_TILE), :] = o.astype(BF16)
        return 0

    lax.fori_loop(0, n_tiles, tile, 0)


def _attn_call(q, kcmp, vcmpt, ksw, vt, gt, zn):
    B, T, _ = q.shape
    n_c = kcmp.shape[1]
    n_ck = T // KEY_CHUNK
    G = NSA_KV_HEADS
    gw = HEADS_PER_GROUP * HEAD_DIM
    in_specs = [
        pl.BlockSpec((1, T, gw), lambda b, g: (b, 0, g)),
        pl.BlockSpec((1, n_c, KV_WIDTH), lambda b, g: (b, 0, 0)),
        pl.BlockSpec((1, HEAD_DIM, n_c), lambda b, g: (b, g, 0)),
        pl.BlockSpec((1, T, KV_WIDTH), lambda b, g: (b, 0, 0)),
        pl.BlockSpec((1, T, KV_WIDTH), lambda b, g: (b, 0, 1)),
        pl.BlockSpec((1, n_ck, HEAD_DIM, KEY_CHUNK), lambda b, g: (b, 0, g, 0)),
        pl.BlockSpec((1, n_ck, HEAD_DIM, KEY_CHUNK), lambda b, g: (b, 0, G + g, 0)),
        pl.BlockSpec((1, n_ck, GATE_ROWS, KEY_CHUNK), lambda b, g: (b, 0, g, 0)),
        pl.BlockSpec((1, T, gw), lambda b, g: (b, 0, g)),
    ]
    return pl.pallas_call(
        _attn_kernel,
        out_shape=jax.ShapeDtypeStruct((B, T, NSA_WIDTH), BF16),
        grid=(B, G),
        in_specs=in_specs,
        out_specs=pl.BlockSpec((1, T, gw), lambda b, g: (b, 0, g)),
        scratch_shapes=[pltpu.VMEM((T // SEL_BLOCK, 8, LANES), F32)],
        name="attn",
        compiler_params=pltpu.CompilerParams(
            dimension_semantics=("arbitrary", "arbitrary"), vmem_limit_bytes=VMEM_LIMIT),
    )(q, kcmp, vcmpt, ksw, ksw, vt, vt, gt, zn)


def _out_kernel(x_ref, mn_ref, mg_ref, wo_ref, g_ref, o_ref):
    h = x_ref[0]
    h = h + jnp.dot(mn_ref[0], wo_ref[0:NSA_WIDTH, :], preferred_element_type=F32)
    h = h + jnp.dot(mg_ref[0], wo_ref[NSA_WIDTH:, :], preferred_element_type=F32)
    ms = jnp.mean(h * h, axis=-1, keepdims=True)
    o_ref[0] = (h * lax.rsqrt(ms + NORM_EPS)) * g_ref[...]


def _out_call(x, mix_nsa, mix_gmlp, wo, g_f, tm):
    B, T, D = x.shape
    row = lambda w: pl.BlockSpec((1, tm, w), lambda b, i: (b, i, 0))
    return pl.pallas_call(
        _out_kernel,
        out_shape=jax.ShapeDtypeStruct((B, T, D), x.dtype),
        grid=(B, T // tm),
        in_specs=[row(D), row(NSA_WIDTH), row(GMLP_WIDTH),
                  pl.BlockSpec(wo.shape, lambda b, i: (0, 0)),
                  pl.BlockSpec((1, D), lambda b, i: (0, 0))],
        out_specs=row(D),
        name="out",
        compiler_params=pltpu.CompilerParams(
            dimension_semantics=("arbitrary", "arbitrary"), vmem_limit_bytes=VMEM_LIMIT),
    )(x, mix_nsa, mix_gmlp, wo, g_f)


def _split_w_in(w):
    sizes = (NSA_WIDTH, KV_WIDTH, KV_WIDTH, KV_WIDTH, KV_WIDTH, KV_WIDTH, KV_WIDTH, N_GATES,
             NSA_WIDTH, GMLP_WIDTH, GMLP_WIDTH, GMLP_WIDTH)
    parts, off = [], 0
    for s in sizes:
        parts.append(w[:, off:off + s])
        off += s
    return parts


def _gate_columns(w_gate):
    d = w_gate.shape[0]
    wg = w_gate.reshape(d, NSA_KV_HEADS, HEADS_PER_GROUP, 3).transpose(0, 1, 3, 2)
    wg = wg.reshape(d, NSA_KV_HEADS, 3 * HEADS_PER_GROUP)
    wg = jnp.pad(wg, ((0, 0), (0, 0), (0, GATE_ROWS - 3 * HEADS_PER_GROUP)))
    return wg.reshape(d, NSA_KV_HEADS * GATE_ROWS)


def _compress_w1(w1):
    half = CMP_BLOCK // 2
    w = w1.reshape(2, half, HEAD_DIM, CMP_HIDDEN)
    eye = jnp.eye(NSA_KV_HEADS, dtype=w1.dtype)
    full = jnp.einsum('hlde,gk->lgdhke', w, eye)
    return full.reshape(half * KV_WIDTH, 2 * NSA_KV_HEADS * CMP_HIDDEN)


def _compress_pe(pe):
    half = CMP_BLOCK // 2
    p = pe.reshape(2, half, 1, HEAD_DIM)
    p = jnp.broadcast_to(p, (2, half, NSA_KV_HEADS, HEAD_DIM))
    return p.reshape(2, half * KV_WIDTH)


def _block_diag2(w2):
    z = jnp.zeros_like(w2)
    return jnp.concatenate([jnp.concatenate([w2, z], axis=1),
                            jnp.concatenate([z, w2], axis=1)], axis=0)


def kernel(x, norm_in_g, w_in, w_cmp_k1, w_cmp_k2, pe_cmp_k, w_cmp_v1, w_cmp_v2, pe_cmp_v,
           gmlp_ln_g, gmlp_ln_b, w_spatial, b_spatial, w_out, norm_f_g):
    B, T, D = x.shape
    assert w_in.shape[0] == 1, "single-layer block"
    tm = min(PROJ_TILE, T)
    (wq, wkc, wvc, wks, wvs, wkw, wvw, wgt, wzn, wu, wv, wzg) = _split_w_in(w_in[0])
    wn = jnp.concatenate([wq, wkc, wvc, wks, wkw, wzn], axis=1).astype(BF16)
    wt = jnp.concatenate([wvs, wvw, _gate_columns(wgt), wu, wv, wzg], axis=1).T.astype(BF16)
    lng = gmlp_ln_g[0].reshape(GMLP_GROUPS, GMLP_GROUP_DIM, 1)
    lnb = gmlp_ln_b[0].reshape(GMLP_GROUPS, GMLP_GROUP_DIM, 1)
    ws_t = jnp.swapaxes(w_spatial[0], 1, 2)
    bs = b_spatial[0].reshape(GMLP_GROUPS, 1, GMLP_CHUNK)
    q, kc, vc, ksw, zn, vt, gt, o_gmlp = _proj_call(
        x, norm_in_g[0].reshape(1, D), wn, wt, lng, lnb, ws_t, bs, tm)

    n_c = T // CMP_STRIDE
    kcmp, vcmpt = _compress_call(
        kc.reshape(B, n_c, CMP_STRIDE * KV_WIDTH), vc.reshape(B, n_c, CMP_STRIDE * KV_WIDTH),
        _compress_pe(pe_cmp_k[0]), _compress_pe(pe_cmp_v[0]),
        _compress_w1(w_cmp_k1[0]).astype(BF16), _compress_w1(w_cmp_v1[0]).astype(BF16),
        _block_diag2(w_cmp_k2[0]).astype(BF16), _block_diag2(w_cmp_v2[0]).T.astype(BF16))

    o_nsa = _attn_call(q, kcmp, vcmpt, ksw, vt, gt, zn)
    return _out_call(x, o_nsa, o_gmlp, w_out[0].astype(BF16), norm_f_g.reshape(1, D), tm)
```

```python
import functools
import math

import jax
import jax.numpy as jnp
from jax import lax
from jax.experimental import pallas as pl
from jax.experimental.pallas import tpu as pltpu

F32 = jnp.float32
BF16 = jnp.bfloat16

HEAD_DIM = 64
NSA_HEADS = 8
NSA_KV_HEADS = 2
HEADS_PER_GROUP = NSA_HEADS // NSA_KV_HEADS
NSA_WIDTH = NSA_HEADS * HEAD_DIM
KV_WIDTH = NSA_KV_HEADS * HEAD_DIM
N_GATES = 3 * NSA_HEADS
GMLP_GROUPS = 8
GMLP_GROUP_DIM = 64
GMLP_WIDTH = GMLP_GROUPS * GMLP_GROUP_DIM
GMLP_CHUNK = 128
CMP_BLOCK = 32
CMP_STRIDE = 16
CMP_HIDDEN = 128
SEL_BLOCK = 64
SEL_TOP_N = 16
WINDOW = 512
NORM_EPS = 1e-6
NEG_INF = -1e30
FORCE_BONUS = 1e4

Q_TILE = 128
KEY_CHUNK = 128
LANES = HEADS_PER_GROUP * Q_TILE
SEL_KEYS = 512
WIN_KEYS = WINDOW + Q_TILE
ONES_ROWS = 16
LOG2_E = math.log2(math.e)
GATE_ROWS = 16
PROJ_TILE = 512
VMEM_LIMIT = 48 * 1024 * 1024


def _sigmoid(x):
    return 1.0 / (1.0 + jnp.exp(-x))


def _gelu_tanh(x):
    c = math.sqrt(2.0 / math.pi)
    return x * (0.5 * (1.0 + jnp.tanh(c * (x + 0.044715 * (x * x * x)))))


def _proj_kernel(x_ref, g_ref, wn_ref, wt_ref, lng_ref, lnb_ref, ws_ref, bs_ref,
                 q_ref, kc_ref, vc_ref, ksw_ref, zn_ref, vt_ref, gt_ref, og_ref):
    tm = x_ref.shape[1]
    n_ck = tm // GMLP_CHUNK
    x = x_ref[0]
    ms = jnp.mean(x * x, axis=-1, keepdims=True)
    hn = ((x * lax.rsqrt(ms + NORM_EPS)) * g_ref[...]).astype(BF16)
    pn = jnp.dot(hn, wn_ref[...], preferred_element_type=F32)
    q_ref[0] = (pn[:, 0:512] * (HEAD_DIM ** -0.5 * LOG2_E)).astype(BF16)
    kc_ref[0] = pn[:, 512:640].astype(BF16)
    vc_ref[0] = pn[:, 640:768].astype(BF16)
    ksw_ref[0] = pn[:, 768:1024].astype(BF16)
    z = pn[:, 1024:1536]
    zn_ref[0] = (z * _sigmoid(z)).astype(BF16)
    pt = lax.dot_general(wt_ref[...], hn, (((1,), (1,)), ((), ())),
                         preferred_element_type=F32)
    vt = pt[0:256].astype(BF16)
    gt = _sigmoid(pt[256:288])
    for ck in range(n_ck):
        sl = slice(ck * GMLP_CHUNK, (ck + 1) * GMLP_CHUNK)
        vt_ref[0, ck] = vt[:, sl]
        gt_ref[0, ck] = gt[:, sl]
    gu = _gelu_tanh(pt[288:800])
    gv = _gelu_tanh(pt[800:1312]).reshape(GMLP_GROUPS, GMLP_GROUP_DIM, tm)
    zg = pt[1312:1824]
    mu = jnp.mean(gv, axis=1, keepdims=True)
    dv = gv - mu
    var = jnp.mean(dv * dv, axis=1, keepdims=True)
    vn = (dv * lax.rsqrt(var + NORM_EPS)) * lng_ref[...] + lnb_ref[...]
    s_i = lax.broadcasted_iota(jnp.int32, (GMLP_CHUNK, GMLP_CHUNK), 0)
    t_i = lax.broadcasted_iota(jnp.int32, (GMLP_CHUNK, GMLP_CHUNK), 1)
    causal = s_i <= t_i
    outs = []
    for g in range(GMLP_GROUPS):
        a = vn[g].astype(BF16)
        a_st = jnp.concatenate(
            [a[:, ck * GMLP_CHUNK:(ck + 1) * GMLP_CHUNK] for ck in range(n_ck)], axis=0)
        w = jnp.where(causal, ws_ref[g], 0.0).astype(BF16)
        r = jnp.dot(a_st, w, preferred_element_type=F32) + bs_ref[g]
        outs.append(jnp.concatenate(
            [r[ck * GMLP_GROUP_DIM:(ck + 1) * GMLP_GROUP_DIM] for ck in range(n_ck)], axis=1))
    mixed = jnp.concatenate(outs, axis=0)
    og_t = (gu * mixed) * (zg * _sigmoid(zg))
    og_ref[0] = og_t.T.astype(BF16)


def _proj_call(x, g_in, wn, wt, lng, lnb, ws_t, bs, tm):
    B, T, D = x.shape
    n_ck = tm // GMLP_CHUNK
    grid = (B, T // tm)
    const = lambda *shape: pl.BlockSpec(shape, lambda b, i: (0,) * len(shape))
    row = lambda w: pl.BlockSpec((1, tm, w), lambda b, i: (b, i, 0))
    out_shape = (
        jax.ShapeDtypeStruct((B, T, NSA_WIDTH), BF16),
        jax.ShapeDtypeStruct((B, T, KV_WIDTH), BF16),
        jax.ShapeDtypeStruct((B, T, KV_WIDTH), BF16),
        jax.ShapeDtypeStruct((B, T, 2 * KV_WIDTH), BF16),
        jax.ShapeDtypeStruct((B, T, NSA_WIDTH), BF16),
        jax.ShapeDtypeStruct((B, T // GMLP_CHUNK, 256, GMLP_CHUNK), BF16),
        jax.ShapeDtypeStruct((B, T // GMLP_CHUNK, 2 * GATE_ROWS, GMLP_CHUNK), F32),
        jax.ShapeDtypeStruct((B, T, GMLP_WIDTH), BF16),
    )
    out_specs = (
        row(NSA_WIDTH), row(KV_WIDTH), row(KV_WIDTH), row(2 * KV_WIDTH), row(NSA_WIDTH),
        pl.BlockSpec((1, n_ck, 256, GMLP_CHUNK), lambda b, i: (b, i, 0, 0)),
        pl.BlockSpec((1, n_ck, 2 * GATE_ROWS, GMLP_CHUNK), lambda b, i: (b, i, 0, 0)),
        row(GMLP_WIDTH),
    )
    in_specs = [
        pl.BlockSpec((1, tm, D), lambda b, i: (b, i, 0)),
        const(1, D), const(*wn.shape), const(*wt.shape),
        const(*lng.shape), const(*lnb.shape), const(*ws_t.shape), const(*bs.shape),
    ]
    return pl.pallas_call(
        _proj_kernel, out_shape=out_shape, grid=grid, in_specs=in_specs, out_specs=out_specs,
        name="proj",
        compiler_params=pltpu.CompilerParams(
            dimension_semantics=("arbitrary", "arbitrary"), vmem_limit_bytes=VMEM_LIMIT),
    )(x, g_in, wn, wt, lng, lnb, ws_t, bs)


def _compress_kernel(kc_ref, vc_ref, pek_ref, pev_ref, wk1_ref, wv1_ref, wk2_ref, wv2t_ref,
                     kcmp_ref, vcmpt_ref):
    n_c = kc_ref.shape[1]
    half = 2 * CMP_HIDDEN

    def hidden(src_ref, pe_ref, w1_ref):
        xs = src_ref[0].astype(F32)
        xa = (xs + pe_ref[0:1, :]).astype(BF16)
        xb = (xs + pe_ref[1:2, :]).astype(BF16)
        ha = jnp.dot(xa, w1_ref[:, 0:half], preferred_element_type=F32)
        hb = jnp.dot(xb, w1_ref[:, half:2 * half], preferred_element_type=F32)
        hb_next = pltpu.roll(hb, shift=n_c - 1, axis=0)
        return _gelu_tanh(ha + hb_next).astype(BF16)

    hk = hidden(kc_ref, pek_ref, wk1_ref)
    kcmp_ref[0] = jnp.dot(hk, wk2_ref[...], preferred_element_type=F32).astype(BF16)
    hv = hidden(vc_ref, pev_ref, wv1_ref)
    vcmpt_ref[0] = lax.dot_general(wv2t_ref[...], hv, (((1,), (1,)), ((), ())),
                                   preferred_element_type=F32).astype(BF16)


def _compress_call(kc2, vc2, pek, pev, wk1, wv1, wk2, wv2t):
    B, n_c, W = kc2.shape
    const = lambda a: pl.BlockSpec(a.shape, lambda b: (0,) * a.ndim)
    return pl.pallas_call(
        _compress_kernel,
        out_shape=(jax.ShapeDtypeStruct((B, n_c, KV_WIDTH), BF16),
                   jax.ShapeDtypeStruct((B, KV_WIDTH, n_c), BF16)),
        grid=(B,),
        in_specs=[pl.BlockSpec((1, n_c, W), lambda b: (b, 0, 0)),
                  pl.BlockSpec((1, n_c, W), lambda b: (b, 0, 0)),
                  const(pek), const(pev), const(wk1), const(wv1), const(wk2), const(wv2t)],
        out_specs=(pl.BlockSpec((1, n_c, KV_WIDTH), lambda b: (b, 0, 0)),
                   pl.BlockSpec((1, KV_WIDTH, n_c), lambda b: (b, 0, 0))),
        name="compress",
        compiler_params=pltpu.CompilerParams(
            dimension_semantics=("arbitrary",), vmem_limit_bytes=VMEM_LIMIT),
    )(kc2, vc2, pek, pev, wk1, wv1, wk2, wv2t)


def _attn_kernel(q_ref, kcmp_ref, vcmpt_ref, ks_ref, kw_ref, vst_ref, vwt_ref, gt_ref, zn_ref,
                 o_ref, bias_scr, tri_scr, p_scr):
    g = pl.program_id(1)
    T = q_ref.shape[1]
    n_c = kcmp_ref.shape[1]
    n_blk = T // SEL_BLOCK
    n_tiles = T // Q_TILE

    def with_ones(vt):
        return jnp.concatenate([vt, jnp.ones((ONES_ROWS, vt.shape[1]), vt.dtype)], axis=0)

    kcmp = kcmp_ref[0]
    vct = vcmpt_ref[0]
    jb_c = lax.broadcasted_iota(jnp.int32, (n_blk, n_c), 0)
    ic_c = lax.broadcasted_iota(jnp.int32, (n_blk, n_c), 1)
    overlap_t = jnp.where(ic_c * CMP_STRIDE < (jb_c + 1) * SEL_BLOCK,
                          jnp.where(ic_c * CMP_STRIDE + (CMP_BLOCK - 1) >= jb_c * SEL_BLOCK, 1.0, 0.0),
                          0.0).astype(BF16)
    row_group = lax.broadcasted_iota(jnp.int32, (KV_WIDTH, LANES), 0) >> (HEAD_DIM.bit_length() - 1)
    lane_q = lax.broadcasted_iota(jnp.int32, (1, LANES), 1) & (Q_TILE - 1)
    win_row = lax.broadcasted_iota(jnp.int32, (WIN_KEYS, LANES), 0)
    key_row = lax.broadcasted_iota(jnp.int32, (KEY_CHUNK, LANES), 0)
    tri_scr[0] = jnp.where(key_row <= lane_q, 0.0, NEG_INF)
    tri_scr[1] = jnp.where(key_row > lane_q, 0.0, NEG_INF)
    cmp_end = lax.broadcasted_iota(jnp.int32, (n_c, LANES), 0) * CMP_STRIDE + (CMP_BLOCK - 1)
    blk_row = lax.broadcasted_iota(jnp.int32, (n_blk, Q_TILE), 0)

    def tile(c, full_window):
        t0 = pl.multiple_of(c * Q_TILE, Q_TILE)
        tq = t0 + lane_q
        qt = q_ref[0, pl.ds(t0, Q_TILE), :].astype(F32).T
        b64 = jnp.concatenate(
            [qt[h * HEAD_DIM:(h + 1) * HEAD_DIM] for h in range(HEADS_PER_GROUP)], axis=1)
        qmat = jnp.where(row_group == g, jnp.concatenate([b64, b64], axis=0), 0.0).astype(BF16)

        s = jnp.dot(kcmp, qmat, preferred_element_type=F32)
        valid_c = cmp_end <= tq
        s = jnp.where(valid_c, s, NEG_INF)
        m_c = jnp.max(s, axis=0, keepdims=True)
        e16 = jnp.where(valid_c, jnp.exp2(s - m_c), 0.0).astype(BF16)
        acc_c = jnp.dot(with_ones(vct), e16, preferred_element_type=F32)
        l_c = acc_c[HEAD_DIM:HEAD_DIM + 1]
        linv_c = jnp.where(l_c > 0.0, 1.0 / l_c, 0.0)
        o_c = acc_c[0:HEAD_DIM] * linv_c
        imp4 = jnp.dot(overlap_t, e16, preferred_element_type=F32) * linv_c
        imp = imp4[:, 0:Q_TILE]
        for h in range(1, HEADS_PER_GROUP):
            imp = imp + imp4[:, h * Q_TILE:(h + 1) * Q_TILE]

        cur = (t0 + lax.broadcasted_iota(jnp.int32, (1, Q_TILE), 1)) >> (SEL_BLOCK.bit_length() - 1)
        valid_b = blk_row <= cur
        bonus = jnp.where(blk_row == 0, FORCE_BONUS,
                          jnp.where(blk_row == cur, FORCE_BONUS,
                                    jnp.where(blk_row == cur - 1, FORCE_BONUS, 0.0)))
        score = jnp.where(valid_b, imp + bonus, -1.0)
        n_rb = n_blk // 8
        sblk = [score[8 * r:8 * r + 8] for r in range(n_rb)]
        cnt = [jnp.zeros((8, Q_TILE), F32) for _ in range(n_rb)]
        sub_row = lax.broadcasted_iota(jnp.int32, (8, Q_TILE), 0)
        for j in range(n_blk):
            rj = jnp.broadcast_to(score[j:j + 1], (8, Q_TILE))
            for r in range(n_rb):
                if 8 * r + 7 <= j:
                    cnt[r] = cnt[r] + jnp.where(rj > sblk[r], 1.0, 0.0)
                elif 8 * r > j:
                    cnt[r] = cnt[r] + jnp.where(rj >= sblk[r], 1.0, 0.0)
                else:
                    cnt[r] = cnt[r] + jnp.where(sub_row + 8 * r > j,
                                                jnp.where(rj >= sblk[r], 1.0, 0.0),
                                                jnp.where(rj > sblk[r], 1.0, 0.0))
        for r in range(n_rb):
            keep = jnp.where(blk_row[8 * r:8 * r + 8] < c * (Q_TILE // SEL_BLOCK),
                             jnp.where(cnt[r] < float(SEL_TOP_N), 0.0, NEG_INF), NEG_INF)
            keep4 = jnp.concatenate([keep] * HEADS_PER_GROUP, axis=1)
            for jj in range(8):
                bias_scr[8 * r + jj] = jnp.broadcast_to(keep4[jj:jj + 1], (8, LANES))

        w0 = jnp.maximum(c - WINDOW // KEY_CHUNK, 0)
        wk0 = pl.multiple_of(w0 * KEY_CHUNK, KEY_CHUNK)
        sw = jnp.dot(kw_ref[0, pl.ds(wk0, WIN_KEYS), :], qmat, preferred_element_type=F32)
        if full_window:
            sw = jnp.concatenate([sw[0:KEY_CHUNK] + tri_scr[1], sw[KEY_CHUNK:WINDOW],
                                  sw[WINDOW:WIN_KEYS] + tri_scr[0]], axis=0)
        else:
            pos = wk0 + win_row
            sw = jnp.where(pos <= tq, jnp.where(pos > tq - WINDOW, sw, NEG_INF), NEG_INF)
        m_w = jnp.max(sw, axis=0, keepdims=True)
        pw = jnp.exp2(sw - m_w).astype(BF16)
        vw = jnp.concatenate([vwt_ref[0, w0 + u] for u in range(WIN_KEYS // KEY_CHUNK)], axis=1)
        acc_w = jnp.dot(with_ones(vw), pw, preferred_element_type=F32)

        sd = jnp.dot(ks_ref[0, pl.ds(t0, KEY_CHUNK), :], qmat, preferred_element_type=F32) + tri_scr[0]
        m_d = jnp.max(sd, axis=0, keepdims=True)
        pd = jnp.exp2(sd - m_d).astype(BF16)
        acc_d = jnp.dot(with_ones(vst_ref[0, c]), pd, preferred_element_type=F32)

        blocks_per_step = SEL_KEYS // SEL_BLOCK
        chunks_per_step = SEL_KEYS // KEY_CHUNK

        def sel_scores(j):
            k0 = pl.multiple_of(j * SEL_KEYS, SEL_KEYS)
            sc = jnp.dot(ks_ref[0, pl.ds(k0, SEL_KEYS), :], qmat, preferred_element_type=F32)
            sc = jnp.concatenate(
                [sc[SEL_BLOCK * jb:SEL_BLOCK * (jb + 1)]
                 + jnp.concatenate([bias_scr[blocks_per_step * j + jb]] * (SEL_BLOCK // 8), axis=0)
                 for jb in range(blocks_per_step)], axis=0)
            mx = jnp.max(sc, axis=0, keepdims=True)
            p_scr[...] = jnp.exp2(sc - mx).astype(BF16)
            return mx

        def sel_accumulate(j, m, acc, mx):
            vs = jnp.concatenate(
                [vst_ref[0, chunks_per_step * j + u] for u in range(chunks_per_step)], axis=1)
            pv = jnp.dot(with_ones(vs), p_scr[...], preferred_element_type=F32)
            m_new = jnp.maximum(m, mx)
            return m_new, jnp.exp2(m - m_new) * acc + jnp.exp2(mx - m_new) * pv

        def sel_body(j, carry):
            m, acc, mx = carry
            m, acc = sel_accumulate(j, m, acc, mx)
            return m, acc, sel_scores(j + 1)

        n_steps = jnp.maximum((t0 + SEL_KEYS - 1) // SEL_KEYS, 1)
        m_s, acc_s, mx_s = lax.fori_loop(0, n_steps - 1, sel_body, (m_d, acc_d, sel_scores(0)))
        m_s, acc_s = sel_accumulate(n_steps - 1, m_s, acc_s, mx_s)

        def normalized(acc):
            return acc[0:HEAD_DIM] / acc[HEAD_DIM:HEAD_DIM + 1]

        gt = gt_ref[0, c]

        def gate_row(br):
            return jnp.concatenate(
                [gt[4 * br + h:4 * br + h + 1] for h in range(HEADS_PER_GROUP)], axis=1)

        o_t = (gate_row(0) * o_c + gate_row(1) * normalized(acc_s) + gate_row(2) * normalized(acc_w))
        stacked = jnp.concatenate(
            [o_t[:, h * Q_TILE:(h + 1) * Q_TILE] for h in range(HEADS_PER_GROUP)], axis=0)
        o = stacked.T * zn_ref[0, pl.ds(t0, Q_TILE), :].astype(F32)
        o_ref[0, pl.ds(t0, Q_TILE), :] = o.astype(BF16)

    first_full = min(WINDOW // Q_TILE, n_tiles)
    pl.loop(0, first_full)(functools.partial(tile, full_window=False))
    pl.loop(first_full, n_tiles)(functools.partial(tile, full_window=True))


def _attn_call(q, kcmp, vcmpt, ksw, vt, gt, zn):
    B, T, _ = q.shape
    n_c = kcmp.shape[1]
    n_ck = T // KEY_CHUNK
    G = NSA_KV_HEADS
    gw = HEADS_PER_GROUP * HEAD_DIM
    in_specs = [
        pl.BlockSpec((1, T, gw), lambda b, g: (b, 0, g)),
        pl.BlockSpec((1, n_c, KV_WIDTH), lambda b, g: (b, 0, 0)),
        pl.BlockSpec((1, HEAD_DIM, n_c), lambda b, g: (b, g, 0)),
        pl.BlockSpec((1, T, KV_WIDTH), lambda b, g: (b, 0, 0)),
        pl.BlockSpec((1, T, KV_WIDTH), lambda b, g: (b, 0, 1)),
        pl.BlockSpec((1, n_ck, HEAD_DIM, KEY_CHUNK), lambda b, g: (b, 0, g, 0)),
        pl.BlockSpec((1, n_ck, HEAD_DIM, KEY_CHUNK), lambda b, g: (b, 0, G + g, 0)),
        pl.BlockSpec((1, n_ck, GATE_ROWS, KEY_CHUNK), lambda b, g: (b, 0, g, 0)),
        pl.BlockSpec((1, T, gw), lambda b, g: (b, 0, g)),
    ]
    return pl.pallas_call(
        _attn_kernel,
        out_shape=jax.ShapeDtypeStruct((B, T, NSA_WIDTH), BF16),
        grid=(B, G),
        in_specs=in_specs,
        out_specs=pl.BlockSpec((1, T, gw), lambda b, g: (b, 0, g)),
        scratch_shapes=[pltpu.VMEM((T // SEL_BLOCK, 8, LANES), F32),
                        pltpu.VMEM((2, KEY_CHUNK, LANES), F32),
                        pltpu.VMEM((SEL_KEYS, LANES), BF16)],
        name="attn",
        compiler_params=pltpu.CompilerParams(
            dimension_semantics=("arbitrary", "arbitrary"), vmem_limit_bytes=VMEM_LIMIT),
    )(q, kcmp, vcmpt, ksw, ksw, vt, vt, gt, zn)


def _out_kernel(x_ref, mn_ref, mg_ref, wo_ref, g_ref, o_ref):
    h = x_ref[0]
    h = h + jnp.dot(mn_ref[0], wo_ref[0:NSA_WIDTH, :], preferred_element_type=F32)
    h = h + jnp.dot(mg_ref[0], wo_ref[NSA_WIDTH:, :], preferred_element_type=F32)
    ms = jnp.mean(h * h, axis=-1, keepdims=True)
    o_ref[0] = (h * lax.rsqrt(ms + NORM_EPS)) * g_ref[...]


def _out_call(x, mix_nsa, mix_gmlp, wo, g_f, tm):
    B, T, D = x.shape
    row = lambda w: pl.BlockSpec((1, tm, w), lambda b, i: (b, i, 0))
    return pl.pallas_call(
        _out_kernel,
        out_shape=jax.ShapeDtypeStruct((B, T, D), x.dtype),
        grid=(B, T // tm),
        in_specs=[row(D), row(NSA_WIDTH), row(GMLP_WIDTH),
                  pl.BlockSpec(wo.shape, lambda b, i: (0, 0)),
                  pl.BlockSpec((1, D), lambda b, i: (0, 0))],
        out_specs=row(D),
        name="out",
        compiler_params=pltpu.CompilerParams(
            dimension_semantics=("arbitrary", "arbitrary"), vmem_limit_bytes=VMEM_LIMIT),
    )(x, mix_nsa, mix_gmlp, wo, g_f)


def _split_w_in(w):
    sizes = (NSA_WIDTH, KV_WIDTH, KV_WIDTH, KV_WIDTH, KV_WIDTH, KV_WIDTH, KV_WIDTH, N_GATES,
             NSA_WIDTH, GMLP_WIDTH, GMLP_WIDTH, GMLP_WIDTH)
    parts, off = [], 0
    for s in sizes:
        parts.append(w[:, off:off + s])
        off += s
    return parts


def _gate_columns(w_gate):
    d = w_gate.shape[0]
    wg = w_gate.reshape(d, NSA_KV_HEADS, HEADS_PER_GROUP, 3).transpose(0, 1, 3, 2)
    wg = wg.reshape(d, NSA_KV_HEADS, 3 * HEADS_PER_GROUP)
    wg = jnp.pad(wg, ((0, 0), (0, 0), (0, GATE_ROWS - 3 * HEADS_PER_GROUP)))
    return wg.reshape(d, NSA_KV_HEADS * GATE_ROWS)


def _compress_w1(w1):
    half = CMP_BLOCK // 2
    w = w1.reshape(2, half, HEAD_DIM, CMP_HIDDEN)
    eye = jnp.eye(NSA_KV_HEADS, dtype=w1.dtype)
    full = jnp.einsum('hlde,gk->lgdhke', w, eye)
    return full.reshape(half * KV_WIDTH, 2 * NSA_KV_HEADS * CMP_HIDDEN)


def _compress_pe(pe):
    half = CMP_BLOCK // 2
    p = pe.reshape(2, half, 1, HEAD_DIM)
    p = jnp.broadcast_to(p, (2, half, NSA_KV_HEADS, HEAD_DIM))
    return p.reshape(2, half * KV_WIDTH)


def _block_diag2(w2):
    z = jnp.zeros_like(w2)
    return jnp.concatenate([jnp.concatenate([w2, z], axis=1),
                            jnp.concatenate([z, w2], axis=1)], axis=0)


def kernel(x, norm_in_g, w_in, w_cmp_k1, w_cmp_k2, pe_cmp_k, w_cmp_v1, w_cmp_v2, pe_cmp_v,
           gmlp_ln_g, gmlp_ln_b, w_spatial, b_spatial, w_out, norm_f_g):
    B, T, D = x.shape
    assert w_in.shape[0] == 1, "single-layer block"
    tm = min(PROJ_TILE, T)
    (wq, wkc, wvc, wks, wvs, wkw, wvw, wgt, wzn, wu, wv, wzg) = _split_w_in(w_in[0])
    wn = jnp.concatenate([wq, wkc, wvc, wks, wkw, wzn], axis=1).astype(BF16)
    wt = jnp.concatenate([wvs, wvw, _gate_columns(wgt), wu, wv, wzg], axis=1).T.astype(BF16)
    lng = gmlp_ln_g[0].reshape(GMLP_GROUPS, GMLP_GROUP_DIM, 1)
    lnb = gmlp_ln_b[0].reshape(GMLP_GROUPS, GMLP_GROUP_DIM, 1)
    ws_t = jnp.swapaxes(w_spatial[0], 1, 2)
    bs = b_spatial[0].reshape(GMLP_GROUPS, 1, GMLP_CHUNK)
    q, kc, vc, ksw, zn, vt, gt, o_gmlp = _proj_call(
        x, norm_in_g[0].reshape(1, D), wn, wt, lng, lnb, ws_t, bs, tm)

    n_c = T // CMP_STRIDE
    kcmp, vcmpt = _compress_call(
        kc.reshape(B, n_c, CMP_STRIDE * KV_WIDTH), vc.reshape(B, n_c, CMP_STRIDE * KV_WIDTH),
        _compress_pe(pe_cmp_k[0]), _compress_pe(pe_cmp_v[0]),
        _compress_w1(w_cmp_k1[0]).astype(BF16), _compress_w1(w_cmp_v1[0]).astype(BF16),
        _block_diag2(w_cmp_k2[0]).astype(BF16), _block_diag2(w_cmp_v2[0]).T.astype(BF16))

    o_nsa = _attn_call(q, kcmp, vcmpt, ksw, vt, gt, zn)
    return _out_call(x, o_nsa, o_gmlp, w_out[0].astype(BF16), norm_f_g.reshape(1, D), tm)
```

```python
import functools
import math

import jax
import jax.numpy as jnp
from jax import lax
from jax.experimental import pallas as pl
from jax.experimental.pallas import tpu as pltpu

F32 = jnp.float32
BF16 = jnp.bfloat16

HEAD_DIM = 64
NSA_HEADS = 8
NSA_KV_HEADS = 2
HEADS_PER_GROUP = NSA_HEADS // NSA_KV_HEADS
NSA_WIDTH = NSA_HEADS * HEAD_DIM
KV_WIDTH = NSA_KV_HEADS * HEAD_DIM
N_GATES = 3 * NSA_HEADS
GMLP_GROUPS = 8
GMLP_GROUP_DIM = 64
GMLP_WIDTH = GMLP_GROUPS * GMLP_GROUP_DIM
GMLP_CHUNK = 128
CMP_BLOCK = 32
CMP_STRIDE = 16
CMP_HIDDEN = 128
SEL_BLOCK = 64
SEL_TOP_N = 16
WINDOW = 512
NORM_EPS = 1e-6
NEG_INF = -1e30
FORCE_BONUS = 1e4

Q_TILE = 128
KEY_CHUNK = 128
LANES = HEADS_PER_GROUP * Q_TILE
SEL_KEYS = 512
WIN_KEYS = WINDOW + Q_TILE
ONES_ROWS = 16
LOG2_E = math.log2(math.e)
GATE_ROWS = 16
PROJ_TILE = 512
MXU_DEPTH = 256
VMEM_LIMIT = 48 * 1024 * 1024
ATTN_VMEM_LIMIT = 56 * 1024 * 1024


def _sigmoid(x):
    return 1.0 / (1.0 + jnp.exp(-x))


def _gelu_tanh(x):
    c = math.sqrt(2.0 / math.pi)
    return x * (0.5 * (1.0 + jnp.tanh(c * (x + 0.044715 * (x * x * x)))))


def _proj_kernel(x_ref, g_ref, wn_ref, wt_ref, lng_ref, lnb_ref, ws_ref, bs_ref,
                 q_ref, kc_ref, vc_ref, ksw_ref, zn_ref, vt_ref, gt_ref, og_ref):
    tm = x_ref.shape[1]
    n_ck = tm // GMLP_CHUNK
    x = x_ref[0]
    ms = jnp.mean(x * x, axis=-1, keepdims=True)
    hn = ((x * lax.rsqrt(ms + NORM_EPS)) * g_ref[...]).astype(BF16)
    pn = jnp.dot(hn, wn_ref[...], preferred_element_type=F32)
    q_ref[0] = (pn[:, 0:512] * (HEAD_DIM ** -0.5 * LOG2_E)).astype(BF16)
    kc_ref[0] = pn[:, 512:640].astype(BF16)
    vc_ref[0] = pn[:, 640:768].astype(BF16)
    ksw_ref[0] = pn[:, 768:1024].astype(BF16)
    z = pn[:, 1024:1536]
    zn_ref[0] = (z * _sigmoid(z)).astype(BF16)
    pt = lax.dot_general(wt_ref[...], hn, (((1,), (1,)), ((), ())),
                         preferred_element_type=F32)
    vt = pt[0:256].astype(BF16)
    gt = _sigmoid(pt[256:288])
    for ck in range(n_ck):
        sl = slice(ck * GMLP_CHUNK, (ck + 1) * GMLP_CHUNK)
        vt_ref[0, ck] = vt[:, sl]
        gt_ref[0, ck] = gt[:, sl]
    gu = _gelu_tanh(pt[288:800])
    gv = _gelu_tanh(pt[800:1312]).reshape(GMLP_GROUPS, GMLP_GROUP_DIM, tm)
    zg = pt[1312:1824]
    mu = jnp.mean(gv, axis=1, keepdims=True)
    dv = gv - mu
    var = jnp.mean(dv * dv, axis=1, keepdims=True)
    vn = (dv * lax.rsqrt(var + NORM_EPS)) * lng_ref[...] + lnb_ref[...]
    s_i = lax.broadcasted_iota(jnp.int32, (GMLP_CHUNK, GMLP_CHUNK), 0)
    t_i = lax.broadcasted_iota(jnp.int32, (GMLP_CHUNK, GMLP_CHUNK), 1)
    causal = s_i <= t_i
    outs = []
    for g in range(GMLP_GROUPS):
        a = vn[g].astype(BF16)
        a_st = jnp.concatenate(
            [a[:, ck * GMLP_CHUNK:(ck + 1) * GMLP_CHUNK] for ck in range(n_ck)], axis=0)
        w = jnp.where(causal, ws_ref[g], 0.0).astype(BF16)
        r = jnp.dot(a_st, w, preferred_element_type=F32) + bs_ref[g]
        outs.append(jnp.concatenate(
            [r[ck * GMLP_GROUP_DIM:(ck + 1) * GMLP_GROUP_DIM] for ck in range(n_ck)], axis=1))
    mixed = jnp.concatenate(outs, axis=0)
    og_t = (gu * mixed) * (zg * _sigmoid(zg))
    og_ref[0] = og_t.T.astype(BF16)


def _proj_call(x, g_in, wn, wt, lng, lnb, ws_t, bs, tm):
    B, T, D = x.shape
    n_ck = tm // GMLP_CHUNK
    grid = (B, T // tm)
    const = lambda *shape: pl.BlockSpec(shape, lambda b, i: (0,) * len(shape))
    row = lambda w: pl.BlockSpec((1, tm, w), lambda b, i: (b, i, 0))
    out_shape = (
        jax.ShapeDtypeStruct((B, T, NSA_WIDTH), BF16),
        jax.ShapeDtypeStruct((B, T, KV_WIDTH), BF16),
        jax.ShapeDtypeStruct((B, T, KV_WIDTH), BF16),
        jax.ShapeDtypeStruct((B, T, 2 * KV_WIDTH), BF16),
        jax.ShapeDtypeStruct((B, T, NSA_WIDTH), BF16),
        jax.ShapeDtypeStruct((B, T // GMLP_CHUNK, 256, GMLP_CHUNK), BF16),
        jax.ShapeDtypeStruct((B, T // GMLP_CHUNK, 2 * GATE_ROWS, GMLP_CHUNK), F32),
        jax.ShapeDtypeStruct((B, T, GMLP_WIDTH), BF16),
    )
    out_specs = (
        row(NSA_WIDTH), row(KV_WIDTH), row(KV_WIDTH), row(2 * KV_WIDTH), row(NSA_WIDTH),
        pl.BlockSpec((1, n_ck, 256, GMLP_CHUNK), lambda b, i: (b, i, 0, 0)),
        pl.BlockSpec((1, n_ck, 2 * GATE_ROWS, GMLP_CHUNK), lambda b, i: (b, i, 0, 0)),
        row(GMLP_WIDTH),
    )
    in_specs = [
        pl.BlockSpec((1, tm, D), lambda b, i: (b, i, 0)),
        const(1, D), const(*wn.shape), const(*wt.shape),
        const(*lng.shape), const(*lnb.shape), const(*ws_t.shape), const(*bs.shape),
    ]
    return pl.pallas_call(
        _proj_kernel, out_shape=out_shape, grid=grid, in_specs=in_specs, out_specs=out_specs,
        name="proj",
        compiler_params=pltpu.CompilerParams(
            dimension_semantics=("arbitrary", "arbitrary"), vmem_limit_bytes=VMEM_LIMIT),
    )(x, g_in, wn, wt, lng, lnb, ws_t, bs)


def _compress_kernel(kc_ref, vc_ref, pek_ref, pev_ref, wk1_ref, wv1_ref, wk2_ref, wv2t_ref,
                     kcmp_ref, vcmpt_ref):
    n_c = kc_ref.shape[1]
    half = 2 * CMP_HIDDEN

    def hidden(src_ref, pe_ref, w1_ref):
        xs = src_ref[0].astype(F32)
        xa = (xs + pe_ref[0:1, :]).astype(BF16)
        xb = (xs + pe_ref[1:2, :]).astype(BF16)
        ha = jnp.dot(xa, w1_ref[:, 0:half], preferred_element_type=F32)
        hb = jnp.dot(xb, w1_ref[:, half:2 * half], preferred_element_type=F32)
        hb_next = pltpu.roll(hb, shift=n_c - 1, axis=0)
        return _gelu_tanh(ha + hb_next).astype(BF16)

    hk = hidden(kc_ref, pek_ref, wk1_ref)
    kcmp_ref[0] = jnp.dot(hk, wk2_ref[...], preferred_element_type=F32).astype(BF16)
    hv = hidden(vc_ref, pev_ref, wv1_ref)
    vcmpt_ref[0] = lax.dot_general(wv2t_ref[...], hv, (((1,), (1,)), ((), ())),
                                   preferred_element_type=F32).astype(BF16)


def _compress_call(kc2, vc2, pek, pev, wk1, wv1, wk2, wv2t):
    B, n_c, W = kc2.shape
    const = lambda a: pl.BlockSpec(a.shape, lambda b: (0,) * a.ndim)
    return pl.pallas_call(
        _compress_kernel,
        out_shape=(jax.ShapeDtypeStruct((B, n_c, KV_WIDTH), BF16),
                   jax.ShapeDtypeStruct((B, KV_WIDTH, n_c), BF16)),
        grid=(B,),
        in_specs=[pl.BlockSpec((1, n_c, W), lambda b: (b, 0, 0)),
                  pl.BlockSpec((1, n_c, W), lambda b: (b, 0, 0)),
                  const(pek), const(pev), const(wk1), const(wv1), const(wk2), const(wv2t)],
        out_specs=(pl.BlockSpec((1, n_c, KV_WIDTH), lambda b: (b, 0, 0)),
                   pl.BlockSpec((1, KV_WIDTH, n_c), lambda b: (b, 0, 0))),
        name="compress",
        compiler_params=pltpu.CompilerParams(
            dimension_semantics=("arbitrary",), vmem_limit_bytes=VMEM_LIMIT),
    )(kc2, vc2, pek, pev, wk1, wv1, wk2, wv2t)


def _attn_kernel(q_ref, kcmp_ref, vcmpt_ref, ks_ref, kw_ref, vst_ref, vwt_ref, gt_ref, zn_ref,
                 o_ref, tri_scr, onehot_scr, qb_scr, m_scr, acc_scr, o_scr, s_scr, p_scr):
    g = pl.program_id(1)
    T = q_ref.shape[1]
    n_c = kcmp_ref.shape[1]
    n_blk = T // SEL_BLOCK
    n_tiles = T // Q_TILE
    assert KV_WIDTH + n_blk <= MXU_DEPTH

    def with_ones(vt):
        return jnp.concatenate([vt, jnp.ones((ONES_ROWS, vt.shape[1]), vt.dtype)], axis=0)

    def normalized(acc):
        return acc[0:HEAD_DIM] / acc[HEAD_DIM:HEAD_DIM + 1]

    def gate_row(c, br):
        gt = gt_ref[0, c]
        return jnp.concatenate(
            [gt[4 * br + h:4 * br + h + 1] for h in range(HEADS_PER_GROUP)], axis=1)

    key_i = lax.broadcasted_iota(jnp.int32, (T, MXU_DEPTH - KV_WIDTH), 0)
    blk_i = lax.broadcasted_iota(jnp.int32, (T, MXU_DEPTH - KV_WIDTH), 1)
    onehot_scr[...] = jnp.where((key_i >> (SEL_BLOCK.bit_length() - 1)) == blk_i, 1.0, 0.0).astype(BF16)
    if KV_WIDTH + n_blk < MXU_DEPTH:
        qb_scr[:, KV_WIDTH + n_blk:, :] = jnp.zeros(
            (n_tiles, MXU_DEPTH - KV_WIDTH - n_blk, LANES), BF16)

    kcmp = kcmp_ref[0]
    vct = vcmpt_ref[0]
    jb_c = lax.broadcasted_iota(jnp.int32, (n_blk, n_c), 0)
    ic_c = lax.broadcasted_iota(jnp.int32, (n_blk, n_c), 1)
    overlap_t = jnp.where(ic_c * CMP_STRIDE < (jb_c + 1) * SEL_BLOCK,
                          jnp.where(ic_c * CMP_STRIDE + (CMP_BLOCK - 1) >= jb_c * SEL_BLOCK, 1.0, 0.0),
                          0.0).astype(BF16)
    row_group = lax.broadcasted_iota(jnp.int32, (KV_WIDTH, LANES), 0) >> (HEAD_DIM.bit_length() - 1)
    lane_q = lax.broadcasted_iota(jnp.int32, (1, LANES), 1) & (Q_TILE - 1)
    win_row = lax.broadcasted_iota(jnp.int32, (WIN_KEYS, LANES), 0)
    key_row = lax.broadcasted_iota(jnp.int32, (KEY_CHUNK, LANES), 0)
    tri_scr[0] = jnp.where(key_row <= lane_q, 0.0, NEG_INF)
    tri_scr[1] = jnp.where(key_row > lane_q, 0.0, NEG_INF)
    cmp_end = lax.broadcasted_iota(jnp.int32, (n_c, LANES), 0) * CMP_STRIDE + (CMP_BLOCK - 1)
    blk_row = lax.broadcasted_iota(jnp.int32, (n_blk, Q_TILE), 0)

    def tile(c, full_window):
        t0 = pl.multiple_of(c * Q_TILE, Q_TILE)
        tq = t0 + lane_q
        qt = q_ref[0, pl.ds(t0, Q_TILE), :].astype(F32).T
        b64 = jnp.concatenate(
            [qt[h * HEAD_DIM:(h + 1) * HEAD_DIM] for h in range(HEADS_PER_GROUP)], axis=1)
        qmat = jnp.where(row_group == g, jnp.concatenate([b64, b64], axis=0), 0.0).astype(BF16)

        s = jnp.dot(kcmp, qmat, preferred_element_type=F32)
        valid_c = cmp_end <= tq
        s = jnp.where(valid_c, s, NEG_INF)
        m_c = jnp.max(s, axis=0, keepdims=True)
        e16 = jnp.where(valid_c, jnp.exp2(s - m_c), 0.0).astype(BF16)
        acc_c = jnp.dot(with_ones(vct), e16, preferred_element_type=F32)
        l_c = acc_c[HEAD_DIM:HEAD_DIM + 1]
        linv_c = jnp.where(l_c > 0.0, 1.0 / l_c, 0.0)
        o_c = acc_c[0:HEAD_DIM] * linv_c
        imp4 = jnp.dot(overlap_t, e16, preferred_element_type=F32) * linv_c
        imp = imp4[:, 0:Q_TILE]
        for h in range(1, HEADS_PER_GROUP):
            imp = imp + imp4[:, h * Q_TILE:(h + 1) * Q_TILE]

        cur = (t0 + lax.broadcasted_iota(jnp.int32, (1, Q_TILE), 1)) >> (SEL_BLOCK.bit_length() - 1)
        valid_b = blk_row <= cur
        bonus = jnp.where(blk_row == 0, FORCE_BONUS,
                          jnp.where(blk_row == cur, FORCE_BONUS,
                                    jnp.where(blk_row == cur - 1, FORCE_BONUS, 0.0)))
        score = jnp.where(valid_b, imp + bonus, -1.0)
        n_rb = n_blk // 8
        sblk = [score[8 * r:8 * r + 8] for r in range(n_rb)]
        cnt = [jnp.zeros((8, Q_TILE), F32) for _ in range(n_rb)]
        sub_row = lax.broadcasted_iota(jnp.int32, (8, Q_TILE), 0)
        for j in range(n_blk):
            rj = jnp.broadcast_to(score[j:j + 1], (8, Q_TILE))
            for r in range(n_rb):
                if 8 * r + 7 <= j:
                    cnt[r] = cnt[r] + jnp.where(rj > sblk[r], 1.0, 0.0)
                elif 8 * r > j:
                    cnt[r] = cnt[r] + jnp.where(rj >= sblk[r], 1.0, 0.0)
                else:
                    cnt[r] = cnt[r] + jnp.where(sub_row + 8 * r > j,
                                                jnp.where(rj >= sblk[r], 1.0, 0.0),
                                                jnp.where(rj > sblk[r], 1.0, 0.0))
        keep = jnp.concatenate(
            [jnp.where(blk_row[8 * r:8 * r + 8] < c * (Q_TILE // SEL_BLOCK),
                       jnp.where(cnt[r] < float(SEL_TOP_N), 0.0, NEG_INF), NEG_INF)
             for r in range(n_rb)], axis=0)
        qb_scr[c, 0:KV_WIDTH, :] = qmat
        qb_scr[c, KV_WIDTH:KV_WIDTH + n_blk, :] = jnp.concatenate(
            [keep] * HEADS_PER_GROUP, axis=1).astype(BF16)

        w0 = jnp.maximum(c - WINDOW // KEY_CHUNK, 0)
        wk0 = pl.multiple_of(w0 * KEY_CHUNK, KEY_CHUNK)
        sw = jnp.dot(kw_ref[0, pl.ds(wk0, WIN_KEYS), :], qmat, preferred_element_type=F32)
        if full_window:
            sw = jnp.concatenate([sw[0:KEY_CHUNK] + tri_scr[1], sw[KEY_CHUNK:WINDOW],
                                  sw[WINDOW:WIN_KEYS] + tri_scr[0]], axis=0)
        else:
            pos = wk0 + win_row
            sw = jnp.where(pos <= tq, jnp.where(pos > tq - WINDOW, sw, NEG_INF), NEG_INF)
        m_w = jnp.max(sw, axis=0, keepdims=True)
        pw = jnp.exp2(sw - m_w).astype(BF16)
        vw = jnp.concatenate([vwt_ref[0, w0 + u] for u in range(WIN_KEYS // KEY_CHUNK)], axis=1)
        acc_w = jnp.dot(with_ones(vw), pw, preferred_element_type=F32)

        sd = jnp.dot(ks_ref[0, pl.ds(t0, KEY_CHUNK), :], qmat, preferred_element_type=F32) + tri_scr[0]
        m_d = jnp.max(sd, axis=0, keepdims=True)
        pd = jnp.exp2(sd - m_d).astype(BF16)
        acc_d = jnp.dot(with_ones(vst_ref[0, c]), pd, preferred_element_type=F32)
        m_scr[c] = m_d
        acc_scr[c] = acc_d
        o_scr[c] = gate_row(c, 0) * o_c + gate_row(c, 2) * normalized(acc_w)

    first_full = min(WINDOW // Q_TILE, n_tiles)
    pl.loop(0, first_full)(functools.partial(tile, full_window=False))
    pl.loop(first_full, n_tiles)(functools.partial(tile, full_window=True))

    chunks_per_step = SEL_KEYS // KEY_CHUNK
    tiles_per_step = SEL_KEYS // Q_TILE
    n_sel_steps = T // SEL_KEYS
    assert n_tiles - (tiles_per_step * (n_sel_steps - 1) + 1) >= 2

    def sel_sweep(j):
        k0 = pl.multiple_of(j * SEL_KEYS, SEL_KEYS)
        first = tiles_per_step * j + 1
        k_ext = jnp.concatenate([ks_ref[0, pl.ds(k0, SEL_KEYS), :], onehot_scr[pl.ds(k0, SEL_KEYS), :]],
                                axis=1)
        vs = with_ones(jnp.concatenate(
            [vst_ref[0, chunks_per_step * j + u] for u in range(chunks_per_step)], axis=1))

        def scores(c):
            sc = jnp.dot(k_ext, qb_scr[c], preferred_element_type=F32)
            s_scr[...] = sc
            return jnp.max(sc, axis=0, keepdims=True)

        def exponentials(mx):
            p_scr[...] = jnp.exp2(s_scr[...] - mx).astype(BF16)

        def accumulate(c, mx):
            pv = jnp.dot(vs, p_scr[...], preferred_element_type=F32)
            m = m_scr[c]
            m_new = jnp.maximum(m, mx)
            acc_scr[c] = jnp.exp2(m - m_new) * acc_scr[c] + jnp.exp2(mx - m_new) * pv
            m_scr[c] = m_new

        mx_a = scores(first)
        exponentials(mx_a)
        mx_b = scores(first + 1)

        def body(c, carry):
            mx_c, mx_n = carry
            accumulate(c, mx_c)
            exponentials(mx_n)
            return mx_n, scores(c + 2)

        mx_a, mx_b = lax.fori_loop(first, n_tiles - 2, body, (mx_a, mx_b))
        accumulate(n_tiles - 2, mx_a)
        exponentials(mx_b)
        accumulate(n_tiles - 1, mx_b)

    pl.loop(0, n_sel_steps)(sel_sweep)

    @pl.loop(0, n_tiles)
    def _(c):
        t0 = pl.multiple_of(c * Q_TILE, Q_TILE)
        o_t = o_scr[c] + gate_row(c, 1) * normalized(acc_scr[c])
        stacked = jnp.concatenate(
            [o_t[:, h * Q_TILE:(h + 1) * Q_TILE] for h in range(HEADS_PER_GROUP)], axis=0)
        o = stacked.T * zn_ref[0, pl.ds(t0, Q_TILE), :].astype(F32)
        o_ref[0, pl.ds(t0, Q_TILE), :] = o.astype(BF16)


def _attn_call(q, kcmp, vcmpt, ksw, vt, gt, zn):
    B, T, _ = q.shape
    n_c = kcmp.shape[1]
    n_ck = T // KEY_CHUNK
    n_tiles = T // Q_TILE
    G = NSA_KV_HEADS
    gw = HEADS_PER_GROUP * HEAD_DIM
    in_specs = [
        pl.BlockSpec((1, T, gw), lambda b, g: (b, 0, g)),
        pl.BlockSpec((1, n_c, KV_WIDTH), lambda b, g: (b, 0, 0)),
        pl.BlockSpec((1, HEAD_DIM, n_c), lambda b, g: (b, g, 0)),
        pl.BlockSpec((1, T, KV_WIDTH), lambda b, g: (b, 0, 0)),
        pl.BlockSpec((1, T, KV_WIDTH), lambda b, g: (b, 0, 1)),
        pl.BlockSpec((1, n_ck, HEAD_DIM, KEY_CHUNK), lambda b, g: (b, 0, g, 0)),
        pl.BlockSpec((1, n_ck, HEAD_DIM, KEY_CHUNK), lambda b, g: (b, 0, G + g, 0)),
        pl.BlockSpec((1, n_ck, GATE_ROWS, KEY_CHUNK), lambda b, g: (b, 0, g, 0)),
        pl.BlockSpec((1, T, gw), lambda b, g: (b, 0, g)),
    ]
    return pl.pallas_call(
        _attn_kernel,
        out_shape=jax.ShapeDtypeStruct((B, T, NSA_WIDTH), BF16),
        grid=(B, G),
        in_specs=in_specs,
        out_specs=pl.BlockSpec((1, T, gw), lambda b, g: (b, 0, g)),
        scratch_shapes=[pltpu.VMEM((2, KEY_CHUNK, LANES), F32),
                        pltpu.VMEM((T, MXU_DEPTH - KV_WIDTH), BF16),
                        pltpu.VMEM((n_tiles, MXU_DEPTH, LANES), BF16),
                        pltpu.VMEM((n_tiles, 1, LANES), F32),
                        pltpu.VMEM((n_tiles, HEAD_DIM + ONES_ROWS, LANES), F32),
                        pltpu.VMEM((n_tiles, HEAD_DIM, LANES), F32),
                        pltpu.VMEM((SEL_KEYS, LANES), F32),
                        pltpu.VMEM((SEL_KEYS, LANES), BF16)],
        name="attn",
        compiler_params=pltpu.CompilerParams(
            dimension_semantics=("arbitrary", "arbitrary"), vmem_limit_bytes=ATTN_VMEM_LIMIT),
    )(q, kcmp, vcmpt, ksw, ksw, vt, vt, gt, zn)


def _out_kernel(x_ref, mn_ref, mg_ref, wo_ref, g_ref, o_ref):
    h = x_ref[0]
    h = h + jnp.dot(mn_ref[0], wo_ref[0:NSA_WIDTH, :], preferred_element_type=F32)
    h = h + jnp.dot(mg_ref[0], wo_ref[NSA_WIDTH:, :], preferred_element_type=F32)
    ms = jnp.mean(h * h, axis=-1, keepdims=True)
    o_ref[0] = (h * lax.rsqrt(ms + NORM_EPS)) * g_ref[...]


def _out_call(x, mix_nsa, mix_gmlp, wo, g_f, tm):
    B, T, D = x.shape
    row = lambda w: pl.BlockSpec((1, tm, w), lambda b, i: (b, i, 0))
    return pl.pallas_call(
        _out_kernel,
        out_shape=jax.ShapeDtypeStruct((B, T, D), x.dtype),
        grid=(B, T // tm),
        in_specs=[row(D), row(NSA_WIDTH), row(GMLP_WIDTH),
                  pl.BlockSpec(wo.shape, lambda b, i: (0, 0)),
                  pl.BlockSpec((1, D), lambda b, i: (0, 0))],
        out_specs=row(D),
        name="out",
        compiler_params=pltpu.CompilerParams(
            dimension_semantics=("arbitrary", "arbitrary"), vmem_limit_bytes=VMEM_LIMIT),
    )(x, mix_nsa, mix_gmlp, wo, g_f)


def _split_w_in(w):
    sizes = (NSA_WIDTH, KV_WIDTH, KV_WIDTH, KV_WIDTH, KV_WIDTH, KV_WIDTH, KV_WIDTH, N_GATES,
             NSA_WIDTH, GMLP_WIDTH, GMLP_WIDTH, GMLP_WIDTH)
    parts, off = [], 0
    for s in sizes:
        parts.append(w[:, off:off + s])
        off += s
    return parts


def _gate_columns(w_gate):
    d = w_gate.shape[0]
    wg = w_gate.reshape(d, NSA_KV_HEADS, HEADS_PER_GROUP, 3).transpose(0, 1, 3, 2)
    wg = wg.reshape(d, NSA_KV_HEADS, 3 * HEADS_PER_GROUP)
    wg = jnp.pad(wg, ((0, 0), (0, 0), (0, GATE_ROWS - 3 * HEADS_PER_GROUP)))
    return wg.reshape(d, NSA_KV_HEADS * GATE_ROWS)


def _compress_w1(w1):
    half = CMP_BLOCK // 2
    w = w1.reshape(2, half, HEAD_DIM, CMP_HIDDEN)
    eye = jnp.eye(NSA_KV_HEADS, dtype=w1.dtype)
    full = jnp.einsum('hlde,gk->lgdhke', w, eye)
    return full.reshape(half * KV_WIDTH, 2 * NSA_KV_HEADS * CMP_HIDDEN)


def _compress_pe(pe):
    half = CMP_BLOCK // 2
    p = pe.reshape(2, half, 1, HEAD_DIM)
    p = jnp.broadcast_to(p, (2, half, NSA_KV_HEADS, HEAD_DIM))
    return p.reshape(2, half * KV_WIDTH)


def _block_diag2(w2):
    z = jnp.zeros_like(w2)
    return jnp.concatenate([jnp.concatenate([w2, z], axis=1),
                            jnp.concatenate([z, w2], axis=1)], axis=0)


def kernel(x, norm_in_g, w_in, w_cmp_k1, w_cmp_k2, pe_cmp_k, w_cmp_v1, w_cmp_v2, pe_cmp_v,
           gmlp_ln_g, gmlp_ln_b, w_spatial, b_spatial, w_out, norm_f_g):
    B, T, D = x.shape
    assert w_in.shape[0] == 1, "single-layer block"
    tm = min(PROJ_TILE, T)
    (wq, wkc, wvc, wks, wvs, wkw, wvw, wgt, wzn, wu, wv, wzg) = _split_w_in(w_in[0])
    wn = jnp.concatenate([wq, wkc, wvc, wks, wkw, wzn], axis=1).astype(BF16)
    wt = jnp.concatenate([wvs, wvw, _gate_columns(wgt), wu, wv, wzg], axis=1).T.astype(BF16)
    lng = gmlp_ln_g[0].reshape(GMLP_GROUPS, GMLP_GROUP_DIM, 1)
    lnb = gmlp_ln_b[0].reshape(GMLP_GROUPS, GMLP_GROUP_DIM, 1)
    ws_t = jnp.swapaxes(w_spatial[0], 1, 2)
    bs = b_spatial[0].reshape(GMLP_GROUPS, 1, GMLP_CHUNK)
    q, kc, vc, ksw, zn, vt, gt, o_gmlp = _proj_call(
        x, norm_in_g[0].reshape(1, D), wn, wt, lng, lnb, ws_t, bs, tm)

    n_c = T // CMP_STRIDE
    kcmp, vcmpt = _compress_call(
        kc.reshape(B, n_c, CMP_STRIDE * KV_WIDTH), vc.reshape(B, n_c, CMP_STRIDE * KV_WIDTH),
        _compress_pe(pe_cmp_k[0]), _compress_pe(pe_cmp_v[0]),
        _compress_w1(w_cmp_k1[0]).astype(BF16), _compress_w1(w_cmp_v1[0]).astype(BF16),
        _block_diag2(w_cmp_k2[0]).astype(BF16), _block_diag2(w_cmp_v2[0]).T.astype(BF16))

    o_nsa = _attn_call(q, kcmp, vcmpt, ksw, vt, gt, zn)
    return _out_call(x, o_nsa, o_gmlp, w_out[0].astype(BF16), norm_f_g.reshape(1, D), tm)
```

```python
import functools
import math

import jax
import jax.numpy as jnp
from jax import lax
from jax.experimental import pallas as pl
from jax.experimental.pallas import tpu as pltpu

F32 = jnp.float32
BF16 = jnp.bfloat16

HEAD_DIM = 64
NSA_HEADS = 8
NSA_KV_HEADS = 2
HEADS_PER_GROUP = NSA_HEADS // NSA_KV_HEADS
NSA_WIDTH = NSA_HEADS * HEAD_DIM
KV_WIDTH = NSA_KV_HEADS * HEAD_DIM
N_GATES = 3 * NSA_HEADS
GMLP_GROUPS = 8
GMLP_GROUP_DIM = 64
GMLP_WIDTH = GMLP_GROUPS * GMLP_GROUP_DIM
GMLP_CHUNK = 128
CMP_BLOCK = 32
CMP_STRIDE = 16
CMP_HIDDEN = 128
SEL_BLOCK = 64
SEL_TOP_N = 16
WINDOW = 512
NORM_EPS = 1e-6
NEG_INF = -1e30
FORCE_BONUS = 1e4

Q_TILE = 128
KEY_CHUNK = 128
LANES = HEADS_PER_GROUP * Q_TILE
SEL_KEYS = 512
WIN_KEYS = WINDOW + Q_TILE
ONES_ROWS = 16
LOG2_E = math.log2(math.e)
GATE_ROWS = 16
PROJ_TILE = 512
MXU_DEPTH = 256
VMEM_LIMIT = 48 * 1024 * 1024
ATTN_VMEM_LIMIT = 56 * 1024 * 1024


def _sigmoid(x):
    return 1.0 / (1.0 + jnp.exp(-x))


def _gelu_tanh(x):
    c = math.sqrt(2.0 / math.pi)
    return x * (0.5 * (1.0 + jnp.tanh(c * (x + 0.044715 * (x * x * x)))))


def _proj_kernel(x_ref, g_ref, wn_ref, wt_ref, lng_ref, lnb_ref, ws_ref, bs_ref,
                 q_ref, kc_ref, vc_ref, ksw_ref, zn_ref, vt_ref, gt_ref, og_ref):
    tm = x_ref.shape[1]
    n_ck = tm // GMLP_CHUNK
    x = x_ref[0]
    ms = jnp.mean(x * x, axis=-1, keepdims=True)
    hn = ((x * lax.rsqrt(ms + NORM_EPS)) * g_ref[...]).astype(BF16)
    pn = jnp.dot(hn, wn_ref[...], preferred_element_type=F32)
    q_ref[0] = (pn[:, 0:512] * (HEAD_DIM ** -0.5 * LOG2_E)).astype(BF16)
    kc_ref[0] = pn[:, 512:640].astype(BF16)
    vc_ref[0] = pn[:, 640:768].astype(BF16)
    ksw_ref[0] = pn[:, 768:1024].astype(BF16)
    z = pn[:, 1024:1536]
    zn_ref[0] = (z * _sigmoid(z)).astype(BF16)
    pt = lax.dot_general(wt_ref[...], hn, (((1,), (1,)), ((), ())),
                         preferred_element_type=F32)
    vt = pt[0:256].astype(BF16)
    gt = _sigmoid(pt[256:288])
    for ck in range(n_ck):
        sl = slice(ck * GMLP_CHUNK, (ck + 1) * GMLP_CHUNK)
        vt_ref[0, ck] = vt[:, sl]
        gt_ref[0, ck] = gt[:, sl]
    gu = _gelu_tanh(pt[288:800])
    gv = _gelu_tanh(pt[800:1312]).reshape(GMLP_GROUPS, GMLP_GROUP_DIM, tm)
    zg = pt[1312:1824]
    mu = jnp.mean(gv, axis=1, keepdims=True)
    dv = gv - mu
    var = jnp.mean(dv * dv, axis=1, keepdims=True)
    vn = (dv * lax.rsqrt(var + NORM_EPS)) * lng_ref[...] + lnb_ref[...]
    s_i = lax.broadcasted_iota(jnp.int32, (GMLP_CHUNK, GMLP_CHUNK), 0)
    t_i = lax.broadcasted_iota(jnp.int32, (GMLP_CHUNK, GMLP_CHUNK), 1)
    causal = s_i <= t_i
    outs = []
    for g in range(GMLP_GROUPS):
        a = vn[g].astype(BF16)
        a_st = jnp.concatenate(
            [a[:, ck * GMLP_CHUNK:(ck + 1) * GMLP_CHUNK] for ck in range(n_ck)], axis=0)
        w = jnp.where(causal, ws_ref[g], 0.0).astype(BF16)
        r = jnp.dot(a_st, w, preferred_element_type=F32) + bs_ref[g]
        outs.append(jnp.concatenate(
            [r[ck * GMLP_GROUP_DIM:(ck + 1) * GMLP_GROUP_DIM] for ck in range(n_ck)], axis=1))
    mixed = jnp.concatenate(outs, axis=0)
    og_t = (gu * mixed) * (zg * _sigmoid(zg))
    og_ref[0] = og_t.T.astype(BF16)


def _proj_call(x, g_in, wn, wt, lng, lnb, ws_t, bs, tm):
    B, T, D = x.shape
    n_ck = tm // GMLP_CHUNK
    grid = (B, T // tm)
    const = lambda *shape: pl.BlockSpec(shape, lambda b, i: (0,) * len(shape))
    row = lambda w: pl.BlockSpec((1, tm, w), lambda b, i: (b, i, 0))
    out_shape = (
        jax.ShapeDtypeStruct((B, T, NSA_WIDTH), BF16),
        jax.ShapeDtypeStruct((B, T, KV_WIDTH), BF16),
        jax.ShapeDtypeStruct((B, T, KV_WIDTH), BF16),
        jax.ShapeDtypeStruct((B, T, 2 * KV_WIDTH), BF16),
        jax.ShapeDtypeStruct((B, T, NSA_WIDTH), BF16),
        jax.ShapeDtypeStruct((B, T // GMLP_CHUNK, 256, GMLP_CHUNK), BF16),
        jax.ShapeDtypeStruct((B, T // GMLP_CHUNK, 2 * GATE_ROWS, GMLP_CHUNK), F32),
        jax.ShapeDtypeStruct((B, T, GMLP_WIDTH), BF16),
    )
    out_specs = (
        row(NSA_WIDTH), row(KV_WIDTH), row(KV_WIDTH), row(2 * KV_WIDTH), row(NSA_WIDTH),
        pl.BlockSpec((1, n_ck, 256, GMLP_CHUNK), lambda b, i: (b, i, 0, 0)),
        pl.BlockSpec((1, n_ck, 2 * GATE_ROWS, GMLP_CHUNK), lambda b, i: (b, i, 0, 0)),
        row(GMLP_WIDTH),
    )
    in_specs = [
        pl.BlockSpec((1, tm, D), lambda b, i: (b, i, 0)),
        const(1, D), const(*wn.shape), const(*wt.shape),
        const(*lng.shape), const(*lnb.shape), const(*ws_t.shape), const(*bs.shape),
    ]
    return pl.pallas_call(
        _proj_kernel, out_shape=out_shape, grid=grid, in_specs=in_specs, out_specs=out_specs,
        name="proj",
        compiler_params=pltpu.CompilerParams(
            dimension_semantics=("arbitrary", "arbitrary"), vmem_limit_bytes=VMEM_LIMIT),
    )(x, g_in, wn, wt, lng, lnb, ws_t, bs)


def _compress_kernel(kc_ref, vc_ref, pek_ref, pev_ref, wk1_ref, wv1_ref, wk2_ref, wv2t_ref,
                     kcmp_ref, vcmpt_ref):
    n_c = kc_ref.shape[1]
    half = 2 * CMP_HIDDEN

    def hidden(src_ref, pe_ref, w1_ref):
        xs = src_ref[0].astype(F32)
        xa = (xs + pe_ref[0:1, :]).astype(BF16)
        xb = (xs + pe_ref[1:2, :]).astype(BF16)
        ha = jnp.dot(xa, w1_ref[:, 0:half], preferred_element_type=F32)
        hb = jnp.dot(xb, w1_ref[:, half:2 * half], preferred_element_type=F32)
        hb_next = pltpu.roll(hb, shift=n_c - 1, axis=0)
        return _gelu_tanh(ha + hb_next).astype(BF16)

    hk = hidden(kc_ref, pek_ref, wk1_ref)
    kcmp_ref[0] = jnp.dot(hk, wk2_ref[...], preferred_element_type=F32).astype(BF16)
    hv = hidden(vc_ref, pev_ref, wv1_ref)
    vcmpt_ref[0] = lax.dot_general(wv2t_ref[...], hv, (((1,), (1,)), ((), ())),
                                   preferred_element_type=F32).astype(BF16)


def _compress_call(kc2, vc2, pek, pev, wk1, wv1, wk2, wv2t):
    B, n_c, W = kc2.shape
    const = lambda a: pl.BlockSpec(a.shape, lambda b: (0,) * a.ndim)
    return pl.pallas_call(
        _compress_kernel,
        out_shape=(jax.ShapeDtypeStruct((B, n_c, KV_WIDTH), BF16),
                   jax.ShapeDtypeStruct((B, KV_WIDTH, n_c), BF16)),
        grid=(B,),
        in_specs=[pl.BlockSpec((1, n_c, W), lambda b: (b, 0, 0)),
                  pl.BlockSpec((1, n_c, W), lambda b: (b, 0, 0)),
                  const(pek), const(pev), const(wk1), const(wv1), const(wk2), const(wv2t)],
        out_specs=(pl.BlockSpec((1, n_c, KV_WIDTH), lambda b: (b, 0, 0)),
                   pl.BlockSpec((1, KV_WIDTH, n_c), lambda b: (b, 0, 0))),
        name="compress",
        compiler_params=pltpu.CompilerParams(
            dimension_semantics=("arbitrary",), vmem_limit_bytes=VMEM_LIMIT),
    )(kc2, vc2, pek, pev, wk1, wv1, wk2, wv2t)


def _attn_kernel(q_ref, kcmp_ref, vcmpt_ref, ks_ref, kw_ref, vst_ref, vwt_ref, gt_ref, zn_ref,
                 o_ref, mask_scr, onehot_scr, qb_scr, m_scr, acc_scr, o_scr, s_scr, p_scr,
                 s1_scr, p1_scr):
    g = pl.program_id(1)
    T = q_ref.shape[1]
    n_c = kcmp_ref.shape[1]
    n_blk = T // SEL_BLOCK
    n_tiles = T // Q_TILE
    assert KV_WIDTH + n_blk <= MXU_DEPTH

    def with_ones(vt):
        return jnp.concatenate([vt, jnp.ones((ONES_ROWS, vt.shape[1]), vt.dtype)], axis=0)

    def normalized(acc):
        return acc[0:HEAD_DIM] / acc[HEAD_DIM:HEAD_DIM + 1]

    def gate_row(c, br):
        gt = gt_ref[0, c]
        return jnp.concatenate(
            [gt[4 * br + h:4 * br + h + 1] for h in range(HEADS_PER_GROUP)], axis=1)

    key_i = lax.broadcasted_iota(jnp.int32, (T, MXU_DEPTH - KV_WIDTH), 0)
    blk_i = lax.broadcasted_iota(jnp.int32, (T, MXU_DEPTH - KV_WIDTH), 1)
    onehot_scr[...] = jnp.where((key_i >> (SEL_BLOCK.bit_length() - 1)) == blk_i, 1.0, 0.0).astype(BF16)
    if KV_WIDTH + n_blk < MXU_DEPTH:
        qb_scr[:, KV_WIDTH + n_blk:, :] = jnp.zeros(
            (n_tiles, MXU_DEPTH - KV_WIDTH - n_blk, LANES), BF16)

    kcmp = kcmp_ref[0]
    vct1 = with_ones(vcmpt_ref[0])
    jb_c = lax.broadcasted_iota(jnp.int32, (n_blk, n_c), 0)
    ic_c = lax.broadcasted_iota(jnp.int32, (n_blk, n_c), 1)
    overlap_t = jnp.where(ic_c * CMP_STRIDE < (jb_c + 1) * SEL_BLOCK,
                          jnp.where(ic_c * CMP_STRIDE + (CMP_BLOCK - 1) >= jb_c * SEL_BLOCK, 1.0, 0.0),
                          0.0).astype(BF16)
    row_group = lax.broadcasted_iota(jnp.int32, (KV_WIDTH, LANES), 0) >> (HEAD_DIM.bit_length() - 1)
    lane_q = lax.broadcasted_iota(jnp.int32, (1, LANES), 1) & (Q_TILE - 1)
    key_row = lax.broadcasted_iota(jnp.int32, (KEY_CHUNK, LANES), 0)
    MASK_CAUSAL, MASK_WINDOW, MASK_NONE, MASK_ALL = range(4)
    mask_scr[MASK_CAUSAL] = jnp.where(key_row <= lane_q, 0.0, NEG_INF)
    mask_scr[MASK_WINDOW] = jnp.where(key_row > lane_q, 0.0, NEG_INF)
    mask_scr[MASK_NONE] = jnp.zeros((KEY_CHUNK, LANES), F32)
    mask_scr[MASK_ALL] = jnp.full((KEY_CHUNK, LANES), NEG_INF, F32)
    cmp_end = lax.broadcasted_iota(jnp.int32, (n_c, LANES), 0) * CMP_STRIDE + (CMP_BLOCK - 1)
    blk_row = lax.broadcasted_iota(jnp.int32, (n_blk, Q_TILE), 0)

    win_chunks = WIN_KEYS // KEY_CHUNK
    r_win, r_own, r_end = n_c, n_c + WIN_KEYS, n_c + WIN_KEYS + KEY_CHUNK

    def win_first_chunk(c):
        return jnp.maximum(c - WINDOW // KEY_CHUNK, 0)

    def tile_scores(c):
        t0 = c * Q_TILE if isinstance(c, int) else pl.multiple_of(c * Q_TILE, Q_TILE)
        qt = q_ref[0, pl.ds(t0, Q_TILE), :].astype(F32).T
        b64 = jnp.concatenate(
            [qt[h * HEAD_DIM:(h + 1) * HEAD_DIM] for h in range(HEADS_PER_GROUP)], axis=1)
        qmat = jnp.where(row_group == g, jnp.concatenate([b64, b64], axis=0), 0.0).astype(BF16)
        qb_scr[c, 0:KV_WIDTH, :] = qmat
        wk0 = pl.multiple_of(win_first_chunk(c) * KEY_CHUNK, KEY_CHUNK)
        s1_scr[0:r_win] = jnp.dot(kcmp, qmat, preferred_element_type=F32)
        s1_scr[r_win:r_own] = jnp.dot(kw_ref[0, pl.ds(wk0, WIN_KEYS), :], qmat,
                                      preferred_element_type=F32)
        s1_scr[r_own:r_end] = jnp.dot(ks_ref[0, pl.ds(t0, KEY_CHUNK), :], qmat,
                                      preferred_element_type=F32)

    def tile_exponentials(c):
        tq = c * Q_TILE + lane_q
        s = jnp.where(cmp_end <= tq, s1_scr[0:r_win], NEG_INF)
        p1_scr[0:r_win] = jnp.exp2(s - jnp.max(s, axis=0, keepdims=True)).astype(BF16)
        w0 = win_first_chunk(c)
        parts = []
        for u in range(win_chunks):
            ck = w0 + u
            kind = jnp.where(ck == c, MASK_CAUSAL,
                             jnp.where(ck > c, MASK_ALL,
                                       jnp.where(ck == c - WINDOW // KEY_CHUNK, MASK_WINDOW, MASK_NONE)))
            parts.append(s1_scr[r_win + u * KEY_CHUNK:r_win + (u + 1) * KEY_CHUNK] + mask_scr[kind])
        sw = jnp.concatenate(parts, axis=0)
        p1_scr[r_win:r_own] = jnp.exp2(sw - jnp.max(sw, axis=0, keepdims=True)).astype(BF16)
        sd = s1_scr[r_own:r_end] + mask_scr[MASK_CAUSAL]
        m_d = jnp.max(sd, axis=0, keepdims=True)
        p1_scr[r_own:r_end] = jnp.exp2(sd - m_d).astype(BF16)
        m_scr[c] = m_d

    def tile_outputs(c):
        t0 = c * Q_TILE
        tq = t0 + lane_q
        e16 = p1_scr[0:r_win]
        acc_c = jnp.dot(vct1, e16, preferred_element_type=F32)
        linv_c = jnp.where(tq >= CMP_BLOCK - 1, 1.0 / acc_c[HEAD_DIM:HEAD_DIM + 1], 0.0)
        o_c = acc_c[0:HEAD_DIM] * linv_c
        imp4 = jnp.dot(overlap_t, e16, preferred_element_type=F32) * linv_c
        imp = imp4[:, 0:Q_TILE]
        for h in range(1, HEADS_PER_GROUP):
            imp = imp + imp4[:, h * Q_TILE:(h + 1) * Q_TILE]

        cur = (t0 + lax.broadcasted_iota(jnp.int32, (1, Q_TILE), 1)) >> (SEL_BLOCK.bit_length() - 1)
        valid_b = blk_row <= cur
        bonus = jnp.where(blk_row == 0, FORCE_BONUS,
                          jnp.where(blk_row == cur, FORCE_BONUS,
                                    jnp.where(blk_row == cur - 1, FORCE_BONUS, 0.0)))
        score = jnp.where(valid_b, imp + bonus, -1.0)
        n_rb = n_blk // 8
        sblk = [score[8 * r:8 * r + 8] for r in range(n_rb)]
        cnt = [jnp.zeros((8, Q_TILE), F32) for _ in range(n_rb)]
        sub_row = lax.broadcasted_iota(jnp.int32, (8, Q_TILE), 0)
        for j in range(n_blk):
            rj = jnp.broadcast_to(score[j:j + 1], (8, Q_TILE))
            for r in range(n_rb):
                if 8 * r + 7 <= j:
                    cnt[r] = cnt[r] + jnp.where(rj > sblk[r], 1.0, 0.0)
                elif 8 * r > j:
                    cnt[r] = cnt[r] + jnp.where(rj >= sblk[r], 1.0, 0.0)
                else:
                    cnt[r] = cnt[r] + jnp.where(sub_row + 8 * r > j,
                                                jnp.where(rj >= sblk[r], 1.0, 0.0),
                                                jnp.where(rj > sblk[r], 1.0, 0.0))
        keep = jnp.concatenate(
            [jnp.where(blk_row[8 * r:8 * r + 8] < c * (Q_TILE // SEL_BLOCK),
                       jnp.where(cnt[r] < float(SEL_TOP_N), 0.0, NEG_INF), NEG_INF)
             for r in range(n_rb)], axis=0)
        qb_scr[c, KV_WIDTH:KV_WIDTH + n_blk, :] = jnp.concatenate(
            [keep] * HEADS_PER_GROUP, axis=1).astype(BF16)

        w0 = win_first_chunk(c)
        vw = jnp.concatenate([vwt_ref[0, w0 + u] for u in range(win_chunks)], axis=1)
        acc_w = jnp.dot(with_ones(vw), p1_scr[r_win:r_own], preferred_element_type=F32)
        acc_scr[c] = jnp.dot(with_ones(vst_ref[0, c]), p1_scr[r_own:r_end],
                             preferred_element_type=F32)
        o_scr[c] = gate_row(c, 0) * o_c + gate_row(c, 2) * normalized(acc_w)

    assert n_tiles >= 2
    tile_scores(0)
    tile_exponentials(0)
    tile_scores(1)

    @pl.loop(0, n_tiles - 2)
    def _(c):
        tile_outputs(c)
        tile_exponentials(c + 1)
        tile_scores(c + 2)

    tile_outputs(n_tiles - 2)
    tile_exponentials(n_tiles - 1)
    tile_outputs(n_tiles - 1)

    chunks_per_step = SEL_KEYS // KEY_CHUNK
    tiles_per_step = SEL_KEYS // Q_TILE
    n_sel_steps = T // SEL_KEYS
    assert n_tiles - (tiles_per_step * (n_sel_steps - 1) + 1) >= 2

    def sel_sweep(j):
        k0 = pl.multiple_of(j * SEL_KEYS, SEL_KEYS)
        first = tiles_per_step * j + 1
        k_ext = jnp.concatenate([ks_ref[0, pl.ds(k0, SEL_KEYS), :], onehot_scr[pl.ds(k0, SEL_KEYS), :]],
                                axis=1)
        vs = with_ones(jnp.concatenate(
            [vst_ref[0, chunks_per_step * j + u] for u in range(chunks_per_step)], axis=1))

        def scores(c):
            sc = jnp.dot(k_ext, qb_scr[c], preferred_element_type=F32)
            s_scr[...] = sc
            return jnp.max(sc, axis=0, keepdims=True)

        def exponentials(mx):
            p_scr[...] = jnp.exp2(s_scr[...] - mx).astype(BF16)

        def accumulate(c, mx):
            pv = jnp.dot(vs, p_scr[...], preferred_element_type=F32)
            m = m_scr[c]
            m_new = jnp.maximum(m, mx)
            acc_scr[c] = jnp.exp2(m - m_new) * acc_scr[c] + jnp.exp2(mx - m_new) * pv
            m_scr[c] = m_new

        mx_a = scores(first)
        exponentials(mx_a)
        mx_b = scores(first + 1)

        def body(c, carry):
            mx_c, mx_n = carry
            accumulate(c, mx_c)
            exponentials(mx_n)
            return mx_n, scores(c + 2)

        mx_a, mx_b = lax.fori_loop(first, n_tiles - 2, body, (mx_a, mx_b))
        accumulate(n_tiles - 2, mx_a)
        exponentials(mx_b)
        accumulate(n_tiles - 1, mx_b)

    pl.loop(0, n_sel_steps)(sel_sweep)

    @pl.loop(0, n_tiles)
    def _(c):
        t0 = pl.multiple_of(c * Q_TILE, Q_TILE)
        o_t = o_scr[c] + gate_row(c, 1) * normalized(acc_scr[c])
        stacked = jnp.concatenate(
            [o_t[:, h * Q_TILE:(h + 1) * Q_TILE] for h in range(HEADS_PER_GROUP)], axis=0)
        o = stacked.T * zn_ref[0, pl.ds(t0, Q_TILE), :].astype(F32)
        o_ref[0, pl.ds(t0, Q_TILE), :] = o.astype(BF16)


def _attn_call(q, kcmp, vcmpt, ksw, vt, gt, zn):
    B, T, _ = q.shape
    n_c = kcmp.shape[1]
    n_ck = T // KEY_CHUNK
    n_tiles = T // Q_TILE
    G = NSA_KV_HEADS
    gw = HEADS_PER_GROUP * HEAD_DIM
    in_specs = [
        pl.BlockSpec((1, T, gw), lambda b, g: (b, 0, g)),
        pl.BlockSpec((1, n_c, KV_WIDTH), lambda b, g: (b, 0, 0)),
        pl.BlockSpec((1, HEAD_DIM, n_c), lambda b, g: (b, g, 0)),
        pl.BlockSpec((1, T, KV_WIDTH), lambda b, g: (b, 0, 0)),
        pl.BlockSpec((1, T, KV_WIDTH), lambda b, g: (b, 0, 1)),
        pl.BlockSpec((1, n_ck, HEAD_DIM, KEY_CHUNK), lambda b, g: (b, 0, g, 0)),
        pl.BlockSpec((1, n_ck, HEAD_DIM, KEY_CHUNK), lambda b, g: (b, 0, G + g, 0)),
        pl.BlockSpec((1, n_ck, GATE_ROWS, KEY_CHUNK), lambda b, g: (b, 0, g, 0)),
        pl.BlockSpec((1, T, gw), lambda b, g: (b, 0, g)),
    ]
    return pl.pallas_call(
        _attn_kernel,
        out_shape=jax.ShapeDtypeStruct((B, T, NSA_WIDTH), BF16),
        grid=(B, G),
        in_specs=in_specs,
        out_specs=pl.BlockSpec((1, T, gw), lambda b, g: (b, 0, g)),
        scratch_shapes=[pltpu.VMEM((4, KEY_CHUNK, LANES), F32),
                        pltpu.VMEM((T, MXU_DEPTH - KV_WIDTH), BF16),
                        pltpu.VMEM((n_tiles, MXU_DEPTH, LANES), BF16),
                        pltpu.VMEM((n_tiles, 1, LANES), F32),
                        pltpu.VMEM((n_tiles, HEAD_DIM + ONES_ROWS, LANES), F32),
                        pltpu.VMEM((n_tiles, HEAD_DIM, LANES), F32),
                        pltpu.VMEM((SEL_KEYS, LANES), F32),
                        pltpu.VMEM((SEL_KEYS, LANES), BF16),
                        pltpu.VMEM((n_c + WIN_KEYS + KEY_CHUNK, LANES), F32),
                        pltpu.VMEM((n_c + WIN_KEYS + KEY_CHUNK, LANES), BF16)],
        name="attn",
        compiler_params=pltpu.CompilerParams(
            dimension_semantics=("arbitrary", "arbitrary"), vmem_limit_bytes=ATTN_VMEM_LIMIT),
    )(q, kcmp, vcmpt, ksw, ksw, vt, vt, gt, zn)


def _out_kernel(x_ref, mn_ref, mg_ref, wo_ref, g_ref, o_ref):
    h = x_ref[0]
    h = h + jnp.dot(mn_ref[0], wo_ref[0:NSA_WIDTH, :], preferred_element_type=F32)
    h = h + jnp.dot(mg_ref[0], wo_ref[NSA_WIDTH:, :], preferred_element_type=F32)
    ms = jnp.mean(h * h, axis=-1, keepdims=True)
    o_ref[0] = (h * lax.rsqrt(ms + NORM_EPS)) * g_ref[...]


def _out_call(x, mix_nsa, mix_gmlp, wo, g_f, tm):
    B, T, D = x.shape
    row = lambda w: pl.BlockSpec((1, tm, w), lambda b, i: (b, i, 0))
    return pl.pallas_call(
        _out_kernel,
        out_shape=jax.ShapeDtypeStruct((B, T, D), x.dtype),
        grid=(B, T // tm),
        in_specs=[row(D), row(NSA_WIDTH), row(GMLP_WIDTH),
                  pl.BlockSpec(wo.shape, lambda b, i: (0, 0)),
                  pl.BlockSpec((1, D), lambda b, i: (0, 0))],
        out_specs=row(D),
        name="out",
        compiler_params=pltpu.CompilerParams(
            dimension_semantics=("arbitrary", "arbitrary"), vmem_limit_bytes=VMEM_LIMIT),
    )(x, mix_nsa, mix_gmlp, wo, g_f)


def _split_w_in(w):
    sizes = (NSA_WIDTH, KV_WIDTH, KV_WIDTH, KV_WIDTH, KV_WIDTH, KV_WIDTH, KV_WIDTH, N_GATES,
             NSA_WIDTH, GMLP_WIDTH, GMLP_WIDTH, GMLP_WIDTH)
    parts, off = [], 0
    for s in sizes:
        parts.append(w[:, off:off + s])
        off += s
    return parts


def _gate_columns(w_gate):
    d = w_gate.shape[0]
    wg = w_gate.reshape(d, NSA_KV_HEADS, HEADS_PER_GROUP, 3).transpose(0, 1, 3, 2)
    wg = wg.reshape(d, NSA_KV_HEADS, 3 * HEADS_PER_GROUP)
    wg = jnp.pad(wg, ((0, 0), (0, 0), (0, GATE_ROWS - 3 * HEADS_PER_GROUP)))
    return wg.reshape(d, NSA_KV_HEADS * GATE_ROWS)


def _compress_w1(w1):
    half = CMP_BLOCK // 2
    w = w1.reshape(2, half, HEAD_DIM, CMP_HIDDEN)
    eye = jnp.eye(NSA_KV_HEADS, dtype=w1.dtype)
    full = jnp.einsum('hlde,gk->lgdhke', w, eye)
    return full.reshape(half * KV_WIDTH, 2 * NSA_KV_HEADS * CMP_HIDDEN)


def _compress_pe(pe):
    half = CMP_BLOCK // 2
    p = pe.reshape(2, half, 1, HEAD_DIM)
    p = jnp.broadcast_to(p, (2, half, NSA_KV_HEADS, HEAD_DIM))
    return p.reshape(2, half * KV_WIDTH)


def _block_diag2(w2):
    z = jnp.zeros_like(w2)
    return jnp.concatenate([jnp.concatenate([w2, z], axis=1),
                            jnp.concatenate([z, w2], axis=1)], axis=0)


def kernel(x, norm_in_g, w_in, w_cmp_k1, w_cmp_k2, pe_cmp_k, w_cmp_v1, w_cmp_v2, pe_cmp_v,
           gmlp_ln_g, gmlp_ln_b, w_spatial, b_spatial, w_out, norm_f_g):
    B, T, D = x.shape
    assert w_in.shape[0] == 1, "single-layer block"
    tm = min(PROJ_TILE, T)
    (wq, wkc, wvc, wks, wvs, wkw, wvw, wgt, wzn, wu, wv, wzg) = _split_w_in(w_in[0])
    wn = jnp.concatenate([wq, wkc, wvc, wks, wkw, wzn], axis=1).astype(BF16)
    wt = jnp.concatenate([wvs, wvw, _gate_columns(wgt), wu, wv, wzg], axis=1).T.astype(BF16)
    lng = gmlp_ln_g[0].reshape(GMLP_GROUPS, GMLP_GROUP_DIM, 1)
    lnb = gmlp_ln_b[0].reshape(GMLP_GROUPS, GMLP_GROUP_DIM, 1)
    ws_t = jnp.swapaxes(w_spatial[0], 1, 2)
    bs = b_spatial[0].reshape(GMLP_GROUPS, 1, GMLP_CHUNK)
    q, kc, vc, ksw, zn, vt, gt, o_gmlp = _proj_call(
        x, norm_in_g[0].reshape(1, D), wn, wt, lng, lnb, ws_t, bs, tm)

    n_c = T // CMP_STRIDE
    kcmp, vcmpt = _compress_call(
        kc.reshape(B, n_c, CMP_STRIDE * KV_WIDTH), vc.reshape(B, n_c, CMP_STRIDE * KV_WIDTH),
        _compress_pe(pe_cmp_k[0]), _compress_pe(pe_cmp_v[0]),
        _compress_w1(w_cmp_k1[0]).astype(BF16), _compress_w1(w_cmp_v1[0]).astype(BF16),
        _block_diag2(w_cmp_k2[0]).astype(BF16), _block_diag2(w_cmp_v2[0]).T.astype(BF16))

    o_nsa = _attn_call(q, kcmp, vcmpt, ksw, vt, gt, zn)
    return _out_call(x, o_nsa, o_gmlp, w_out[0].astype(BF16), norm_f_g.reshape(1, D), tm)
```

```python
import functools
import math

import jax
import jax.numpy as jnp
from jax import lax
from jax.experimental import pallas as pl
from jax.experimental.pallas import tpu as pltpu

F32 = jnp.float32
BF16 = jnp.bfloat16

HEAD_DIM = 64
NSA_HEADS = 8
NSA_KV_HEADS = 2
HEADS_PER_GROUP = NSA_HEADS // NSA_KV_HEADS
NSA_WIDTH = NSA_HEADS * HEAD_DIM
KV_WIDTH = NSA_KV_HEADS * HEAD_DIM
N_GATES = 3 * NSA_HEADS
GMLP_GROUPS = 8
GMLP_GROUP_DIM = 64
GMLP_WIDTH = GMLP_GROUPS * GMLP_GROUP_DIM
GMLP_CHUNK = 128
CMP_BLOCK = 32
CMP_STRIDE = 16
CMP_HIDDEN = 128
SEL_BLOCK = 64
SEL_TOP_N = 16
WINDOW = 512
NORM_EPS = 1e-6
NEG_INF = -1e30
FORCE_BONUS = 1e4

Q_TILE = 128
KEY_CHUNK = 128
LANES = HEADS_PER_GROUP * Q_TILE
SEL_KEYS = 512
SEL_GROUP = 4
WIN_KEYS = WINDOW + Q_TILE
ONES_ROWS = 16
LOG2_E = math.log2(math.e)
GATE_ROWS = 16
PROJ_TILE = 512
MXU_DEPTH = 256
VMEM_LIMIT = 48 * 1024 * 1024
ATTN_VMEM_LIMIT = 56 * 1024 * 1024


def _sigmoid(x):
    return 1.0 / (1.0 + jnp.exp(-x))


def _gelu_tanh(x):
    c = math.sqrt(2.0 / math.pi)
    return x * (0.5 * (1.0 + jnp.tanh(c * (x + 0.044715 * (x * x * x)))))


def _proj_kernel(x_ref, g_ref, wn_ref, wt_ref, lng_ref, lnb_ref, ws_ref, bs_ref,
                 q_ref, kc_ref, vc_ref, ksw_ref, zn_ref, vt_ref, gt_ref, og_ref):
    tm = x_ref.shape[1]
    n_ck = tm // GMLP_CHUNK
    x = x_ref[0]
    ms = jnp.mean(x * x, axis=-1, keepdims=True)
    hn = ((x * lax.rsqrt(ms + NORM_EPS)) * g_ref[...]).astype(BF16)
    pn = jnp.dot(hn, wn_ref[...], preferred_element_type=F32)
    q_ref[0] = (pn[:, 0:512] * (HEAD_DIM ** -0.5 * LOG2_E)).astype(BF16)
    kc_ref[0] = pn[:, 512:640].astype(BF16)
    vc_ref[0] = pn[:, 640:768].astype(BF16)
    ksw_ref[0] = pn[:, 768:1024].astype(BF16)
    z = pn[:, 1024:1536]
    zn_ref[0] = (z * _sigmoid(z)).astype(BF16)
    pt = lax.dot_general(wt_ref[...], hn, (((1,), (1,)), ((), ())),
                         preferred_element_type=F32)
    vt = pt[0:256].astype(BF16)
    gt = _sigmoid(pt[256:288])
    for ck in range(n_ck):
        sl = slice(ck * GMLP_CHUNK, (ck + 1) * GMLP_CHUNK)
        vt_ref[0, ck] = vt[:, sl]
        gt_ref[0, ck] = gt[:, sl]
    gu = _gelu_tanh(pt[288:800])
    gv = _gelu_tanh(pt[800:1312]).reshape(GMLP_GROUPS, GMLP_GROUP_DIM, tm)
    zg = pt[1312:1824]
    mu = jnp.mean(gv, axis=1, keepdims=True)
    dv = gv - mu
    var = jnp.mean(dv * dv, axis=1, keepdims=True)
    vn = (dv * lax.rsqrt(var + NORM_EPS)) * lng_ref[...] + lnb_ref[...]
    s_i = lax.broadcasted_iota(jnp.int32, (GMLP_CHUNK, GMLP_CHUNK), 0)
    t_i = lax.broadcasted_iota(jnp.int32, (GMLP_CHUNK, GMLP_CHUNK), 1)
    causal = s_i <= t_i
    outs = []
    for g in range(GMLP_GROUPS):
        a = vn[g].astype(BF16)
        a_st = jnp.concatenate(
            [a[:, ck * GMLP_CHUNK:(ck + 1) * GMLP_CHUNK] for ck in range(n_ck)], axis=0)
        w = jnp.where(causal, ws_ref[g], 0.0).astype(BF16)
        r = jnp.dot(a_st, w, preferred_element_type=F32) + bs_ref[g]
        outs.append(jnp.concatenate(
            [r[ck * GMLP_GROUP_DIM:(ck + 1) * GMLP_GROUP_DIM] for ck in range(n_ck)], axis=1))
    mixed = jnp.concatenate(outs, axis=0)
    og_t = (gu * mixed) * (zg * _sigmoid(zg))
    og_ref[0] = og_t.T.astype(BF16)


def _proj_call(x, g_in, wn, wt, lng, lnb, ws_t, bs, tm):
    B, T, D = x.shape
    n_ck = tm // GMLP_CHUNK
    grid = (B, T // tm)
    const = lambda *shape: pl.BlockSpec(shape, lambda b, i: (0,) * len(shape))
    row = lambda w: pl.BlockSpec((1, tm, w), lambda b, i: (b, i, 0))
    out_shape = (
        jax.ShapeDtypeStruct((B, T, NSA_WIDTH), BF16),
        jax.ShapeDtypeStruct((B, T, KV_WIDTH), BF16),
        jax.ShapeDtypeStruct((B, T, KV_WIDTH), BF16),
        jax.ShapeDtypeStruct((B, T, 2 * KV_WIDTH), BF16),
        jax.ShapeDtypeStruct((B, T, NSA_WIDTH), BF16),
        jax.ShapeDtypeStruct((B, T // GMLP_CHUNK, 256, GMLP_CHUNK), BF16),
        jax.ShapeDtypeStruct((B, T // GMLP_CHUNK, 2 * GATE_ROWS, GMLP_CHUNK), F32),
        jax.ShapeDtypeStruct((B, T, GMLP_WIDTH), BF16),
    )
    out_specs = (
        row(NSA_WIDTH), row(KV_WIDTH), row(KV_WIDTH), row(2 * KV_WIDTH), row(NSA_WIDTH),
        pl.BlockSpec((1, n_ck, 256, GMLP_CHUNK), lambda b, i: (b, i, 0, 0)),
        pl.BlockSpec((1, n_ck, 2 * GATE_ROWS, GMLP_CHUNK), lambda b, i: (b, i, 0, 0)),
        row(GMLP_WIDTH),
    )
    in_specs = [
        pl.BlockSpec((1, tm, D), lambda b, i: (b, i, 0)),
        const(1, D), const(*wn.shape), const(*wt.shape),
        const(*lng.shape), const(*lnb.shape), const(*ws_t.shape), const(*bs.shape),
    ]
    return pl.pallas_call(
        _proj_kernel, out_shape=out_shape, grid=grid, in_specs=in_specs, out_specs=out_specs,
        name="proj",
        compiler_params=pltpu.CompilerParams(
            dimension_semantics=("arbitrary", "arbitrary"), vmem_limit_bytes=VMEM_LIMIT),
    )(x, g_in, wn, wt, lng, lnb, ws_t, bs)


def _compress_kernel(kc_ref, vc_ref, pek_ref, pev_ref, wk1_ref, wv1_ref, wk2_ref, wv2t_ref,
                     kcmp_ref, vcmpt_ref):
    n_c = kc_ref.shape[1]
    half = 2 * CMP_HIDDEN

    def hidden(src_ref, pe_ref, w1_ref):
        xs = src_ref[0].astype(F32)
        xa = (xs + pe_ref[0:1, :]).astype(BF16)
        xb = (xs + pe_ref[1:2, :]).astype(BF16)
        ha = jnp.dot(xa, w1_ref[:, 0:half], preferred_element_type=F32)
        hb = jnp.dot(xb, w1_ref[:, half:2 * half], preferred_element_type=F32)
        hb_next = pltpu.roll(hb, shift=n_c - 1, axis=0)
        return _gelu_tanh(ha + hb_next).astype(BF16)

    hk = hidden(kc_ref, pek_ref, wk1_ref)
    kcmp_ref[0] = jnp.dot(hk, wk2_ref[...], preferred_element_type=F32).astype(BF16)
    hv = hidden(vc_ref, pev_ref, wv1_ref)
    vcmpt_ref[0] = lax.dot_general(wv2t_ref[...], hv, (((1,), (1,)), ((), ())),
                                   preferred_element_type=F32).astype(BF16)


def _compress_call(kc2, vc2, pek, pev, wk1, wv1, wk2, wv2t):
    B, n_c, W = kc2.shape
    const = lambda a: pl.BlockSpec(a.shape, lambda b: (0,) * a.ndim)
    return pl.pallas_call(
        _compress_kernel,
        out_shape=(jax.ShapeDtypeStruct((B, n_c, KV_WIDTH), BF16),
                   jax.ShapeDtypeStruct((B, KV_WIDTH, n_c), BF16)),
        grid=(B,),
        in_specs=[pl.BlockSpec((1, n_c, W), lambda b: (b, 0, 0)),
                  pl.BlockSpec((1, n_c, W), lambda b: (b, 0, 0)),
                  const(pek), const(pev), const(wk1), const(wv1), const(wk2), const(wv2t)],
        out_specs=(pl.BlockSpec((1, n_c, KV_WIDTH), lambda b: (b, 0, 0)),
                   pl.BlockSpec((1, KV_WIDTH, n_c), lambda b: (b, 0, 0))),
        name="compress",
        compiler_params=pltpu.CompilerParams(
            dimension_semantics=("arbitrary",), vmem_limit_bytes=VMEM_LIMIT),
    )(kc2, vc2, pek, pev, wk1, wv1, wk2, wv2t)


def _attn_kernel(q_ref, kcmp_ref, vcmpt_ref, ks_ref, kw_ref, vst_ref, vwt_ref, gt_ref, zn_ref,
                 o_ref, mask_scr, onehot_scr, qb_scr, m_scr, acc_scr, o_scr, s_scr, p_scr,
                 s1_scr, p1_scr):
    g = pl.program_id(1)
    T = q_ref.shape[1]
    n_c = kcmp_ref.shape[1]
    n_blk = T // SEL_BLOCK
    n_tiles = T // Q_TILE
    assert KV_WIDTH + n_blk <= MXU_DEPTH

    def with_ones(vt):
        return jnp.concatenate([vt, jnp.ones((ONES_ROWS, vt.shape[1]), vt.dtype)], axis=0)

    def normalized(acc):
        return acc[0:HEAD_DIM] / acc[HEAD_DIM:HEAD_DIM + 1]

    def gate_row(c, br):
        gt = gt_ref[0, c]
        return jnp.concatenate(
            [gt[4 * br + h:4 * br + h + 1] for h in range(HEADS_PER_GROUP)], axis=1)

    key_i = lax.broadcasted_iota(jnp.int32, (T, MXU_DEPTH - KV_WIDTH), 0)
    blk_i = lax.broadcasted_iota(jnp.int32, (T, MXU_DEPTH - KV_WIDTH), 1)
    onehot_scr[...] = jnp.where((key_i >> (SEL_BLOCK.bit_length() - 1)) == blk_i, 1.0, 0.0).astype(BF16)
    if KV_WIDTH + n_blk < MXU_DEPTH:
        qb_scr[:, KV_WIDTH + n_blk:, :] = jnp.zeros(
            (n_tiles, MXU_DEPTH - KV_WIDTH - n_blk, LANES), BF16)

    kcmp = kcmp_ref[0]
    vct1 = with_ones(vcmpt_ref[0])
    jb_c = lax.broadcasted_iota(jnp.int32, (n_blk, n_c), 0)
    ic_c = lax.broadcasted_iota(jnp.int32, (n_blk, n_c), 1)
    overlap_t = jnp.where(ic_c * CMP_STRIDE < (jb_c + 1) * SEL_BLOCK,
                          jnp.where(ic_c * CMP_STRIDE + (CMP_BLOCK - 1) >= jb_c * SEL_BLOCK, 1.0, 0.0),
                          0.0).astype(BF16)
    row_group = lax.broadcasted_iota(jnp.int32, (KV_WIDTH, LANES), 0) >> (HEAD_DIM.bit_length() - 1)
    lane_q = lax.broadcasted_iota(jnp.int32, (1, LANES), 1) & (Q_TILE - 1)
    key_row = lax.broadcasted_iota(jnp.int32, (KEY_CHUNK, LANES), 0)
    MASK_CAUSAL, MASK_WINDOW, MASK_NONE, MASK_ALL = range(4)
    mask_scr[MASK_CAUSAL] = jnp.where(key_row <= lane_q, 0.0, NEG_INF)
    mask_scr[MASK_WINDOW] = jnp.where(key_row > lane_q, 0.0, NEG_INF)
    mask_scr[MASK_NONE] = jnp.zeros((KEY_CHUNK, LANES), F32)
    mask_scr[MASK_ALL] = jnp.full((KEY_CHUNK, LANES), NEG_INF, F32)
    cmp_end = lax.broadcasted_iota(jnp.int32, (n_c, LANES), 0) * CMP_STRIDE + (CMP_BLOCK - 1)
    blk_row = lax.broadcasted_iota(jnp.int32, (n_blk, Q_TILE), 0)

    win_chunks = WIN_KEYS // KEY_CHUNK
    r_win, r_own, r_end = n_c, n_c + WIN_KEYS, n_c + WIN_KEYS + KEY_CHUNK

    def win_first_chunk(c):
        return jnp.maximum(c - WINDOW // KEY_CHUNK, 0)

    def tile_scores(c):
        t0 = c * Q_TILE if isinstance(c, int) else pl.multiple_of(c * Q_TILE, Q_TILE)
        qt = q_ref[0, pl.ds(t0, Q_TILE), :].astype(F32).T
        b64 = jnp.concatenate(
            [qt[h * HEAD_DIM:(h + 1) * HEAD_DIM] for h in range(HEADS_PER_GROUP)], axis=1)
        qmat = jnp.where(row_group == g, jnp.concatenate([b64, b64], axis=0), 0.0).astype(BF16)
        qb_scr[c, 0:KV_WIDTH, :] = qmat
        wk0 = pl.multiple_of(win_first_chunk(c) * KEY_CHUNK, KEY_CHUNK)
        s1_scr[0:r_win] = jnp.dot(kcmp, qmat, preferred_element_type=F32)
        s1_scr[r_win:r_own] = jnp.dot(kw_ref[0, pl.ds(wk0, WIN_KEYS), :], qmat,
                                      preferred_element_type=F32)
        s1_scr[r_own:r_end] = jnp.dot(ks_ref[0, pl.ds(t0, KEY_CHUNK), :], qmat,
                                      preferred_element_type=F32)

    def tile_exponentials(c):
        tq = c * Q_TILE + lane_q
        s = jnp.where(cmp_end <= tq, s1_scr[0:r_win], NEG_INF)
        p1_scr[0:r_win] = jnp.exp2(s - jnp.max(s, axis=0, keepdims=True)).astype(BF16)
        w0 = win_first_chunk(c)
        parts = []
        for u in range(win_chunks):
            ck = w0 + u
            kind = jnp.where(ck == c, MASK_CAUSAL,
                             jnp.where(ck > c, MASK_ALL,
                                       jnp.where(ck == c - WINDOW // KEY_CHUNK, MASK_WINDOW, MASK_NONE)))
            parts.append(s1_scr[r_win + u * KEY_CHUNK:r_win + (u + 1) * KEY_CHUNK] + mask_scr[kind])
        sw = jnp.concatenate(parts, axis=0)
        p1_scr[r_win:r_own] = jnp.exp2(sw - jnp.max(sw, axis=0, keepdims=True)).astype(BF16)
        sd = s1_scr[r_own:r_end] + mask_scr[MASK_CAUSAL]
        m_d = jnp.max(sd, axis=0, keepdims=True)
        p1_scr[r_own:r_end] = jnp.exp2(sd - m_d).astype(BF16)
        m_scr[c] = m_d

    def tile_outputs(c):
        t0 = c * Q_TILE
        tq = t0 + lane_q
        e16 = p1_scr[0:r_win]
        acc_c = jnp.dot(vct1, e16, preferred_element_type=F32)
        linv_c = jnp.where(tq >= CMP_BLOCK - 1, 1.0 / acc_c[HEAD_DIM:HEAD_DIM + 1], 0.0)
        o_c = acc_c[0:HEAD_DIM] * linv_c
        imp4 = jnp.dot(overlap_t, e16, preferred_element_type=F32) * linv_c
        imp = imp4[:, 0:Q_TILE]
        for h in range(1, HEADS_PER_GROUP):
            imp = imp + imp4[:, h * Q_TILE:(h + 1) * Q_TILE]

        cur = (t0 + lax.broadcasted_iota(jnp.int32, (1, Q_TILE), 1)) >> (SEL_BLOCK.bit_length() - 1)
        valid_b = blk_row <= cur
        bonus = jnp.where(blk_row == 0, FORCE_BONUS,
                          jnp.where(blk_row == cur, FORCE_BONUS,
                                    jnp.where(blk_row == cur - 1, FORCE_BONUS, 0.0)))
        score = jnp.where(valid_b, imp + bonus, -1.0)
        n_rb = n_blk // 8
        sblk = [score[8 * r:8 * r + 8] for r in range(n_rb)]
        cnt = [jnp.zeros((8, Q_TILE), F32) for _ in range(n_rb)]
        sub_row = lax.broadcasted_iota(jnp.int32, (8, Q_TILE), 0)
        for j in range(n_blk):
            rj = jnp.broadcast_to(score[j:j + 1], (8, Q_TILE))
            for r in range(n_rb):
                if 8 * r + 7 <= j:
                    cnt[r] = cnt[r] + jnp.where(rj > sblk[r], 1.0, 0.0)
                elif 8 * r > j:
                    cnt[r] = cnt[r] + jnp.where(rj >= sblk[r], 1.0, 0.0)
                else:
                    cnt[r] = cnt[r] + jnp.where(sub_row + 8 * r > j,
                                                jnp.where(rj >= sblk[r], 1.0, 0.0),
                                                jnp.where(rj > sblk[r], 1.0, 0.0))
        keep = jnp.concatenate(
            [jnp.where(blk_row[8 * r:8 * r + 8] < c * (Q_TILE // SEL_BLOCK),
                       jnp.where(cnt[r] < float(SEL_TOP_N), 0.0, NEG_INF), NEG_INF)
             for r in range(n_rb)], axis=0)
        qb_scr[c, KV_WIDTH:KV_WIDTH + n_blk, :] = jnp.concatenate(
            [keep] * HEADS_PER_GROUP, axis=1).astype(BF16)

        w0 = win_first_chunk(c)
        vw = jnp.concatenate([vwt_ref[0, w0 + u] for u in range(win_chunks)], axis=1)
        acc_w = jnp.dot(with_ones(vw), p1_scr[r_win:r_own], preferred_element_type=F32)
        acc_scr[c] = jnp.dot(with_ones(vst_ref[0, c]), p1_scr[r_own:r_end],
                             preferred_element_type=F32)
        o_scr[c] = gate_row(c, 0) * o_c + gate_row(c, 2) * normalized(acc_w)

    assert n_tiles >= 2
    tile_scores(0)
    tile_exponentials(0)
    tile_scores(1)

    @pl.loop(0, n_tiles - 2)
    def _(c):
        tile_outputs(c)
        tile_exponentials(c + 1)
        tile_scores(c + 2)

    tile_outputs(n_tiles - 2)
    tile_exponentials(n_tiles - 1)
    tile_outputs(n_tiles - 1)

    chunks_per_step = SEL_KEYS // KEY_CHUNK
    n_sel_steps = T // SEL_KEYS
    n_groups = n_tiles // SEL_GROUP
    assert SEL_KEYS % (SEL_GROUP * Q_TILE) == 0 and n_tiles % SEL_GROUP == 0
    groups_per_step = SEL_KEYS // (SEL_GROUP * Q_TILE)
    n_items = sum(n_groups - groups_per_step * j for j in range(n_sel_steps))
    assert n_items >= 2

    def next_item(item):
        j, q = item
        wrap = q + 1 == n_groups
        return jnp.where(wrap, j + 1, j), jnp.where(wrap, groups_per_step * (j + 1), q + 1)

    def sel_scores(item):
        j, q = item
        k0 = pl.multiple_of(j * SEL_KEYS, SEL_KEYS)
        k_ext = jnp.concatenate([ks_ref[0, pl.ds(k0, SEL_KEYS), :], onehot_scr[pl.ds(k0, SEL_KEYS), :]],
                                axis=1)
        mxs = []
        for i in range(SEL_GROUP):
            sc = jnp.dot(k_ext, qb_scr[q * SEL_GROUP + i], preferred_element_type=F32)
            s_scr[i] = sc
            mxs.append(jnp.max(sc, axis=0, keepdims=True))
        return tuple(mxs)

    def sel_exponentials(mxs):
        for i in range(SEL_GROUP):
            p_scr[i] = jnp.exp2(s_scr[i] - mxs[i]).astype(BF16)

    def sel_accumulate(item, mxs):
        j, q = item
        vs = with_ones(jnp.concatenate(
            [vst_ref[0, chunks_per_step * j + u] for u in range(chunks_per_step)], axis=1))
        for i in range(SEL_GROUP):
            c = q * SEL_GROUP + i
            pv = jnp.dot(vs, p_scr[i], preferred_element_type=F32)
            m = m_scr[c]
            m_new = jnp.maximum(m, mxs[i])
            acc_scr[c] = jnp.exp2(m - m_new) * acc_scr[c] + jnp.exp2(mxs[i] - m_new) * pv
            m_scr[c] = m_new

    item_a = (jnp.int32(0), jnp.int32(0))
    item_b = next_item(item_a)
    mx_a = sel_scores(item_a)
    sel_exponentials(mx_a)
    mx_b = sel_scores(item_b)

    def sel_body(_, carry):
        item_a, item_b, mx_a, mx_b = carry
        item_c = next_item(item_b)
        sel_accumulate(item_a, mx_a)
        sel_exponentials(mx_b)
        return item_b, item_c, mx_b, sel_scores(item_c)

    item_a, item_b, mx_a, mx_b = lax.fori_loop(0, n_items - 2, sel_body, (item_a, item_b, mx_a, mx_b))
    sel_accumulate(item_a, mx_a)
    sel_exponentials(mx_b)
    sel_accumulate(item_b, mx_b)

    @pl.loop(0, n_tiles)
    def _(c):
        t0 = pl.multiple_of(c * Q_TILE, Q_TILE)
        o_t = o_scr[c] + gate_row(c, 1) * normalized(acc_scr[c])
        stacked = jnp.concatenate(
            [o_t[:, h * Q_TILE:(h + 1) * Q_TILE] for h in range(HEADS_PER_GROUP)], axis=0)
        o = stacked.T * zn_ref[0, pl.ds(t0, Q_TILE), :].astype(F32)
        o_ref[0, pl.ds(t0, Q_TILE), :] = o.astype(BF16)


def _attn_call(q, kcmp, vcmpt, ksw, vt, gt, zn):
    B, T, _ = q.shape
    n_c = kcmp.shape[1]
    n_ck = T // KEY_CHUNK
    n_tiles = T // Q_TILE
    G = NSA_KV_HEADS
    gw = HEADS_PER_GROUP * HEAD_DIM
    in_specs = [
        pl.BlockSpec((1, T, gw), lambda b, g: (b, 0, g)),
        pl.BlockSpec((1, n_c, KV_WIDTH), lambda b, g: (b, 0, 0)),
        pl.BlockSpec((1, HEAD_DIM, n_c), lambda b, g: (b, g, 0)),
        pl.BlockSpec((1, T, KV_WIDTH), lambda b, g: (b, 0, 0)),
        pl.BlockSpec((1, T, KV_WIDTH), lambda b, g: (b, 0, 1)),
        pl.BlockSpec((1, n_ck, HEAD_DIM, KEY_CHUNK), lambda b, g: (b, 0, g, 0)),
        pl.BlockSpec((1, n_ck, HEAD_DIM, KEY_CHUNK), lambda b, g: (b, 0, G + g, 0)),
        pl.BlockSpec((1, n_ck, GATE_ROWS, KEY_CHUNK), lambda b, g: (b, 0, g, 0)),
        pl.BlockSpec((1, T, gw), lambda b, g: (b, 0, g)),
    ]
    return pl.pallas_call(
        _attn_kernel,
        out_shape=jax.ShapeDtypeStruct((B, T, NSA_WIDTH), BF16),
        grid=(B, G),
        in_specs=in_specs,
        out_specs=pl.BlockSpec((1, T, gw), lambda b, g: (b, 0, g)),
        scratch_shapes=[pltpu.VMEM((4, KEY_CHUNK, LANES), F32),
                        pltpu.VMEM((T, MXU_DEPTH - KV_WIDTH), BF16),
                        pltpu.VMEM((n_tiles, MXU_DEPTH, LANES), BF16),
                        pltpu.VMEM((n_tiles, 1, LANES), F32),
                        pltpu.VMEM((n_tiles, HEAD_DIM + ONES_ROWS, LANES), F32),
                        pltpu.VMEM((n_tiles, HEAD_DIM, LANES), F32),
                        pltpu.VMEM((SEL_GROUP, SEL_KEYS, LANES), F32),
                        pltpu.VMEM((SEL_GROUP, SEL_KEYS, LANES), BF16),
                        pltpu.VMEM((n_c + WIN_KEYS + KEY_CHUNK, LANES), F32),
                        pltpu.VMEM((n_c + WIN_KEYS + KEY_CHUNK, LANES), BF16)],
        name="attn",
        compiler_params=pltpu.CompilerParams(
            dimension_semantics=("arbitrary", "arbitrary"), vmem_limit_bytes=ATTN_VMEM_LIMIT),
    )(q, kcmp, vcmpt, ksw, ksw, vt, vt, gt, zn)


def _out_kernel(x_ref, mn_ref, mg_ref, wo_ref, g_ref, o_ref):
    h = x_ref[0]
    h = h + jnp.dot(mn_ref[0], wo_ref[0:NSA_WIDTH, :], preferred_element_type=F32)
    h = h + jnp.dot(mg_ref[0], wo_ref[NSA_WIDTH:, :], preferred_element_type=F32)
    ms = jnp.mean(h * h, axis=-1, keepdims=True)
    o_ref[0] = (h * lax.rsqrt(ms + NORM_EPS)) * g_ref[...]


def _out_call(x, mix_nsa, mix_gmlp, wo, g_f, tm):
    B, T, D = x.shape
    row = lambda w: pl.BlockSpec((1, tm, w), lambda b, i: (b, i, 0))
    return pl.pallas_call(
        _out_kernel,
        out_shape=jax.ShapeDtypeStruct((B, T, D), x.dtype),
        grid=(B, T // tm),
        in_specs=[row(D), row(NSA_WIDTH), row(GMLP_WIDTH),
                  pl.BlockSpec(wo.shape, lambda b, i: (0, 0)),
                  pl.BlockSpec((1, D), lambda b, i: (0, 0))],
        out_specs=row(D),
        name="out",
        compiler_params=pltpu.CompilerParams(
            dimension_semantics=("arbitrary", "arbitrary"), vmem_limit_bytes=VMEM_LIMIT),
    )(x, mix_nsa, mix_gmlp, wo, g_f)


def _split_w_in(w):
    sizes = (NSA_WIDTH, KV_WIDTH, KV_WIDTH, KV_WIDTH, KV_WIDTH, KV_WIDTH, KV_WIDTH, N_GATES,
             NSA_WIDTH, GMLP_WIDTH, GMLP_WIDTH, GMLP_WIDTH)
    parts, off = [], 0
    for s in sizes:
        parts.append(w[:, off:off + s])
        off += s
    return parts


def _gate_columns(w_gate):
    d = w_gate.shape[0]
    wg = w_gate.reshape(d, NSA_KV_HEADS, HEADS_PER_GROUP, 3).transpose(0, 1, 3, 2)
    wg = wg.reshape(d, NSA_KV_HEADS, 3 * HEADS_PER_GROUP)
    wg = jnp.pad(wg, ((0, 0), (0, 0), (0, GATE_ROWS - 3 * HEADS_PER_GROUP)))
    return wg.reshape(d, NSA_KV_HEADS * GATE_ROWS)


def _compress_w1(w1):
    half = CMP_BLOCK // 2
    w = w1.reshape(2, half, HEAD_DIM, CMP_HIDDEN)
    eye = jnp.eye(NSA_KV_HEADS, dtype=w1.dtype)
    full = jnp.einsum('hlde,gk->lgdhke', w, eye)
    return full.reshape(half * KV_WIDTH, 2 * NSA_KV_HEADS * CMP_HIDDEN)


def _compress_pe(pe):
    half = CMP_BLOCK // 2
    p = pe.reshape(2, half, 1, HEAD_DIM)
    p = jnp.broadcast_to(p, (2, half, NSA_KV_HEADS, HEAD_DIM))
    return p.reshape(2, half * KV_WIDTH)


def _block_diag2(w2):
    z = jnp.zeros_like(w2)
    return jnp.concatenate([jnp.concatenate([w2, z], axis=1),
                            jnp.concatenate([z, w2], axis=1)], axis=0)


def kernel(x, norm_in_g, w_in, w_cmp_k1, w_cmp_k2, pe_cmp_k, w_cmp_v1, w_cmp_v2, pe_cmp_v,
           gmlp_ln_g, gmlp_ln_b, w_spatial, b_spatial, w_out, norm_f_g):
    B, T, D = x.shape
    assert w_in.shape[0] == 1, "single-layer block"
    tm = min(PROJ_TILE, T)
    (wq, wkc, wvc, wks, wvs, wkw, wvw, wgt, wzn, wu, wv, wzg) = _split_w_in(w_in[0])
    wn = jnp.concatenate([wq, wkc, wvc, wks, wkw, wzn], axis=1).astype(BF16)
    wt = jnp.concatenate([wvs, wvw, _gate_columns(wgt), wu, wv, wzg], axis=1).T.astype(BF16)
    lng = gmlp_ln_g[0].reshape(GMLP_GROUPS, GMLP_GROUP_DIM, 1)
    lnb = gmlp_ln_b[0].reshape(GMLP_GROUPS, GMLP_GROUP_DIM, 1)
    ws_t = jnp.swapaxes(w_spatial[0], 1, 2)
    bs = b_spatial[0].reshape(GMLP_GROUPS, 1, GMLP_CHUNK)
    q, kc, vc, ksw, zn, vt, gt, o_gmlp = _proj_call(
        x, norm_in_g[0].reshape(1, D), wn, wt, lng, lnb, ws_t, bs, tm)

    n_c = T // CMP_STRIDE
    kcmp, vcmpt = _compress_call(
        kc.reshape(B, n_c, CMP_STRIDE * KV_WIDTH), vc.reshape(B, n_c, CMP_STRIDE * KV_WIDTH),
        _compress_pe(pe_cmp_k[0]), _compress_pe(pe_cmp_v[0]),
        _compress_w1(w_cmp_k1[0]).astype(BF16), _compress_w1(w_cmp_v1[0]).astype(BF16),
        _block_diag2(w_cmp_k2[0]).astype(BF16), _block_diag2(w_cmp_v2[0]).T.astype(BF16))

    o_nsa = _attn_call(q, kcmp, vcmpt, ksw, vt, gt, zn)
    return _out_call(x, o_nsa, o_gmlp, w_out[0].astype(BF16), norm_f_g.reshape(1, D), tm)
```

```python
import functools
import math

import jax
import jax.numpy as jnp
from jax import lax
from jax.experimental import pallas as pl
from jax.experimental.pallas import tpu as pltpu

F32 = jnp.float32
BF16 = jnp.bfloat16

HEAD_DIM = 64
NSA_HEADS = 8
NSA_KV_HEADS = 2
HEADS_PER_GROUP = NSA_HEADS // NSA_KV_HEADS
NSA_WIDTH = NSA_HEADS * HEAD_DIM
KV_WIDTH = NSA_KV_HEADS * HEAD_DIM
N_GATES = 3 * NSA_HEADS
GMLP_GROUPS = 8
GMLP_GROUP_DIM = 64
GMLP_WIDTH = GMLP_GROUPS * GMLP_GROUP_DIM
GMLP_CHUNK = 128
CMP_BLOCK = 32
CMP_STRIDE = 16
CMP_HIDDEN = 128
SEL_BLOCK = 64
SEL_TOP_N = 16
WINDOW = 512
NORM_EPS = 1e-6
NEG_INF = -1e30
FORCE_BONUS = 1e4

Q_TILE = 128
KEY_CHUNK = 128
LANES = HEADS_PER_GROUP * Q_TILE
SEL_KEYS = 512
SEL_GROUP = 4
WIN_KEYS = WINDOW + Q_TILE
ONES_ROWS = 16
LOG2_E = math.log2(math.e)
GATE_ROWS = 16
PROJ_TILE = 512
MXU_DEPTH = 256
VMEM_LIMIT = 48 * 1024 * 1024
ATTN_VMEM_LIMIT = 56 * 1024 * 1024


def _sigmoid(x):
    return 1.0 / (1.0 + jnp.exp(-x))


def _gelu_tanh(x):
    c = math.sqrt(2.0 / math.pi)
    return x * (0.5 * (1.0 + jnp.tanh(c * (x + 0.044715 * (x * x * x)))))


def _proj_kernel(x_ref, g_ref, wn_ref, wt_ref, lng_ref, lnb_ref, ws_ref, bs_ref,
                 q_ref, kc_ref, vc_ref, ksw_ref, zn_ref, vt_ref, gt_ref, og_ref, cmp_scr):
    tm = x_ref.shape[1]
    n_ck = tm // GMLP_CHUNK
    x = x_ref[0]
    ms = jnp.mean(x * x, axis=-1, keepdims=True)
    hn = ((x * lax.rsqrt(ms + NORM_EPS)) * g_ref[...]).astype(BF16)
    pn = jnp.dot(hn, wn_ref[...], preferred_element_type=F32)
    q_ref[0] = (pn[:, 0:512] * (HEAD_DIM ** -0.5 * LOG2_E)).astype(BF16)
    for src, dst_ref, lo in ((0, kc_ref, 512), (1, vc_ref, 640)):
        cmp_scr[src] = pn[:, lo:lo + KV_WIDTH]
        for pos in range(CMP_STRIDE):
            dst_ref[0, :, pos * KV_WIDTH:(pos + 1) * KV_WIDTH] = cmp_scr[
                src, pl.ds(pos, tm // CMP_STRIDE, stride=CMP_STRIDE), :].astype(BF16)
    ksw_ref[0] = pn[:, 768:1024].astype(BF16)
    z = pn[:, 1024:1536]
    zn_ref[0] = (z * _sigmoid(z)).astype(BF16)
    pt = lax.dot_general(wt_ref[...], hn, (((1,), (1,)), ((), ())),
                         preferred_element_type=F32)
    vt = pt[0:256].astype(BF16)
    gt = _sigmoid(pt[256:288])
    for ck in range(n_ck):
        sl = slice(ck * GMLP_CHUNK, (ck + 1) * GMLP_CHUNK)
        vt_ref[0, ck] = vt[:, sl]
        gt_ref[0, ck] = gt[:, sl]
    gu = _gelu_tanh(pt[288:800])
    gv = _gelu_tanh(pt[800:1312]).reshape(GMLP_GROUPS, GMLP_GROUP_DIM, tm)
    zg = pt[1312:1824]
    mu = jnp.mean(gv, axis=1, keepdims=True)
    dv = gv - mu
    var = jnp.mean(dv * dv, axis=1, keepdims=True)
    vn = (dv * lax.rsqrt(var + NORM_EPS)) * lng_ref[...] + lnb_ref[...]
    s_i = lax.broadcasted_iota(jnp.int32, (GMLP_CHUNK, GMLP_CHUNK), 0)
    t_i = lax.broadcasted_iota(jnp.int32, (GMLP_CHUNK, GMLP_CHUNK), 1)
    causal = s_i <= t_i
    outs = []
    for g in range(GMLP_GROUPS):
        a = vn[g].astype(BF16)
        a_st = jnp.concatenate(
            [a[:, ck * GMLP_CHUNK:(ck + 1) * GMLP_CHUNK] for ck in range(n_ck)], axis=0)
        w = jnp.where(causal, ws_ref[g], 0.0).astype(BF16)
        r = jnp.dot(a_st, w, preferred_element_type=F32) + bs_ref[g]
        outs.append(jnp.concatenate(
            [r[ck * GMLP_GROUP_DIM:(ck + 1) * GMLP_GROUP_DIM] for ck in range(n_ck)], axis=1))
    mixed = jnp.concatenate(outs, axis=0)
    og_t = (gu * mixed) * (zg * _sigmoid(zg))
    og_ref[0] = og_t.T.astype(BF16)


def _proj_call(x, g_in, wn, wt, lng, lnb, ws_t, bs, tm):
    B, T, D = x.shape
    n_ck = tm // GMLP_CHUNK
    grid = (B, T // tm)
    const = lambda *shape: pl.BlockSpec(shape, lambda b, i: (0,) * len(shape))
    row = lambda w: pl.BlockSpec((1, tm, w), lambda b, i: (b, i, 0))
    out_shape = (
        jax.ShapeDtypeStruct((B, T, NSA_WIDTH), BF16),
        jax.ShapeDtypeStruct((B, T // CMP_STRIDE, CMP_STRIDE * KV_WIDTH), BF16),
        jax.ShapeDtypeStruct((B, T // CMP_STRIDE, CMP_STRIDE * KV_WIDTH), BF16),
        jax.ShapeDtypeStruct((B, T, 2 * KV_WIDTH), BF16),
        jax.ShapeDtypeStruct((B, T, NSA_WIDTH), BF16),
        jax.ShapeDtypeStruct((B, T // GMLP_CHUNK, 256, GMLP_CHUNK), BF16),
        jax.ShapeDtypeStruct((B, T // GMLP_CHUNK, 2 * GATE_ROWS, GMLP_CHUNK), F32),
        jax.ShapeDtypeStruct((B, T, GMLP_WIDTH), BF16),
    )
    out_specs = (
        row(NSA_WIDTH),
        pl.BlockSpec((1, tm // CMP_STRIDE, CMP_STRIDE * KV_WIDTH), lambda b, i: (b, i, 0)),
        pl.BlockSpec((1, tm // CMP_STRIDE, CMP_STRIDE * KV_WIDTH), lambda b, i: (b, i, 0)),
        row(2 * KV_WIDTH), row(NSA_WIDTH),
        pl.BlockSpec((1, n_ck, 256, GMLP_CHUNK), lambda b, i: (b, i, 0, 0)),
        pl.BlockSpec((1, n_ck, 2 * GATE_ROWS, GMLP_CHUNK), lambda b, i: (b, i, 0, 0)),
        row(GMLP_WIDTH),
    )
    in_specs = [
        pl.BlockSpec((1, tm, D), lambda b, i: (b, i, 0)),
        const(1, D), const(*wn.shape), const(*wt.shape),
        const(*lng.shape), const(*lnb.shape), const(*ws_t.shape), const(*bs.shape),
    ]
    return pl.pallas_call(
        _proj_kernel, out_shape=out_shape, grid=grid, in_specs=in_specs, out_specs=out_specs,
        scratch_shapes=[pltpu.VMEM((2, tm, KV_WIDTH), F32)],
        name="proj",
        compiler_params=pltpu.CompilerParams(
            dimension_semantics=("arbitrary", "arbitrary"), vmem_limit_bytes=VMEM_LIMIT),
    )(x, g_in, wn, wt, lng, lnb, ws_t, bs)


def _compress_kernel(kc_ref, vc_ref, pek_ref, pev_ref, wk1_ref, wv1_ref, wk2_ref, wv2t_ref,
                     kcmp_ref, vcmpt_ref):
    n_c = kc_ref.shape[1]
    half = 2 * CMP_HIDDEN

    def hidden(src_ref, pe_ref, w1_ref):
        xs = src_ref[0].astype(F32)
        xa = (xs + pe_ref[0:1, :]).astype(BF16)
        xb = (xs + pe_ref[1:2, :]).astype(BF16)
        ha = jnp.dot(xa, w1_ref[:, 0:half], preferred_element_type=F32)
        hb = jnp.dot(xb, w1_ref[:, half:2 * half], preferred_element_type=F32)
        hb_next = pltpu.roll(hb, shift=n_c - 1, axis=0)
        return _gelu_tanh(ha + hb_next).astype(BF16)

    hk = hidden(kc_ref, pek_ref, wk1_ref)
    kcmp_ref[0] = jnp.dot(hk, wk2_ref[...], preferred_element_type=F32).astype(BF16)
    hv = hidden(vc_ref, pev_ref, wv1_ref)
    vcmpt_ref[0] = lax.dot_general(wv2t_ref[...], hv, (((1,), (1,)), ((), ())),
                                   preferred_element_type=F32).astype(BF16)


def _compress_call(kc2, vc2, pek, pev, wk1, wv1, wk2, wv2t):
    B, n_c, W = kc2.shape
    const = lambda a: pl.BlockSpec(a.shape, lambda b: (0,) * a.ndim)
    return pl.pallas_call(
        _compress_kernel,
        out_shape=(jax.ShapeDtypeStruct((B, n_c, KV_WIDTH), BF16),
                   jax.ShapeDtypeStruct((B, KV_WIDTH, n_c), BF16)),
        grid=(B,),
        in_specs=[pl.BlockSpec((1, n_c, W), lambda b: (b, 0, 0)),
                  pl.BlockSpec((1, n_c, W), lambda b: (b, 0, 0)),
                  const(pek), const(pev), const(wk1), const(wv1), const(wk2), const(wv2t)],
        out_specs=(pl.BlockSpec((1, n_c, KV_WIDTH), lambda b: (b, 0, 0)),
                   pl.BlockSpec((1, KV_WIDTH, n_c), lambda b: (b, 0, 0))),
        name="compress",
        compiler_params=pltpu.CompilerParams(
            dimension_semantics=("arbitrary",), vmem_limit_bytes=VMEM_LIMIT),
    )(kc2, vc2, pek, pev, wk1, wv1, wk2, wv2t)


def _attn_kernel(q_ref, kcmp_ref, vcmpt_ref, ks_ref, kw_ref, vst_ref, vwt_ref, gt_ref, zn_ref,
                 o_ref, mask_scr, onehot_scr, qb_scr, m_scr, acc_scr, o_scr, s_scr, p_scr,
                 s1_scr, p1_scr, m1_scr):
    g = pl.program_id(1)
    T = q_ref.shape[1]
    n_c = kcmp_ref.shape[1]
    n_blk = T // SEL_BLOCK
    n_tiles = T // Q_TILE
    assert KV_WIDTH + n_blk <= MXU_DEPTH

    def with_ones(vt):
        return jnp.concatenate([vt, jnp.ones((ONES_ROWS, vt.shape[1]), vt.dtype)], axis=0)

    def normalized(acc):
        return acc[0:HEAD_DIM] / acc[HEAD_DIM:HEAD_DIM + 1]

    def gate_row(c, br):
        gt = gt_ref[0, c]
        return jnp.concatenate(
            [gt[4 * br + h:4 * br + h + 1] for h in range(HEADS_PER_GROUP)], axis=1)

    key_i = lax.broadcasted_iota(jnp.int32, (T, MXU_DEPTH - KV_WIDTH), 0)
    blk_i = lax.broadcasted_iota(jnp.int32, (T, MXU_DEPTH - KV_WIDTH), 1)
    onehot_scr[...] = jnp.where((key_i >> (SEL_BLOCK.bit_length() - 1)) == blk_i, 1.0, 0.0).astype(BF16)
    if KV_WIDTH + n_blk < MXU_DEPTH:
        qb_scr[:, KV_WIDTH + n_blk:, :] = jnp.zeros(
            (n_tiles, MXU_DEPTH - KV_WIDTH - n_blk, LANES), BF16)

    kcmp = kcmp_ref[0]
    vct1 = with_ones(vcmpt_ref[0])
    jb_c = lax.broadcasted_iota(jnp.int32, (n_blk, n_c), 0)
    ic_c = lax.broadcasted_iota(jnp.int32, (n_blk, n_c), 1)
    overlap_t = jnp.where(ic_c * CMP_STRIDE < (jb_c + 1) * SEL_BLOCK,
                          jnp.where(ic_c * CMP_STRIDE + (CMP_BLOCK - 1) >= jb_c * SEL_BLOCK, 1.0, 0.0),
                          0.0).astype(BF16)
    row_group = lax.broadcasted_iota(jnp.int32, (KV_WIDTH, LANES), 0) >> (HEAD_DIM.bit_length() - 1)
    lane_q = lax.broadcasted_iota(jnp.int32, (1, LANES), 1) & (Q_TILE - 1)
    key_row = lax.broadcasted_iota(jnp.int32, (KEY_CHUNK, LANES), 0)
    MASK_CAUSAL, MASK_WINDOW, MASK_NONE, MASK_ALL = range(4)
    mask_scr[MASK_CAUSAL] = jnp.where(key_row <= lane_q, 0.0, NEG_INF)
    mask_scr[MASK_WINDOW] = jnp.where(key_row > lane_q, 0.0, NEG_INF)
    mask_scr[MASK_NONE] = jnp.zeros((KEY_CHUNK, LANES), F32)
    mask_scr[MASK_ALL] = jnp.full((KEY_CHUNK, LANES), NEG_INF, F32)
    cmp_end = lax.broadcasted_iota(jnp.int32, (n_c, LANES), 0) * CMP_STRIDE + (CMP_BLOCK - 1)
    blk_row = lax.broadcasted_iota(jnp.int32, (n_blk, Q_TILE), 0)

    win_chunks = WIN_KEYS // KEY_CHUNK
    r_win, r_own, r_end = n_c, n_c + WIN_KEYS, n_c + WIN_KEYS + KEY_CHUNK

    def win_first_chunk(c):
        return jnp.maximum(c - WINDOW // KEY_CHUNK, 0)

    def tile_scores(c):
        t0 = c * Q_TILE if isinstance(c, int) else pl.multiple_of(c * Q_TILE, Q_TILE)
        qt = q_ref[0, pl.ds(t0, Q_TILE), :].astype(F32).T
        b64 = jnp.concatenate(
            [qt[h * HEAD_DIM:(h + 1) * HEAD_DIM] for h in range(HEADS_PER_GROUP)], axis=1)
        qmat = jnp.where(row_group == g, jnp.concatenate([b64, b64], axis=0), 0.0).astype(BF16)
        qb_scr[c, 0:KV_WIDTH, :] = qmat
        tq = t0 + lane_q
        s = jnp.where(cmp_end <= tq, jnp.dot(kcmp, qmat, preferred_element_type=F32), NEG_INF)
        s1_scr[0:r_win] = s
        m1_scr[0] = jnp.max(s, axis=0, keepdims=True)
        w0 = win_first_chunk(c)
        wk0 = pl.multiple_of(w0 * KEY_CHUNK, KEY_CHUNK)
        sw = jnp.dot(kw_ref[0, pl.ds(wk0, WIN_KEYS), :], qmat, preferred_element_type=F32)
        m_w = None
        for u in range(win_chunks):
            ck = w0 + u
            kind = jnp.where(ck == c, MASK_CAUSAL,
                             jnp.where(ck > c, MASK_ALL,
                                       jnp.where(ck == c - WINDOW // KEY_CHUNK, MASK_WINDOW, MASK_NONE)))
            su = sw[u * KEY_CHUNK:(u + 1) * KEY_CHUNK] + mask_scr[kind]
            s1_scr[r_win + u * KEY_CHUNK:r_win + (u + 1) * KEY_CHUNK] = su
            mu = jnp.max(su, axis=0, keepdims=True)
            m_w = mu if m_w is None else jnp.maximum(m_w, mu)
        m1_scr[1] = m_w
        sd = (jnp.dot(ks_ref[0, pl.ds(t0, KEY_CHUNK), :], qmat, preferred_element_type=F32)
              + mask_scr[MASK_CAUSAL])
        s1_scr[r_own:r_end] = sd
        m_scr[c] = jnp.max(sd, axis=0, keepdims=True)

    def tile_exponentials(c):
        p1_scr[0:r_win] = jnp.exp2(s1_scr[0:r_win] - m1_scr[0]).astype(BF16)
        p1_scr[r_win:r_own] = jnp.exp2(s1_scr[r_win:r_own] - m1_scr[1]).astype(BF16)
        p1_scr[r_own:r_end] = jnp.exp2(s1_scr[r_own:r_end] - m_scr[c]).astype(BF16)

    def tile_outputs(c):
        t0 = c * Q_TILE
        tq = t0 + lane_q
        e16 = p1_scr[0:r_win]
        acc_c = jnp.dot(vct1, e16, preferred_element_type=F32)
        linv_c = jnp.where(tq >= CMP_BLOCK - 1, 1.0 / acc_c[HEAD_DIM:HEAD_DIM + 1], 0.0)
        o_c = acc_c[0:HEAD_DIM] * linv_c
        imp4 = jnp.dot(overlap_t, e16, preferred_element_type=F32) * linv_c
        imp = imp4[:, 0:Q_TILE]
        for h in range(1, HEADS_PER_GROUP):
            imp = imp + imp4[:, h * Q_TILE:(h + 1) * Q_TILE]

        cur = (t0 + lax.broadcasted_iota(jnp.int32, (1, Q_TILE), 1)) >> (SEL_BLOCK.bit_length() - 1)
        valid_b = blk_row <= cur
        bonus = jnp.where(blk_row == 0, FORCE_BONUS,
                          jnp.where(blk_row == cur, FORCE_BONUS,
                                    jnp.where(blk_row == cur - 1, FORCE_BONUS, 0.0)))
        score = jnp.where(valid_b, imp + bonus, -1.0)
        before_tile = blk_row < c * (Q_TILE // SEL_BLOCK)
        sub_row = lax.broadcasted_iota(jnp.int32, (8, Q_TILE), 0)

        def keep_ranked(n_rows):
            n_rb = n_rows // 8
            sblk = [score[8 * r:8 * r + 8] for r in range(n_rb)]
            cnt = [jnp.zeros((8, Q_TILE), F32) for _ in range(n_rb)]
            for j in range(n_rows):
                rj = jnp.broadcast_to(score[j:j + 1], (8, Q_TILE))
                for r in range(n_rb):
                    if 8 * r + 7 <= j:
                        cnt[r] = cnt[r] + jnp.where(rj > sblk[r], 1.0, 0.0)
                    elif 8 * r > j:
                        cnt[r] = cnt[r] + jnp.where(rj >= sblk[r], 1.0, 0.0)
                    else:
                        cnt[r] = cnt[r] + jnp.where(sub_row + 8 * r > j,
                                                    jnp.where(rj >= sblk[r], 1.0, 0.0),
                                                    jnp.where(rj > sblk[r], 1.0, 0.0))
            in_top = jnp.concatenate(
                [jnp.where(cnt[r] < float(SEL_TOP_N), 0.0, NEG_INF) for r in range(n_rb)]
                + [jnp.full((n_blk - n_rows, Q_TILE), NEG_INF, F32)] * (n_rows < n_blk), axis=0)
            return jnp.where(before_tile, in_top, NEG_INF)

        keep = keep_ranked(n_blk)
        qb_scr[c, KV_WIDTH:KV_WIDTH + n_blk, :] = jnp.concatenate(
            [keep] * HEADS_PER_GROUP, axis=1).astype(BF16)

        w0 = win_first_chunk(c)
        vw = jnp.concatenate([vwt_ref[0, w0 + u] for u in range(win_chunks)], axis=1)
        acc_w = jnp.dot(with_ones(vw), p1_scr[r_win:r_own], preferred_element_type=F32)
        acc_scr[c] = jnp.dot(with_ones(vst_ref[0, c]), p1_scr[r_own:r_end],
                             preferred_element_type=F32)
        o_scr[c] = gate_row(c, 0) * o_c + gate_row(c, 2) * normalized(acc_w)

    assert n_tiles >= 2
    tile_scores(0)
    tile_exponentials(0)
    tile_scores(1)

    @pl.loop(0, n_tiles - 2)
    def _(c):
        tile_outputs(c)
        tile_exponentials(c + 1)
        tile_scores(c + 2)

    tile_outputs(n_tiles - 2)
    tile_exponentials(n_tiles - 1)
    tile_outputs(n_tiles - 1)

    chunks_per_step = SEL_KEYS // KEY_CHUNK
    n_sel_steps = T // SEL_KEYS
    n_groups = n_tiles // SEL_GROUP
    assert SEL_KEYS % (SEL_GROUP * Q_TILE) == 0 and n_tiles % SEL_GROUP == 0
    groups_per_step = SEL_KEYS // (SEL_GROUP * Q_TILE)
    n_items = sum(n_groups - groups_per_step * j for j in range(n_sel_steps))
    assert n_items >= 2

    def next_item(item):
        j, q = item
        wrap = q + 1 == n_groups
        return jnp.where(wrap, j + 1, j), jnp.where(wrap, groups_per_step * (j + 1), q + 1)

    def sel_scores(item):
        j, q = item
        k0 = pl.multiple_of(j * SEL_KEYS, SEL_KEYS)
        k_ext = jnp.concatenate([ks_ref[0, pl.ds(k0, SEL_KEYS), :], onehot_scr[pl.ds(k0, SEL_KEYS), :]],
                                axis=1)
        mxs = []
        for i in range(SEL_GROUP):
            sc = jnp.dot(k_ext, qb_scr[q * SEL_GROUP + i], preferred_element_type=F32)
            s_scr[i] = sc
            mxs.append(jnp.max(sc, axis=0, keepdims=True))
        return tuple(mxs)

    def sel_exponentials(mxs):
        for i in range(SEL_GROUP):
            p_scr[i] = jnp.exp2(s_scr[i] - mxs[i]).astype(BF16)

    def sel_accumulate(item, mxs):
        j, q = item
        vs = with_ones(jnp.concatenate(
            [vst_ref[0, chunks_per_step * j + u] for u in range(chunks_per_step)], axis=1))
        for i in range(SEL_GROUP):
            c = q * SEL_GROUP + i
            pv = jnp.dot(vs, p_scr[i], preferred_element_type=F32)
            m = m_scr[c]
            m_new = jnp.maximum(m, mxs[i])
            acc_scr[c] = jnp.exp2(m - m_new) * acc_scr[c] + jnp.exp2(mxs[i] - m_new) * pv
            m_scr[c] = m_new

    item_a = (jnp.int32(0), jnp.int32(0))
    item_b = next_item(item_a)
    mx_a = sel_scores(item_a)
    sel_exponentials(mx_a)
    mx_b = sel_scores(item_b)

    def sel_body(_, carry):
        item_a, item_b, mx_a, mx_b = carry
        item_c = next_item(item_b)
        sel_accumulate(item_a, mx_a)
        sel_exponentials(mx_b)
        return item_b, item_c, mx_b, sel_scores(item_c)

    item_a, item_b, mx_a, mx_b = lax.fori_loop(0, n_items - 2, sel_body, (item_a, item_b, mx_a, mx_b))
    sel_accumulate(item_a, mx_a)
    sel_exponentials(mx_b)
    sel_accumulate(item_b, mx_b)

    @pl.loop(0, n_tiles)
    def _(c):
        t0 = pl.multiple_of(c * Q_TILE, Q_TILE)
        o_t = o_scr[c] + gate_row(c, 1) * normalized(acc_scr[c])
        stacked = jnp.concatenate(
            [o_t[:, h * Q_TILE:(h + 1) * Q_TILE] for h in range(HEADS_PER_GROUP)], axis=0)
        o = stacked.T * zn_ref[0, pl.ds(t0, Q_TILE), :].astype(F32)
        o_ref[0, pl.ds(t0, Q_TILE), :] = o.astype(BF16)


def _attn_call(q, kcmp, vcmpt, ksw, vt, gt, zn):
    B, T, _ = q.shape
    n_c = kcmp.shape[1]
    n_ck = T // KEY_CHUNK
    n_tiles = T // Q_TILE
    G = NSA_KV_HEADS
    gw = HEADS_PER_GROUP * HEAD_DIM
    in_specs = [
        pl.BlockSpec((1, T, gw), lambda b, g: (b, 0, g)),
        pl.BlockSpec((1, n_c, KV_WIDTH), lambda b, g: (b, 0, 0)),
        pl.BlockSpec((1, HEAD_DIM, n_c), lambda b, g: (b, g, 0)),
        pl.BlockSpec((1, T, KV_WIDTH), lambda b, g: (b, 0, 0)),
        pl.BlockSpec((1, T, KV_WIDTH), lambda b, g: (b, 0, 1)),
        pl.BlockSpec((1, n_ck, HEAD_DIM, KEY_CHUNK), lambda b, g: (b, 0, g, 0)),
        pl.BlockSpec((1, n_ck, HEAD_DIM, KEY_CHUNK), lambda b, g: (b, 0, G + g, 0)),
        pl.BlockSpec((1, n_ck, GATE_ROWS, KEY_CHUNK), lambda b, g: (b, 0, g, 0)),
        pl.BlockSpec((1, T, gw), lambda b, g: (b, 0, g)),
    ]
    return pl.pallas_call(
        _attn_kernel,
        out_shape=jax.ShapeDtypeStruct((B, T, NSA_WIDTH), BF16),
        grid=(B, G),
        in_specs=in_specs,
        out_specs=pl.BlockSpec((1, T, gw), lambda b, g: (b, 0, g)),
        scratch_shapes=[pltpu.VMEM((4, KEY_CHUNK, LANES), F32),
                        pltpu.VMEM((T, MXU_DEPTH - KV_WIDTH), BF16),
                        pltpu.VMEM((n_tiles, MXU_DEPTH, LANES), BF16),
                        pltpu.VMEM((n_tiles, 1, LANES), F32),
                        pltpu.VMEM((n_tiles, HEAD_DIM + ONES_ROWS, LANES), F32),
                        pltpu.VMEM((n_tiles, HEAD_DIM, LANES), F32),
                        pltpu.VMEM((SEL_GROUP, SEL_KEYS, LANES), F32),
                        pltpu.VMEM((SEL_GROUP, SEL_KEYS, LANES), BF16),
                        pltpu.VMEM((n_c + WIN_KEYS + KEY_CHUNK, LANES), F32),
                        pltpu.VMEM((n_c + WIN_KEYS + KEY_CHUNK, LANES), BF16),
                        pltpu.VMEM((2, 1, LANES), F32)],
        name="attn",
        compiler_params=pltpu.CompilerParams(
            dimension_semantics=("arbitrary", "arbitrary"), vmem_limit_bytes=ATTN_VMEM_LIMIT),
    )(q, kcmp, vcmpt, ksw, ksw, vt, vt, gt, zn)


def _out_kernel(x_ref, mn_ref, mg_ref, wo_ref, g_ref, o_ref):
    h = x_ref[0]
    h = h + jnp.dot(mn_ref[0], wo_ref[0:NSA_WIDTH, :], preferred_element_type=F32)
    h = h + jnp.dot(mg_ref[0], wo_ref[NSA_WIDTH:, :], preferred_element_type=F32)
    ms = jnp.mean(h * h, axis=-1, keepdims=True)
    o_ref[0] = (h * lax.rsqrt(ms + NORM_EPS)) * g_ref[...]


def _out_call(x, mix_nsa, mix_gmlp, wo, g_f, tm):
    B, T, D = x.shape
    row = lambda w: pl.BlockSpec((1, tm, w), lambda b, i: (b, i, 0))
    return pl.pallas_call(
        _out_kernel,
        out_shape=jax.ShapeDtypeStruct((B, T, D), x.dtype),
        grid=(B, T // tm),
        in_specs=[row(D), row(NSA_WIDTH), row(GMLP_WIDTH),
                  pl.BlockSpec(wo.shape, lambda b, i: (0, 0)),
                  pl.BlockSpec((1, D), lambda b, i: (0, 0))],
        out_specs=row(D),
        name="out",
        compiler_params=pltpu.CompilerParams(
            dimension_semantics=("arbitrary", "arbitrary"), vmem_limit_bytes=VMEM_LIMIT),
    )(x, mix_nsa, mix_gmlp, wo, g_f)


def _split_w_in(w):
    sizes = (NSA_WIDTH, KV_WIDTH, KV_WIDTH, KV_WIDTH, KV_WIDTH, KV_WIDTH, KV_WIDTH, N_GATES,
             NSA_WIDTH, GMLP_WIDTH, GMLP_WIDTH, GMLP_WIDTH)
    parts, off = [], 0
    for s in sizes:
        parts.append(w[:, off:off + s])
        off += s
    return parts


def _gate_columns(w_gate):
    d = w_gate.shape[0]
    wg = w_gate.reshape(d, NSA_KV_HEADS, HEADS_PER_GROUP, 3).transpose(0, 1, 3, 2)
    wg = wg.reshape(d, NSA_KV_HEADS, 3 * HEADS_PER_GROUP)
    wg = jnp.pad(wg, ((0, 0), (0, 0), (0, GATE_ROWS - 3 * HEADS_PER_GROUP)))
    return wg.reshape(d, NSA_KV_HEADS * GATE_ROWS)


def _compress_w1(w1):
    half = CMP_BLOCK // 2
    w = w1.reshape(2, half, HEAD_DIM, CMP_HIDDEN)
    eye = jnp.eye(NSA_KV_HEADS, dtype=w1.dtype)
    full = jnp.einsum('hlde,gk->lgdhke', w, eye)
    return full.reshape(half * KV_WIDTH, 2 * NSA_KV_HEADS * CMP_HIDDEN)


def _compress_pe(pe):
    half = CMP_BLOCK // 2
    p = pe.reshape(2, half, 1, HEAD_DIM)
    p = jnp.broadcast_to(p, (2, half, NSA_KV_HEADS, HEAD_DIM))
    return p.reshape(2, half * KV_WIDTH)


def _block_diag2(w2):
    z = jnp.zeros_like(w2)
    return jnp.concatenate([jnp.concatenate([w2, z], axis=1),
                            jnp.concatenate([z, w2], axis=1)], axis=0)


def kernel(x, norm_in_g, w_in, w_cmp_k1, w_cmp_k2, pe_cmp_k, w_cmp_v1, w_cmp_v2, pe_cmp_v,
           gmlp_ln_g, gmlp_ln_b, w_spatial, b_spatial, w_out, norm_f_g):
    B, T, D = x.shape
    assert w_in.shape[0] == 1, "single-layer block"
    tm = min(PROJ_TILE, T)
    (wq, wkc, wvc, wks, wvs, wkw, wvw, wgt, wzn, wu, wv, wzg) = _split_w_in(w_in[0])
    wn = jnp.concatenate([wq, wkc, wvc, wks, wkw, wzn], axis=1).astype(BF16)
    wt = jnp.concatenate([wvs, wvw, _gate_columns(wgt), wu, wv, wzg], axis=1).T.astype(BF16)
    lng = gmlp_ln_g[0].reshape(GMLP_GROUPS, GMLP_GROUP_DIM, 1)
    lnb = gmlp_ln_b[0].reshape(GMLP_GROUPS, GMLP_GROUP_DIM, 1)
    ws_t = jnp.swapaxes(w_spatial[0], 1, 2)
    bs = b_spatial[0].reshape(GMLP_GROUPS, 1, GMLP_CHUNK)
    q, kc, vc, ksw, zn, vt, gt, o_gmlp = _proj_call(
        x, norm_in_g[0].reshape(1, D), wn, wt, lng, lnb, ws_t, bs, tm)

    n_c = T // CMP_STRIDE
    kcmp, vcmpt = _compress_call(
        kc, vc,
        _compress_pe(pe_cmp_k[0]), _compress_pe(pe_cmp_v[0]),
        _compress_w1(w_cmp_k1[0]).astype(BF16), _compress_w1(w_cmp_v1[0]).astype(BF16),
        _block_diag2(w_cmp_k2[0]).astype(BF16), _block_diag2(w_cmp_v2[0]).T.astype(BF16))

    o_nsa = _attn_call(q, kcmp, vcmpt, ksw, vt, gt, zn)
    return _out_call(x, o_nsa, o_gmlp, w_out[0].astype(BF16), norm_f_g.reshape(1, D), tm)
```

```python
import functools
import math

import jax
import jax.numpy as jnp
from jax import lax
from jax.experimental import pallas as pl
from jax.experimental.pallas import tpu as pltpu

F32 = jnp.float32
BF16 = jnp.bfloat16

HEAD_DIM = 64
NSA_HEADS = 8
NSA_KV_HEADS = 2
HEADS_PER_GROUP = NSA_HEADS // NSA_KV_HEADS
NSA_WIDTH = NSA_HEADS * HEAD_DIM
KV_WIDTH = NSA_KV_HEADS * HEAD_DIM
N_GATES = 3 * NSA_HEADS
GMLP_GROUPS = 8
GMLP_GROUP_DIM = 64
GMLP_WIDTH = GMLP_GROUPS * GMLP_GROUP_DIM
GMLP_CHUNK = 128
CMP_BLOCK = 32
CMP_STRIDE = 16
CMP_HIDDEN = 128
SEL_BLOCK = 64
SEL_TOP_N = 16
WINDOW = 512
NORM_EPS = 1e-6
NEG_INF = -1e30
FORCE_BONUS = 1e4

Q_TILE = 128
KEY_CHUNK = 128
LANES = HEADS_PER_GROUP * Q_TILE
SEL_KEYS = 512
SEL_GROUP = 4
WIN_KEYS = WINDOW + Q_TILE
ONES_ROWS = 16
LOG2_E = math.log2(math.e)
GATE_ROWS = 16
PROJ_PARTS = 1
OUT_TILE = 1024
PROJ_TILE = 512
MXU_DEPTH = 256
VMEM_LIMIT = 48 * 1024 * 1024
ATTN_VMEM_LIMIT = 56 * 1024 * 1024


def _sigmoid(x):
    return 1.0 / (1.0 + jnp.exp(-x))


def _gelu_tanh(x):
    c = math.sqrt(2.0 / math.pi)
    return x * (0.5 * (1.0 + jnp.tanh(c * (x + 0.044715 * (x * x * x)))))


def _proj_kernel(x_ref, g_ref, wn_ref, wt_ref, lng_ref, lnb_ref, ws_ref, bs_ref,
                 q_ref, kc_ref, vc_ref, ksw_ref, zn_ref, vt_ref, gt_ref, og_ref, cmp_scr):
    s_i = lax.broadcasted_iota(jnp.int32, (GMLP_CHUNK, GMLP_CHUNK), 0)
    t_i = lax.broadcasted_iota(jnp.int32, (GMLP_CHUNK, GMLP_CHUNK), 1)
    causal = s_i <= t_i
    w_spatial = [jnp.where(causal, ws_ref[g], 0.0).astype(BF16) for g in range(GMLP_GROUPS)]
    tm = x_ref.shape[1] // PROJ_PARTS
    n_ck = tm // GMLP_CHUNK
    for part in range(PROJ_PARTS):
        rows = slice(part * tm, (part + 1) * tm)
        x = x_ref[0, rows, :]
        ms = jnp.mean(x * x, axis=-1, keepdims=True)
        hn = ((x * lax.rsqrt(ms + NORM_EPS)) * g_ref[...]).astype(BF16)
        pn = jnp.dot(hn, wn_ref[...], preferred_element_type=F32)
        q_ref[0, rows, :] = (pn[:, 0:512] * (HEAD_DIM ** -0.5 * LOG2_E)).astype(BF16)
        cmp_rows = slice(part * (tm // CMP_STRIDE), (part + 1) * (tm // CMP_STRIDE))
        for src, dst_ref, lo in ((0, kc_ref, 512), (1, vc_ref, 640)):
            cmp_scr[part, src] = pn[:, lo:lo + KV_WIDTH]
            for pos in range(CMP_STRIDE):
                dst_ref[0, cmp_rows, pos * KV_WIDTH:(pos + 1) * KV_WIDTH] = cmp_scr[
                    part, src, pl.ds(pos, tm // CMP_STRIDE, stride=CMP_STRIDE), :].astype(BF16)
        ksw_ref[0, rows, :] = pn[:, 768:1024].astype(BF16)
        z = pn[:, 1024:1536]
        zn_ref[0, rows, :] = (z * _sigmoid(z)).astype(BF16)
        pt = lax.dot_general(wt_ref[...], hn, (((1,), (1,)), ((), ())),
                             preferred_element_type=F32)
        vt = pt[0:256].astype(BF16)
        gt = _sigmoid(pt[256:288])
        for ck in range(n_ck):
            sl = slice(ck * GMLP_CHUNK, (ck + 1) * GMLP_CHUNK)
            vt_ref[0, part * n_ck + ck] = vt[:, sl]
            gt_ref[0, part * n_ck + ck] = gt[:, sl]
        gu = _gelu_tanh(pt[288:800])
        gv = _gelu_tanh(pt[800:1312]).reshape(GMLP_GROUPS, GMLP_GROUP_DIM, tm)
        zg = pt[1312:1824]
        mu = jnp.mean(gv, axis=1, keepdims=True)
        dv = gv - mu
        var = jnp.mean(dv * dv, axis=1, keepdims=True)
        vn = (dv * lax.rsqrt(var + NORM_EPS)) * lng_ref[...] + lnb_ref[...]
        outs = []
        for g in range(GMLP_GROUPS):
            a = vn[g].astype(BF16)
            a_st = jnp.concatenate(
                [a[:, ck * GMLP_CHUNK:(ck + 1) * GMLP_CHUNK] for ck in range(n_ck)], axis=0)
            r = jnp.dot(a_st, w_spatial[g], preferred_element_type=F32) + bs_ref[g]
            outs.append(jnp.concatenate(
                [r[ck * GMLP_GROUP_DIM:(ck + 1) * GMLP_GROUP_DIM] for ck in range(n_ck)], axis=1))
        mixed = jnp.concatenate(outs, axis=0)
        og_t = (gu * mixed) * (zg * _sigmoid(zg))
        og_ref[0, rows, :] = og_t.T.astype(BF16)


def _proj_call(x, g_in, wn, wt, lng, lnb, ws_t, bs, tm):
    B, T, D = x.shape
    n_ck = tm // GMLP_CHUNK
    grid = (B, T // tm)
    const = lambda *shape: pl.BlockSpec(shape, lambda b, i: (0,) * len(shape))
    row = lambda w: pl.BlockSpec((1, tm, w), lambda b, i: (b, i, 0))
    out_shape = (
        jax.ShapeDtypeStruct((B, T, NSA_WIDTH), BF16),
        jax.ShapeDtypeStruct((B, T // CMP_STRIDE, CMP_STRIDE * KV_WIDTH), BF16),
        jax.ShapeDtypeStruct((B, T // CMP_STRIDE, CMP_STRIDE * KV_WIDTH), BF16),
        jax.ShapeDtypeStruct((B, T, 2 * KV_WIDTH), BF16),
        jax.ShapeDtypeStruct((B, T, NSA_WIDTH), BF16),
        jax.ShapeDtypeStruct((B, T // GMLP_CHUNK, 256, GMLP_CHUNK), BF16),
        jax.ShapeDtypeStruct((B, T // GMLP_CHUNK, 2 * GATE_ROWS, GMLP_CHUNK), F32),
        jax.ShapeDtypeStruct((B, T, GMLP_WIDTH), BF16),
    )
    out_specs = (
        row(NSA_WIDTH),
        pl.BlockSpec((1, tm // CMP_STRIDE, CMP_STRIDE * KV_WIDTH), lambda b, i: (b, i, 0)),
        pl.BlockSpec((1, tm // CMP_STRIDE, CMP_STRIDE * KV_WIDTH), lambda b, i: (b, i, 0)),
        row(2 * KV_WIDTH), row(NSA_WIDTH),
        pl.BlockSpec((1, n_ck, 256, GMLP_CHUNK), lambda b, i: (b, i, 0, 0)),
        pl.BlockSpec((1, n_ck, 2 * GATE_ROWS, GMLP_CHUNK), lambda b, i: (b, i, 0, 0)),
        row(GMLP_WIDTH),
    )
    in_specs = [
        pl.BlockSpec((1, tm, D), lambda b, i: (b, i, 0)),
        const(1, D), const(*wn.shape), const(*wt.shape),
        const(*lng.shape), const(*lnb.shape), const(*ws_t.shape), const(*bs.shape),
    ]
    return pl.pallas_call(
        _proj_kernel, out_shape=out_shape, grid=grid, in_specs=in_specs, out_specs=out_specs,
        scratch_shapes=[pltpu.VMEM((PROJ_PARTS, 2, tm // PROJ_PARTS, KV_WIDTH), F32)],
        name="proj",
        compiler_params=pltpu.CompilerParams(
            dimension_semantics=("arbitrary", "arbitrary"), vmem_limit_bytes=VMEM_LIMIT),
    )(x, g_in, wn, wt, lng, lnb, ws_t, bs)


def _compress_kernel(kc_ref, vc_ref, pek_ref, pev_ref, wk1_ref, wv1_ref, wk2_ref, wv2t_ref,
                     kcmp_ref, vcmpt_ref):
    n_c = kc_ref.shape[1]
    half = 2 * CMP_HIDDEN

    def hidden(src_ref, pe_ref, w1_ref):
        xs = src_ref[0].astype(F32)
        xa = (xs + pe_ref[0:1, :]).astype(BF16)
        xb = (xs + pe_ref[1:2, :]).astype(BF16)
        ha = jnp.dot(xa, w1_ref[:, 0:half], preferred_element_type=F32)
        hb = jnp.dot(xb, w1_ref[:, half:2 * half], preferred_element_type=F32)
        hb_next = pltpu.roll(hb, shift=n_c - 1, axis=0)
        return _gelu_tanh(ha + hb_next).astype(BF16)

    hk = hidden(kc_ref, pek_ref, wk1_ref)
    kcmp_ref[0] = jnp.dot(hk, wk2_ref[...], preferred_element_type=F32).astype(BF16)
    hv = hidden(vc_ref, pev_ref, wv1_ref)
    vcmpt_ref[0] = lax.dot_general(wv2t_ref[...], hv, (((1,), (1,)), ((), ())),
                                   preferred_element_type=F32).astype(BF16)


def _compress_call(kc2, vc2, pek, pev, wk1, wv1, wk2, wv2t):
    B, n_c, W = kc2.shape
    const = lambda a: pl.BlockSpec(a.shape, lambda b: (0,) * a.ndim)
    return pl.pallas_call(
        _compress_kernel,
        out_shape=(jax.ShapeDtypeStruct((B, n_c, KV_WIDTH), BF16),
                   jax.ShapeDtypeStruct((B, KV_WIDTH, n_c), BF16)),
        grid=(B,),
        in_specs=[pl.BlockSpec((1, n_c, W), lambda b: (b, 0, 0)),
                  pl.BlockSpec((1, n_c, W), lambda b: (b, 0, 0)),
                  const(pek), const(pev), const(wk1), const(wv1), const(wk2), const(wv2t)],
        out_specs=(pl.BlockSpec((1, n_c, KV_WIDTH), lambda b: (b, 0, 0)),
                   pl.BlockSpec((1, KV_WIDTH, n_c), lambda b: (b, 0, 0))),
        name="compress",
        compiler_params=pltpu.CompilerParams(
            dimension_semantics=("arbitrary",), vmem_limit_bytes=VMEM_LIMIT),
    )(kc2, vc2, pek, pev, wk1, wv1, wk2, wv2t)


def _attn_kernel(q_ref, kcmp_ref, vcmpt_ref, ks_ref, kw_ref, vst_ref, vwt_ref, gt_ref, zn_ref,
                 o_ref, mask_scr, onehot_scr, qb_scr, m_scr, acc_scr, o_scr, s_scr, p_scr,
                 s1_scr, p1_scr, m1_scr):
    g = pl.program_id(1)
    T = q_ref.shape[1]
    n_c = kcmp_ref.shape[1]
    n_blk = T // SEL_BLOCK
    n_tiles = T // Q_TILE
    assert KV_WIDTH + n_blk <= MXU_DEPTH

    def with_ones(vt):
        return jnp.concatenate([vt, jnp.ones((ONES_ROWS, vt.shape[1]), vt.dtype)], axis=0)

    def normalized(acc):
        return acc[0:HEAD_DIM] / acc[HEAD_DIM:HEAD_DIM + 1]

    def gate_row(c, br):
        gt = gt_ref[0, c]
        return jnp.concatenate(
            [gt[4 * br + h:4 * br + h + 1] for h in range(HEADS_PER_GROUP)], axis=1)

    key_i = lax.broadcasted_iota(jnp.int32, (T, MXU_DEPTH - KV_WIDTH), 0)
    blk_i = lax.broadcasted_iota(jnp.int32, (T, MXU_DEPTH - KV_WIDTH), 1)
    onehot_scr[...] = jnp.where((key_i >> (SEL_BLOCK.bit_length() - 1)) == blk_i, 1.0, 0.0).astype(BF16)
    if KV_WIDTH + n_blk < MXU_DEPTH:
        qb_scr[:, KV_WIDTH + n_blk:, :] = jnp.zeros(
            (n_tiles, MXU_DEPTH - KV_WIDTH - n_blk, LANES), BF16)

    kcmp = kcmp_ref[0]
    vct1 = with_ones(vcmpt_ref[0])
    jb_c = lax.broadcasted_iota(jnp.int32, (n_blk, n_c), 0)
    ic_c = lax.broadcasted_iota(jnp.int32, (n_blk, n_c), 1)
    overlap_t = jnp.where(ic_c * CMP_STRIDE < (jb_c + 1) * SEL_BLOCK,
                          jnp.where(ic_c * CMP_STRIDE + (CMP_BLOCK - 1) >= jb_c * SEL_BLOCK, 1.0, 0.0),
                          0.0).astype(BF16)
    row_group = lax.broadcasted_iota(jnp.int32, (KV_WIDTH, LANES), 0) >> (HEAD_DIM.bit_length() - 1)
    lane_q = lax.broadcasted_iota(jnp.int32, (1, LANES), 1) & (Q_TILE - 1)
    key_row = lax.broadcasted_iota(jnp.int32, (KEY_CHUNK, LANES), 0)
    MASK_CAUSAL, MASK_WINDOW, MASK_NONE, MASK_ALL = range(4)
    mask_scr[MASK_CAUSAL] = jnp.where(key_row <= lane_q, 0.0, NEG_INF)
    mask_scr[MASK_WINDOW] = jnp.where(key_row > lane_q, 0.0, NEG_INF)
    mask_scr[MASK_NONE] = jnp.zeros((KEY_CHUNK, LANES), F32)
    mask_scr[MASK_ALL] = jnp.full((KEY_CHUNK, LANES), NEG_INF, F32)
    cmp_end = lax.broadcasted_iota(jnp.int32, (n_c, LANES), 0) * CMP_STRIDE + (CMP_BLOCK - 1)
    blk_row = lax.broadcasted_iota(jnp.int32, (n_blk, Q_TILE), 0)

    win_chunks = WIN_KEYS // KEY_CHUNK
    r_win, r_own, r_end = n_c, n_c + WIN_KEYS, n_c + WIN_KEYS + KEY_CHUNK

    def win_first_chunk(c):
        return jnp.maximum(c - WINDOW // KEY_CHUNK, 0)

    def tile_scores(c):
        t0 = c * Q_TILE if isinstance(c, int) else pl.multiple_of(c * Q_TILE, Q_TILE)
        qt = q_ref[0, pl.ds(t0, Q_TILE), :].astype(F32).T
        b64 = jnp.concatenate(
            [qt[h * HEAD_DIM:(h + 1) * HEAD_DIM] for h in range(HEADS_PER_GROUP)], axis=1)
        qmat = jnp.where(row_group == g, jnp.concatenate([b64, b64], axis=0), 0.0).astype(BF16)
        qb_scr[c, 0:KV_WIDTH, :] = qmat
        tq = t0 + lane_q
        s = jnp.where(cmp_end <= tq, jnp.dot(kcmp, qmat, preferred_element_type=F32), NEG_INF)
        s1_scr[0:r_win] = s
        m1_scr[0] = jnp.max(s, axis=0, keepdims=True)
        w0 = win_first_chunk(c)
        wk0 = pl.multiple_of(w0 * KEY_CHUNK, KEY_CHUNK)
        sw = jnp.dot(kw_ref[0, pl.ds(wk0, WIN_KEYS), :], qmat, preferred_element_type=F32)
        m_w = None
        for u in range(win_chunks):
            ck = w0 + u
            kind = jnp.where(ck == c, MASK_CAUSAL,
                             jnp.where(ck > c, MASK_ALL,
                                       jnp.where(ck == c - WINDOW // KEY_CHUNK, MASK_WINDOW, MASK_NONE)))
            su = sw[u * KEY_CHUNK:(u + 1) * KEY_CHUNK] + mask_scr[kind]
            s1_scr[r_win + u * KEY_CHUNK:r_win + (u + 1) * KEY_CHUNK] = su
            mu = jnp.max(su, axis=0, keepdims=True)
            m_w = mu if m_w is None else jnp.maximum(m_w, mu)
        m1_scr[1] = m_w
        sd = (jnp.dot(ks_ref[0, pl.ds(t0, KEY_CHUNK), :], qmat, preferred_element_type=F32)
              + mask_scr[MASK_CAUSAL])
        s1_scr[r_own:r_end] = sd
        m_scr[c] = jnp.max(sd, axis=0, keepdims=True)

    def tile_exponentials(c):
        p1_scr[0:r_win] = jnp.exp2(s1_scr[0:r_win] - m1_scr[0]).astype(BF16)
        p1_scr[r_win:r_own] = jnp.exp2(s1_scr[r_win:r_own] - m1_scr[1]).astype(BF16)
        p1_scr[r_own:r_end] = jnp.exp2(s1_scr[r_own:r_end] - m_scr[c]).astype(BF16)

    def tile_outputs(c, rank_rows):
        t0 = c * Q_TILE
        tq = t0 + lane_q
        e16 = p1_scr[0:r_win]
        acc_c = jnp.dot(vct1, e16, preferred_element_type=F32)
        linv_c = jnp.where(tq >= CMP_BLOCK - 1, 1.0 / acc_c[HEAD_DIM:HEAD_DIM + 1], 0.0)
        o_c = acc_c[0:HEAD_DIM] * linv_c
        imp4 = jnp.dot(overlap_t, e16, preferred_element_type=F32) * linv_c
        imp = imp4[:, 0:Q_TILE]
        for h in range(1, HEADS_PER_GROUP):
            imp = imp + imp4[:, h * Q_TILE:(h + 1) * Q_TILE]

        cur = (t0 + lax.broadcasted_iota(jnp.int32, (1, Q_TILE), 1)) >> (SEL_BLOCK.bit_length() - 1)
        valid_b = blk_row <= cur
        bonus = jnp.where(blk_row == 0, FORCE_BONUS,
                          jnp.where(blk_row == cur, FORCE_BONUS,
                                    jnp.where(blk_row == cur - 1, FORCE_BONUS, 0.0)))
        score = jnp.where(valid_b, imp + bonus, -1.0)
        before_tile = blk_row < c * (Q_TILE // SEL_BLOCK)
        sub_row = lax.broadcasted_iota(jnp.int32, (8, Q_TILE), 0)

        def keep_ranked(n_rows):
            n_rb = n_rows // 8
            sblk = [score[8 * r:8 * r + 8] for r in range(n_rb)]
            cnt = [jnp.zeros((8, Q_TILE), F32) for _ in range(n_rb)]
            for j in range(n_rows):
                rj = jnp.broadcast_to(score[j:j + 1], (8, Q_TILE))
                for r in range(n_rb):
                    if 8 * r + 7 <= j:
                        cnt[r] = cnt[r] + jnp.where(rj > sblk[r], 1.0, 0.0)
                    elif 8 * r > j:
                        cnt[r] = cnt[r] + jnp.where(rj >= sblk[r], 1.0, 0.0)
                    else:
                        cnt[r] = cnt[r] + jnp.where(sub_row + 8 * r > j,
                                                    jnp.where(rj >= sblk[r], 1.0, 0.0),
                                                    jnp.where(rj > sblk[r], 1.0, 0.0))
            in_top = jnp.concatenate(
                [jnp.where(cnt[r] < float(SEL_TOP_N), 0.0, NEG_INF) for r in range(n_rb)]
                + [jnp.full((n_blk - n_rows, Q_TILE), NEG_INF, F32)] * (n_rows < n_blk), axis=0)
            return jnp.where(before_tile, in_top, NEG_INF)

        if rank_rows == 0:
            keep = jnp.where(before_tile, 0.0, NEG_INF)
        else:
            keep = keep_ranked(rank_rows)
        qb_scr[c, KV_WIDTH:KV_WIDTH + n_blk, :] = jnp.concatenate(
            [keep] * HEADS_PER_GROUP, axis=1).astype(BF16)

        w0 = win_first_chunk(c)
        vw = jnp.concatenate([vwt_ref[0, w0 + u] for u in range(win_chunks)], axis=1)
        acc_w = jnp.dot(with_ones(vw), p1_scr[r_win:r_own], preferred_element_type=F32)
        acc_scr[c] = jnp.dot(with_ones(vst_ref[0, c]), p1_scr[r_own:r_end],
                             preferred_element_type=F32)
        o_scr[c] = gate_row(c, 0) * o_c + gate_row(c, 2) * normalized(acc_w)

    assert n_tiles >= 2
    tile_scores(0)
    tile_exponentials(0)
    tile_scores(1)

    blocks_per_tile = Q_TILE // SEL_BLOCK
    sizes = [0] + [r for r in (n_blk // 2,) if r % 8 == 0 and r > SEL_TOP_N] + [n_blk]
    limits = [SEL_TOP_N // blocks_per_tile] + [r // blocks_per_tile for r in sizes[1:]]
    start = 0
    for rank_rows, limit in zip(sizes, limits):
        stop = max(start, min(limit, n_tiles - 2))

        def body(c, rank_rows=rank_rows):
            tile_outputs(c, rank_rows)
            tile_exponentials(c + 1)
            tile_scores(c + 2)

        pl.loop(start, stop)(body)
        start = stop

    tile_outputs(n_tiles - 2, n_blk)
    tile_exponentials(n_tiles - 1)
    tile_outputs(n_tiles - 1, n_blk)

    chunks_per_step = SEL_KEYS // KEY_CHUNK
    n_sel_steps = T // SEL_KEYS
    n_groups = n_tiles // SEL_GROUP
    assert SEL_KEYS % (SEL_GROUP * Q_TILE) == 0 and n_tiles % SEL_GROUP == 0
    groups_per_step = SEL_KEYS // (SEL_GROUP * Q_TILE)
    n_items = sum(n_groups - groups_per_step * j for j in range(n_sel_steps))
    assert n_items >= 2

    def next_item(item):
        j, q = item
        wrap = q + 1 == n_groups
        return jnp.where(wrap, j + 1, j), jnp.where(wrap, groups_per_step * (j + 1), q + 1)

    def sel_scores(item):
        j, q = item
        k0 = pl.multiple_of(j * SEL_KEYS, SEL_KEYS)
        k_ext = jnp.concatenate([ks_ref[0, pl.ds(k0, SEL_KEYS), :], onehot_scr[pl.ds(k0, SEL_KEYS), :]],
                                axis=1)
        mxs = []
        for i in range(SEL_GROUP):
            sc = jnp.dot(k_ext, qb_scr[q * SEL_GROUP + i], preferred_element_type=F32)
            s_scr[i] = sc
            mxs.append(jnp.max(sc, axis=0, keepdims=True))
        return tuple(mxs)

    def sel_exponentials(mxs):
        for i in range(SEL_GROUP):
            p_scr[i] = jnp.exp2(s_scr[i] - mxs[i]).astype(BF16)

    def sel_accumulate(item, mxs):
        j, q = item
        vs = with_ones(jnp.concatenate(
            [vst_ref[0, chunks_per_step * j + u] for u in range(chunks_per_step)], axis=1))
        for i in range(SEL_GROUP):
            c = q * SEL_GROUP + i
            pv = jnp.dot(vs, p_scr[i], preferred_element_type=F32)
            m = m_scr[c]
            m_new = jnp.maximum(m, mxs[i])
            acc_scr[c] = jnp.exp2(m - m_new) * acc_scr[c] + jnp.exp2(mxs[i] - m_new) * pv
            m_scr[c] = m_new

    item_a = (jnp.int32(0), jnp.int32(0))
    item_b = next_item(item_a)
    mx_a = sel_scores(item_a)
    sel_exponentials(mx_a)
    mx_b = sel_scores(item_b)

    def sel_body(_, carry):
        item_a, item_b, mx_a, mx_b = carry
        item_c = next_item(item_b)
        sel_accumulate(item_a, mx_a)
        sel_exponentials(mx_b)
        return item_b, item_c, mx_b, sel_scores(item_c)

    item_a, item_b, mx_a, mx_b = lax.fori_loop(0, n_items - 2, sel_body, (item_a, item_b, mx_a, mx_b))
    sel_accumulate(item_a, mx_a)
    sel_exponentials(mx_b)
    sel_accumulate(item_b, mx_b)

    @pl.loop(0, n_tiles)
    def _(c):
        t0 = pl.multiple_of(c * Q_TILE, Q_TILE)
        o_t = o_scr[c] + gate_row(c, 1) * normalized(acc_scr[c])
        stacked = jnp.concatenate(
            [o_t[:, h * Q_TILE:(h + 1) * Q_TILE] for h in range(HEADS_PER_GROUP)], axis=0)
        o = stacked.T * zn_ref[0, pl.ds(t0, Q_TILE), :].astype(F32)
        o_ref[0, pl.ds(t0, Q_TILE), :] = o.astype(BF16)


def _attn_call(q, kcmp, vcmpt, ksw, vt, gt, zn):
    B, T, _ = q.shape
    n_c = kcmp.shape[1]
    n_ck = T // KEY_CHUNK
    n_tiles = T // Q_TILE
    G = NSA_KV_HEADS
    gw = HEADS_PER_GROUP * HEAD_DIM
    in_specs = [
        pl.BlockSpec((1, T, gw), lambda b, g: (b, 0, g)),
        pl.BlockSpec((1, n_c, KV_WIDTH), lambda b, g: (b, 0, 0)),
        pl.BlockSpec((1, HEAD_DIM, n_c), lambda b, g: (b, g, 0)),
        pl.BlockSpec((1, T, KV_WIDTH), lambda b, g: (b, 0, 0)),
        pl.BlockSpec((1, T, KV_WIDTH), lambda b, g: (b, 0, 1)),
        pl.BlockSpec((1, n_ck, HEAD_DIM, KEY_CHUNK), lambda b, g: (b, 0, g, 0)),
        pl.BlockSpec((1, n_ck, HEAD_DIM, KEY_CHUNK), lambda b, g: (b, 0, G + g, 0)),
        pl.BlockSpec((1, n_ck, GATE_ROWS, KEY_CHUNK), lambda b, g: (b, 0, g, 0)),
        pl.BlockSpec((1, T, gw), lambda b, g: (b, 0, g)),
    ]
    return pl.pallas_call(
        _attn_kernel,
        out_shape=jax.ShapeDtypeStruct((B, T, NSA_WIDTH), BF16),
        grid=(B, G),
        in_specs=in_specs,
        out_specs=pl.BlockSpec((1, T, gw), lambda b, g: (b, 0, g)),
        scratch_shapes=[pltpu.VMEM((4, KEY_CHUNK, LANES), F32),
                        pltpu.VMEM((T, MXU_DEPTH - KV_WIDTH), BF16),
                        pltpu.VMEM((n_tiles, MXU_DEPTH, LANES), BF16),
                        pltpu.VMEM((n_tiles, 1, LANES), F32),
                        pltpu.VMEM((n_tiles, HEAD_DIM + ONES_ROWS, LANES), F32),
                        pltpu.VMEM((n_tiles, HEAD_DIM, LANES), F32),
                        pltpu.VMEM((SEL_GROUP, SEL_KEYS, LANES), F32),
                        pltpu.VMEM((SEL_GROUP, SEL_KEYS, LANES), BF16),
                        pltpu.VMEM((n_c + WIN_KEYS + KEY_CHUNK, LANES), F32),
                        pltpu.VMEM((n_c + WIN_KEYS + KEY_CHUNK, LANES), BF16),
                        pltpu.VMEM((2, 1, LANES), F32)],
        name="attn",
        compiler_params=pltpu.CompilerParams(
            dimension_semantics=("arbitrary", "arbitrary"), vmem_limit_bytes=ATTN_VMEM_LIMIT),
    )(q, kcmp, vcmpt, ksw, ksw, vt, vt, gt, zn)


def _out_kernel(x_ref, mn_ref, mg_ref, wo_ref, g_ref, o_ref):
    h = x_ref[0]
    h = h + jnp.dot(mn_ref[0], wo_ref[0:NSA_WIDTH, :], preferred_element_type=F32)
    h = h + jnp.dot(mg_ref[0], wo_ref[NSA_WIDTH:, :], preferred_element_type=F32)
    ms = jnp.mean(h * h, axis=-1, keepdims=True)
    o_ref[0] = (h * lax.rsqrt(ms + NORM_EPS)) * g_ref[...]


def _out_call(x, mix_nsa, mix_gmlp, wo, g_f, tm):
    B, T, D = x.shape
    row = lambda w: pl.BlockSpec((1, tm, w), lambda b, i: (b, i, 0))
    return pl.pallas_call(
        _out_kernel,
        out_shape=jax.ShapeDtypeStruct((B, T, D), x.dtype),
        grid=(B, T // tm),
        in_specs=[row(D), row(NSA_WIDTH), row(GMLP_WIDTH),
                  pl.BlockSpec(wo.shape, lambda b, i: (0, 0)),
                  pl.BlockSpec((1, D), lambda b, i: (0, 0))],
        out_specs=row(D),
        name="out",
        compiler_params=pltpu.CompilerParams(
            dimension_semantics=("arbitrary", "arbitrary"), vmem_limit_bytes=VMEM_LIMIT),
    )(x, mix_nsa, mix_gmlp, wo, g_f)


def _split_w_in(w):
    sizes = (NSA_WIDTH, KV_WIDTH, KV_WIDTH, KV_WIDTH, KV_WIDTH, KV_WIDTH, KV_WIDTH, N_GATES,
             NSA_WIDTH, GMLP_WIDTH, GMLP_WIDTH, GMLP_WIDTH)
    parts, off = [], 0
    for s in sizes:
        parts.append(w[:, off:off + s])
        off += s
    return parts


def _gate_columns(w_gate):
    d = w_gate.shape[0]
    wg = w_gate.reshape(d, NSA_KV_HEADS, HEADS_PER_GROUP, 3).transpose(0, 1, 3, 2)
    wg = wg.reshape(d, NSA_KV_HEADS, 3 * HEADS_PER_GROUP)
    wg = jnp.pad(wg, ((0, 0), (0, 0), (0, GATE_ROWS - 3 * HEADS_PER_GROUP)))
    return wg.reshape(d, NSA_KV_HEADS * GATE_ROWS)


def _compress_w1(w1):
    half = CMP_BLOCK // 2
    w = w1.reshape(2, half, HEAD_DIM, CMP_HIDDEN)
    eye = jnp.eye(NSA_KV_HEADS, dtype=w1.dtype)
    full = jnp.einsum('hlde,gk->lgdhke', w, eye)
    return full.reshape(half * KV_WIDTH, 2 * NSA_KV_HEADS * CMP_HIDDEN)


def _compress_pe(pe):
    half = CMP_BLOCK // 2
    p = pe.reshape(2, half, 1, HEAD_DIM)
    p = jnp.broadcast_to(p, (2, half, NSA_KV_HEADS, HEAD_DIM))
    return p.reshape(2, half * KV_WIDTH)


def _block_diag2(w2):
    z = jnp.zeros_like(w2)
    return jnp.concatenate([jnp.concatenate([w2, z], axis=1),
                            jnp.concatenate([z, w2], axis=1)], axis=0)


def kernel(x, norm_in_g, w_in, w_cmp_k1, w_cmp_k2, pe_cmp_k, w_cmp_v1, w_cmp_v2, pe_cmp_v,
           gmlp_ln_g, gmlp_ln_b, w_spatial, b_spatial, w_out, norm_f_g):
    B, T, D = x.shape
    assert w_in.shape[0] == 1, "single-layer block"
    tm = min(PROJ_TILE, T)
    (wq, wkc, wvc, wks, wvs, wkw, wvw, wgt, wzn, wu, wv, wzg) = _split_w_in(w_in[0])
    wn = jnp.concatenate([wq, wkc, wvc, wks, wkw, wzn], axis=1).astype(BF16)
    wt = jnp.concatenate([wvs, wvw, _gate_columns(wgt), wu, wv, wzg], axis=1).T.astype(BF16)
    lng = gmlp_ln_g[0].reshape(GMLP_GROUPS, GMLP_GROUP_DIM, 1)
    lnb = gmlp_ln_b[0].reshape(GMLP_GROUPS, GMLP_GROUP_DIM, 1)
    ws_t = jnp.swapaxes(w_spatial[0], 1, 2)
    bs = b_spatial[0].reshape(GMLP_GROUPS, 1, GMLP_CHUNK)
    q, kc, vc, ksw, zn, vt, gt, o_gmlp = _proj_call(
        x, norm_in_g[0].reshape(1, D), wn, wt, lng, lnb, ws_t, bs, tm)

    n_c = T // CMP_STRIDE
    kcmp, vcmpt = _compress_call(
        kc, vc,
        _compress_pe(pe_cmp_k[0]), _compress_pe(pe_cmp_v[0]),
        _compress_w1(w_cmp_k1[0]).astype(BF16), _compress_w1(w_cmp_v1[0]).astype(BF16),
        _block_diag2(w_cmp_k2[0]).astype(BF16), _block_diag2(w_cmp_v2[0]).T.astype(BF16))

    o_nsa = _attn_call(q, kcmp, vcmpt, ksw, vt, gt, zn)
    return _out_call(x, o_nsa, o_gmlp, w_out[0].astype(BF16), norm_f_g.reshape(1, D),
                     min(OUT_TILE, T))
```

```python
import functools
import math

import jax
import jax.numpy as jnp
from jax import lax
from jax.experimental import pallas as pl
from jax.experimental.pallas import tpu as pltpu

F32 = jnp.float32
BF16 = jnp.bfloat16

HEAD_DIM = 64
NSA_HEADS = 8
NSA_KV_HEADS = 2
HEADS_PER_GROUP = NSA_HEADS // NSA_KV_HEADS
NSA_WIDTH = NSA_HEADS * HEAD_DIM
KV_WIDTH = NSA_KV_HEADS * HEAD_DIM
N_GATES = 3 * NSA_HEADS
GMLP_GROUPS = 8
GMLP_GROUP_DIM = 64
GMLP_WIDTH = GMLP_GROUPS * GMLP_GROUP_DIM
GMLP_CHUNK = 128
CMP_BLOCK = 32
CMP_STRIDE = 16
CMP_HIDDEN = 128
SEL_BLOCK = 64
SEL_TOP_N = 16
WINDOW = 512
NORM_EPS = 1e-6
NEG_INF = -1e30
FORCE_BONUS = 1e4

Q_TILE = 128
KEY_CHUNK = 128
LANES = HEADS_PER_GROUP * Q_TILE
SEL_KEYS = 512
SEL_GROUP = 4
WIN_KEYS = WINDOW + Q_TILE
ONES_ROWS = 16
LOG2_E = math.log2(math.e)
GATE_ROWS = 16
PROJ_PARTS = 1
OUT_TILE = 1024
PROJ_TILE = 1024
MXU_DEPTH = 256
VMEM_LIMIT = 48 * 1024 * 1024
ATTN_VMEM_LIMIT = 56 * 1024 * 1024


def _sigmoid(x):
    return 1.0 / (1.0 + jnp.exp(-x))


def _gelu_tanh(x):
    c = math.sqrt(2.0 / math.pi)
    return x * (0.5 * (1.0 + jnp.tanh(c * (x + 0.044715 * (x * x * x)))))


def _proj_kernel(x_ref, g_ref, wn_ref, wt_ref, lng_ref, lnb_ref, ws_ref, bs_ref,
                 q_ref, kc_ref, vc_ref, ksw_ref, zn_ref, vt_ref, gt_ref, og_ref, cmp_scr):
    s_i = lax.broadcasted_iota(jnp.int32, (GMLP_CHUNK, GMLP_CHUNK), 0)
    t_i = lax.broadcasted_iota(jnp.int32, (GMLP_CHUNK, GMLP_CHUNK), 1)
    causal = s_i <= t_i
    w_spatial = [jnp.where(causal, ws_ref[g], 0.0).astype(BF16) for g in range(GMLP_GROUPS)]
    tm = x_ref.shape[1] // PROJ_PARTS
    n_ck = tm // GMLP_CHUNK
    for part in range(PROJ_PARTS):
        rows = slice(part * tm, (part + 1) * tm)
        x = x_ref[0, rows, :]
        ms = jnp.mean(x * x, axis=-1, keepdims=True)
        hn = ((x * lax.rsqrt(ms + NORM_EPS)) * g_ref[...]).astype(BF16)
        pt = lax.dot_general(wt_ref[...], hn, (((1,), (1,)), ((), ())),
                             preferred_element_type=F32)
        vt = pt[0:256].astype(BF16)
        gt = _sigmoid(pt[256:288])
        for ck in range(n_ck):
            sl = slice(ck * GMLP_CHUNK, (ck + 1) * GMLP_CHUNK)
            vt_ref[0, part * n_ck + ck] = vt[:, sl]
            gt_ref[0, part * n_ck + ck] = gt[:, sl]
        gu = _gelu_tanh(pt[288:800])
        gv = _gelu_tanh(pt[800:1312]).reshape(GMLP_GROUPS, GMLP_GROUP_DIM, tm)
        zg = pt[1312:1824]
        mu = jnp.mean(gv, axis=1, keepdims=True)
        dv = gv - mu
        var = jnp.mean(dv * dv, axis=1, keepdims=True)
        vn = (dv * lax.rsqrt(var + NORM_EPS)) * lng_ref[...] + lnb_ref[...]
        outs = []
        for g in range(GMLP_GROUPS):
            a = vn[g].astype(BF16)
            a_st = jnp.concatenate(
                [a[:, ck * GMLP_CHUNK:(ck + 1) * GMLP_CHUNK] for ck in range(n_ck)], axis=0)
            r = jnp.dot(a_st, w_spatial[g], preferred_element_type=F32) + bs_ref[g]
            outs.append(jnp.concatenate(
                [r[ck * GMLP_GROUP_DIM:(ck + 1) * GMLP_GROUP_DIM] for ck in range(n_ck)], axis=1))
        mixed = jnp.concatenate(outs, axis=0)
        og_t = (gu * mixed) * (zg * _sigmoid(zg))
        og_ref[0, rows, :] = og_t.T.astype(BF16)
        pn = jnp.dot(hn, wn_ref[...], preferred_element_type=F32)
        q_ref[0, rows, :] = (pn[:, 0:512] * (HEAD_DIM ** -0.5 * LOG2_E)).astype(BF16)
        cmp_rows = slice(part * (tm // CMP_STRIDE), (part + 1) * (tm // CMP_STRIDE))
        for src, dst_ref, lo in ((0, kc_ref, 512), (1, vc_ref, 640)):
            cmp_scr[part, src] = pn[:, lo:lo + KV_WIDTH]
            for pos in range(CMP_STRIDE):
                dst_ref[0, cmp_rows, pos * KV_WIDTH:(pos + 1) * KV_WIDTH] = cmp_scr[
                    part, src, pl.ds(pos, tm // CMP_STRIDE, stride=CMP_STRIDE), :].astype(BF16)
        ksw_ref[0, rows, :] = pn[:, 768:1024].astype(BF16)
        z = pn[:, 1024:1536]
        zn_ref[0, rows, :] = (z * _sigmoid(z)).astype(BF16)


def _proj_call(x, g_in, wn, wt, lng, lnb, ws_t, bs, tm):
    B, T, D = x.shape
    n_ck = tm // GMLP_CHUNK
    grid = (B, T // tm)
    const = lambda *shape: pl.BlockSpec(shape, lambda b, i: (0,) * len(shape))
    row = lambda w: pl.BlockSpec((1, tm, w), lambda b, i: (b, i, 0))
    out_shape = (
        jax.ShapeDtypeStruct((B, T, NSA_WIDTH), BF16),
        jax.ShapeDtypeStruct((B, T // CMP_STRIDE, CMP_STRIDE * KV_WIDTH), BF16),
        jax.ShapeDtypeStruct((B, T // CMP_STRIDE, CMP_STRIDE * KV_WIDTH), BF16),
        jax.ShapeDtypeStruct((B, T, 2 * KV_WIDTH), BF16),
        jax.ShapeDtypeStruct((B, T, NSA_WIDTH), BF16),
        jax.ShapeDtypeStruct((B, T // GMLP_CHUNK, 256, GMLP_CHUNK), BF16),
        jax.ShapeDtypeStruct((B, T // GMLP_CHUNK, 2 * GATE_ROWS, GMLP_CHUNK), F32),
        jax.ShapeDtypeStruct((B, T, GMLP_WIDTH), BF16),
    )
    out_specs = (
        row(NSA_WIDTH),
        pl.BlockSpec((1, tm // CMP_STRIDE, CMP_STRIDE * KV_WIDTH), lambda b, i: (b, i, 0)),
        pl.BlockSpec((1, tm // CMP_STRIDE, CMP_STRIDE * KV_WIDTH), lambda b, i: (b, i, 0)),
        row(2 * KV_WIDTH), row(NSA_WIDTH),
        pl.BlockSpec((1, n_ck, 256, GMLP_CHUNK), lambda b, i: (b, i, 0, 0)),
        pl.BlockSpec((1, n_ck, 2 * GATE_ROWS, GMLP_CHUNK), lambda b, i: (b, i, 0, 0)),
        row(GMLP_WIDTH),
    )
    in_specs = [
        pl.BlockSpec((1, tm, D), lambda b, i: (b, i, 0)),
        const(1, D), const(*wn.shape), const(*wt.shape),
        const(*lng.shape), const(*lnb.shape), const(*ws_t.shape), const(*bs.shape),
    ]
    return pl.pallas_call(
        _proj_kernel, out_shape=out_shape, grid=grid, in_specs=in_specs, out_specs=out_specs,
        scratch_shapes=[pltpu.VMEM((PROJ_PARTS, 2, tm // PROJ_PARTS, KV_WIDTH), F32)],
        name="proj",
        compiler_params=pltpu.CompilerParams(
            dimension_semantics=("arbitrary", "arbitrary"), vmem_limit_bytes=VMEM_LIMIT),
    )(x, g_in, wn, wt, lng, lnb, ws_t, bs)


def _compress_kernel(kc_ref, vc_ref, pek_ref, pev_ref, wk1_ref, wv1_ref, wk2_ref, wv2t_ref,
                     kcmp_ref, vcmpt_ref):
    n_c = kc_ref.shape[1]
    half = 2 * CMP_HIDDEN

    def hidden(src_ref, pe_ref, w1_ref):
        xs = src_ref[0].astype(F32)
        xa = (xs + pe_ref[0:1, :]).astype(BF16)
        xb = (xs + pe_ref[1:2, :]).astype(BF16)
        ha = jnp.dot(xa, w1_ref[:, 0:half], preferred_element_type=F32)
        hb = jnp.dot(xb, w1_ref[:, half:2 * half], preferred_element_type=F32)
        hb_next = pltpu.roll(hb, shift=n_c - 1, axis=0)
        return _gelu_tanh(ha + hb_next).astype(BF16)

    hk = hidden(kc_ref, pek_ref, wk1_ref)
    kcmp_ref[0] = jnp.dot(hk, wk2_ref[...], preferred_element_type=F32).astype(BF16)
    hv = hidden(vc_ref, pev_ref, wv1_ref)
    vcmpt_ref[0] = lax.dot_general(wv2t_ref[...], hv, (((1,), (1,)), ((), ())),
                                   preferred_element_type=F32).astype(BF16)


def _compress_call(kc2, vc2, pek, pev, wk1, wv1, wk2, wv2t):
    B, n_c, W = kc2.shape
    const = lambda a: pl.BlockSpec(a.shape, lambda b: (0,) * a.ndim)
    return pl.pallas_call(
        _compress_kernel,
        out_shape=(jax.ShapeDtypeStruct((B, n_c, KV_WIDTH), BF16),
                   jax.ShapeDtypeStruct((B, KV_WIDTH, n_c), BF16)),
        grid=(B,),
        in_specs=[pl.BlockSpec((1, n_c, W), lambda b: (b, 0, 0)),
                  pl.BlockSpec((1, n_c, W), lambda b: (b, 0, 0)),
                  const(pek), const(pev), const(wk1), const(wv1), const(wk2), const(wv2t)],
        out_specs=(pl.BlockSpec((1, n_c, KV_WIDTH), lambda b: (b, 0, 0)),
                   pl.BlockSpec((1, KV_WIDTH, n_c), lambda b: (b, 0, 0))),
        name="compress",
        compiler_params=pltpu.CompilerParams(
            dimension_semantics=("arbitrary",), vmem_limit_bytes=VMEM_LIMIT),
    )(kc2, vc2, pek, pev, wk1, wv1, wk2, wv2t)


def _attn_kernel(q_ref, kcmp_ref, vcmpt_ref, ks_ref, kw_ref, vst_ref, vwt_ref, gt_ref, zn_ref,
                 o_ref, mask_scr, onehot_scr, qb_scr, m_scr, acc_scr, o_scr, s_scr, p_scr,
                 s1_scr, p1_scr, m1_scr):
    g = pl.program_id(1)
    T = q_ref.shape[1]
    n_c = kcmp_ref.shape[1]
    n_blk = T // SEL_BLOCK
    n_tiles = T // Q_TILE
    assert KV_WIDTH + n_blk <= MXU_DEPTH

    def with_ones(vt):
        return jnp.concatenate([vt, jnp.ones((ONES_ROWS, vt.shape[1]), vt.dtype)], axis=0)

    def normalized(acc):
        return acc[0:HEAD_DIM] / acc[HEAD_DIM:HEAD_DIM + 1]

    def gate_row(c, br):
        gt = gt_ref[0, c]
        return jnp.concatenate(
            [gt[4 * br + h:4 * br + h + 1] for h in range(HEADS_PER_GROUP)], axis=1)

    lane_q = lax.broadcasted_iota(jnp.int32, (1, LANES), 1) & (Q_TILE - 1)
    MASK_CAUSAL, MASK_WINDOW, MASK_NONE, MASK_ALL = range(4)

    @pl.when((pl.program_id(0) == 0) & (g == 0))
    def _():
        key_i = lax.broadcasted_iota(jnp.int32, (T, MXU_DEPTH - KV_WIDTH), 0)
        blk_i = lax.broadcasted_iota(jnp.int32, (T, MXU_DEPTH - KV_WIDTH), 1)
        onehot_scr[...] = jnp.where((key_i >> (SEL_BLOCK.bit_length() - 1)) == blk_i,
                                    1.0, 0.0).astype(BF16)
        if KV_WIDTH + n_blk < MXU_DEPTH:
            qb_scr[:, KV_WIDTH + n_blk:, :] = jnp.zeros(
                (n_tiles, MXU_DEPTH - KV_WIDTH - n_blk, LANES), BF16)
        key_row = lax.broadcasted_iota(jnp.int32, (KEY_CHUNK, LANES), 0)
        mask_scr[MASK_CAUSAL] = jnp.where(key_row <= lane_q, 0.0, NEG_INF)
        mask_scr[MASK_WINDOW] = jnp.where(key_row > lane_q, 0.0, NEG_INF)
        mask_scr[MASK_NONE] = jnp.zeros((KEY_CHUNK, LANES), F32)
        mask_scr[MASK_ALL] = jnp.full((KEY_CHUNK, LANES), NEG_INF, F32)

    kcmp = kcmp_ref[0]
    vct1 = with_ones(vcmpt_ref[0])
    jb_c = lax.broadcasted_iota(jnp.int32, (n_blk, n_c), 0)
    ic_c = lax.broadcasted_iota(jnp.int32, (n_blk, n_c), 1)
    overlap_t = jnp.where(ic_c * CMP_STRIDE < (jb_c + 1) * SEL_BLOCK,
                          jnp.where(ic_c * CMP_STRIDE + (CMP_BLOCK - 1) >= jb_c * SEL_BLOCK, 1.0, 0.0),
                          0.0).astype(BF16)
    row_group = lax.broadcasted_iota(jnp.int32, (KV_WIDTH, LANES), 0) >> (HEAD_DIM.bit_length() - 1)
    cmp_end = lax.broadcasted_iota(jnp.int32, (n_c, LANES), 0) * CMP_STRIDE + (CMP_BLOCK - 1)
    blk_row = lax.broadcasted_iota(jnp.int32, (n_blk, Q_TILE), 0)

    win_chunks = WIN_KEYS // KEY_CHUNK
    r_win, r_own, r_end = n_c, n_c + WIN_KEYS, n_c + WIN_KEYS + KEY_CHUNK

    def win_first_chunk(c):
        return jnp.maximum(c - WINDOW // KEY_CHUNK, 0)

    def tile_scores(c):
        t0 = c * Q_TILE if isinstance(c, int) else pl.multiple_of(c * Q_TILE, Q_TILE)
        qt = q_ref[0, pl.ds(t0, Q_TILE), :].astype(F32).T
        b64 = jnp.concatenate(
            [qt[h * HEAD_DIM:(h + 1) * HEAD_DIM] for h in range(HEADS_PER_GROUP)], axis=1)
        qmat = jnp.where(row_group == g, jnp.concatenate([b64, b64], axis=0), 0.0).astype(BF16)
        qb_scr[c, 0:KV_WIDTH, :] = qmat
        tq = t0 + lane_q
        s = jnp.where(cmp_end <= tq, jnp.dot(kcmp, qmat, preferred_element_type=F32), NEG_INF)
        s1_scr[0:r_win] = s
        m1_scr[0] = jnp.max(s, axis=0, keepdims=True)
        w0 = win_first_chunk(c)
        wk0 = pl.multiple_of(w0 * KEY_CHUNK, KEY_CHUNK)
        sw = jnp.dot(kw_ref[0, pl.ds(wk0, WIN_KEYS), :], qmat, preferred_element_type=F32)
        m_w = None
        for u in range(win_chunks):
            ck = w0 + u
            kind = jnp.where(ck == c, MASK_CAUSAL,
                             jnp.where(ck > c, MASK_ALL,
                                       jnp.where(ck == c - WINDOW // KEY_CHUNK, MASK_WINDOW, MASK_NONE)))
            su = sw[u * KEY_CHUNK:(u + 1) * KEY_CHUNK] + mask_scr[kind]
            s1_scr[r_win + u * KEY_CHUNK:r_win + (u + 1) * KEY_CHUNK] = su
            mu = jnp.max(su, axis=0, keepdims=True)
            m_w = mu if m_w is None else jnp.maximum(m_w, mu)
        m1_scr[1] = m_w
        sd = (jnp.dot(ks_ref[0, pl.ds(t0, KEY_CHUNK), :], qmat, preferred_element_type=F32)
              + mask_scr[MASK_CAUSAL])
        s1_scr[r_own:r_end] = sd
        m_scr[c] = jnp.max(sd, axis=0, keepdims=True)

    def tile_exponentials(c):
        p1_scr[0:r_win] = jnp.exp2(s1_scr[0:r_win] - m1_scr[0]).astype(BF16)
        p1_scr[r_win:r_own] = jnp.exp2(s1_scr[r_win:r_own] - m1_scr[1]).astype(BF16)
        p1_scr[r_own:r_end] = jnp.exp2(s1_scr[r_own:r_end] - m_scr[c]).astype(BF16)

    def tile_outputs(c, rank_rows):
        t0 = c * Q_TILE
        tq = t0 + lane_q
        e16 = p1_scr[0:r_win]
        acc_c = jnp.dot(vct1, e16, preferred_element_type=F32)
        linv_c = jnp.where(tq >= CMP_BLOCK - 1, 1.0 / acc_c[HEAD_DIM:HEAD_DIM + 1], 0.0)
        o_c = acc_c[0:HEAD_DIM] * linv_c
        imp4 = jnp.dot(overlap_t, e16, preferred_element_type=F32) * linv_c
        imp = imp4[:, 0:Q_TILE]
        for h in range(1, HEADS_PER_GROUP):
            imp = imp + imp4[:, h * Q_TILE:(h + 1) * Q_TILE]

        cur = (t0 + lax.broadcasted_iota(jnp.int32, (1, Q_TILE), 1)) >> (SEL_BLOCK.bit_length() - 1)
        valid_b = blk_row <= cur
        bonus = jnp.where(blk_row == 0, FORCE_BONUS,
                          jnp.where(blk_row == cur, FORCE_BONUS,
                                    jnp.where(blk_row == cur - 1, FORCE_BONUS, 0.0)))
        score = jnp.where(valid_b, imp + bonus, -1.0)
        before_tile = blk_row < c * (Q_TILE // SEL_BLOCK)
        sub_row = lax.broadcasted_iota(jnp.int32, (8, Q_TILE), 0)

        def keep_ranked(n_rows):
            n_rb = n_rows // 8
            sblk = [score[8 * r:8 * r + 8] for r in range(n_rb)]
            cnt = [jnp.zeros((8, Q_TILE), F32) for _ in range(n_rb)]
            for j in range(n_rows):
                rj = jnp.broadcast_to(score[j:j + 1], (8, Q_TILE))
                for r in range(n_rb):
                    if 8 * r + 7 <= j:
                        cnt[r] = cnt[r] + jnp.where(rj > sblk[r], 1.0, 0.0)
                    elif 8 * r > j:
                        cnt[r] = cnt[r] + jnp.where(rj >= sblk[r], 1.0, 0.0)
                    else:
                        cnt[r] = cnt[r] + jnp.where(sub_row + 8 * r > j,
                                                    jnp.where(rj >= sblk[r], 1.0, 0.0),
                                                    jnp.where(rj > sblk[r], 1.0, 0.0))
            in_top = jnp.concatenate(
                [jnp.where(cnt[r] < float(SEL_TOP_N), 0.0, NEG_INF) for r in range(n_rb)]
                + [jnp.full((n_blk - n_rows, Q_TILE), NEG_INF, F32)] * (n_rows < n_blk), axis=0)
            return jnp.where(before_tile, in_top, NEG_INF)

        if rank_rows == 0:
            keep = jnp.where(before_tile, 0.0, NEG_INF)
        else:
            keep = keep_ranked(rank_rows)
        qb_scr[c, KV_WIDTH:KV_WIDTH + n_blk, :] = jnp.concatenate(
            [keep] * HEADS_PER_GROUP, axis=1).astype(BF16)

        w0 = win_first_chunk(c)
        vw = jnp.concatenate([vwt_ref[0, w0 + u] for u in range(win_chunks)], axis=1)
        acc_w = jnp.dot(with_ones(vw), p1_scr[r_win:r_own], preferred_element_type=F32)
        acc_scr[c] = jnp.dot(with_ones(vst_ref[0, c]), p1_scr[r_own:r_end],
                             preferred_element_type=F32)
        o_scr[c] = gate_row(c, 0) * o_c + gate_row(c, 2) * normalized(acc_w)

    assert n_tiles >= 2
    tile_scores(0)
    tile_exponentials(0)
    tile_scores(1)

    blocks_per_tile = Q_TILE // SEL_BLOCK
    sizes = [0] + [r for r in (n_blk // 2,) if r % 8 == 0 and r > SEL_TOP_N] + [n_blk]
    limits = [SEL_TOP_N // blocks_per_tile] + [r // blocks_per_tile for r in sizes[1:]]
    start = 0
    for rank_rows, limit in zip(sizes, limits):
        stop = max(start, min(limit, n_tiles - 2))

        def body(c, rank_rows=rank_rows):
            tile_outputs(c, rank_rows)
            tile_exponentials(c + 1)
            tile_scores(c + 2)

        pl.loop(start, stop)(body)
        start = stop

    tile_outputs(n_tiles - 2, n_blk)
    tile_exponentials(n_tiles - 1)
    tile_outputs(n_tiles - 1, n_blk)

    chunks_per_step = SEL_KEYS // KEY_CHUNK
    n_sel_steps = T // SEL_KEYS
    n_groups = n_tiles // SEL_GROUP
    assert SEL_KEYS % (SEL_GROUP * Q_TILE) == 0 and n_tiles % SEL_GROUP == 0
    groups_per_step = SEL_KEYS // (SEL_GROUP * Q_TILE)
    n_items = sum(n_groups - groups_per_step * j for j in range(n_sel_steps))
    assert n_items >= 2

    def next_item(item):
        j, q = item
        wrap = q + 1 == n_groups
        return jnp.where(wrap, j + 1, j), jnp.where(wrap, groups_per_step * (j + 1), q + 1)

    def sel_scores(item):
        j, q = item
        k0 = pl.multiple_of(j * SEL_KEYS, SEL_KEYS)
        k_ext = jnp.concatenate([ks_ref[0, pl.ds(k0, SEL_KEYS), :], onehot_scr[pl.ds(k0, SEL_KEYS), :]],
                                axis=1)
        mxs = []
        for i in range(SEL_GROUP):
            sc = jnp.dot(k_ext, qb_scr[q * SEL_GROUP + i], preferred_element_type=F32)
            s_scr[i] = sc
            mxs.append(jnp.max(sc, axis=0, keepdims=True))
        return tuple(mxs)

    def sel_exponentials(mxs):
        for i in range(SEL_GROUP):
            p_scr[i] = jnp.exp2(s_scr[i] - mxs[i]).astype(BF16)

    def sel_accumulate(item, mxs):
        j, q = item
        vs = with_ones(jnp.concatenate(
            [vst_ref[0, chunks_per_step * j + u] for u in range(chunks_per_step)], axis=1))
        for i in range(SEL_GROUP):
            c = q * SEL_GROUP + i
            pv = jnp.dot(vs, p_scr[i], preferred_element_type=F32)
            m = m_scr[c]
            m_new = jnp.maximum(m, mxs[i])
            acc_scr[c] = jnp.exp2(m - m_new) * acc_scr[c] + jnp.exp2(mxs[i] - m_new) * pv
            m_scr[c] = m_new

    item_a = (jnp.int32(0), jnp.int32(0))
    item_b = next_item(item_a)
    mx_a = sel_scores(item_a)
    sel_exponentials(mx_a)
    mx_b = sel_scores(item_b)

    def sel_body(_, carry):
        item_a, item_b, mx_a, mx_b = carry
        item_c = next_item(item_b)
        sel_accumulate(item_a, mx_a)
        sel_exponentials(mx_b)
        return item_b, item_c, mx_b, sel_scores(item_c)

    item_a, item_b, mx_a, mx_b = lax.fori_loop(0, n_items - 2, sel_body, (item_a, item_b, mx_a, mx_b))
    sel_accumulate(item_a, mx_a)
    sel_exponentials(mx_b)
    sel_accumulate(item_b, mx_b)

    @pl.loop(0, n_tiles)
    def _(c):
        t0 = pl.multiple_of(c * Q_TILE, Q_TILE)
        o_t = o_scr[c] + gate_row(c, 1) * normalized(acc_scr[c])
        stacked = jnp.concatenate(
            [o_t[:, h * Q_TILE:(h + 1) * Q_TILE] for h in range(HEADS_PER_GROUP)], axis=0)
        o = stacked.T * zn_ref[0, pl.ds(t0, Q_TILE), :].astype(F32)
        o_ref[0, pl.ds(t0, Q_TILE), :] = o.astype(BF16)


def _attn_call(q, kcmp, vcmpt, ksw, vt, gt, zn):
    B, T, _ = q.shape
    n_c = kcmp.shape[1]
    n_ck = T // KEY_CHUNK
    n_tiles = T // Q_TILE
    G = NSA_KV_HEADS
    gw = HEADS_PER_GROUP * HEAD_DIM
    in_specs = [
        pl.BlockSpec((1, T, gw), lambda b, g: (b, 0, g)),
        pl.BlockSpec((1, n_c, KV_WIDTH), lambda b, g: (b, 0, 0)),
        pl.BlockSpec((1, HEAD_DIM, n_c), lambda b, g: (b, g, 0)),
        pl.BlockSpec((1, T, KV_WIDTH), lambda b, g: (b, 0, 0)),
        pl.BlockSpec((1, T, KV_WIDTH), lambda b, g: (b, 0, 1)),
        pl.BlockSpec((1, n_ck, HEAD_DIM, KEY_CHUNK), lambda b, g: (b, 0, g, 0)),
        pl.BlockSpec((1, n_ck, HEAD_DIM, KEY_CHUNK), lambda b, g: (b, 0, G + g, 0)),
        pl.BlockSpec((1, n_ck, GATE_ROWS, KEY_CHUNK), lambda b, g: (b, 0, g, 0)),
        pl.BlockSpec((1, T, gw), lambda b, g: (b, 0, g)),
    ]
    return pl.pallas_call(
        _attn_kernel,
        out_shape=jax.ShapeDtypeStruct((B, T, NSA_WIDTH), BF16),
        grid=(B, G),
        in_specs=in_specs,
        out_specs=pl.BlockSpec((1, T, gw), lambda b, g: (b, 0, g)),
        scratch_shapes=[pltpu.VMEM((4, KEY_CHUNK, LANES), F32),
                        pltpu.VMEM((T, MXU_DEPTH - KV_WIDTH), BF16),
                        pltpu.VMEM((n_tiles, MXU_DEPTH, LANES), BF16),
                        pltpu.VMEM((n_tiles, 1, LANES), F32),
                        pltpu.VMEM((n_tiles, HEAD_DIM + ONES_ROWS, LANES), F32),
                        pltpu.VMEM((n_tiles, HEAD_DIM, LANES), F32),
                        pltpu.VMEM((SEL_GROUP, SEL_KEYS, LANES), F32),
                        pltpu.VMEM((SEL_GROUP, SEL_KEYS, LANES), BF16),
                        pltpu.VMEM((n_c + WIN_KEYS + KEY_CHUNK, LANES), F32),
                        pltpu.VMEM((n_c + WIN_KEYS + KEY_CHUNK, LANES), BF16),
                        pltpu.VMEM((2, 1, LANES), F32)],
        name="attn",
        compiler_params=pltpu.CompilerParams(
            dimension_semantics=("arbitrary", "arbitrary"), vmem_limit_bytes=ATTN_VMEM_LIMIT),
    )(q, kcmp, vcmpt, ksw, ksw, vt, vt, gt, zn)


def _out_kernel(x_ref, mn_ref, mg_ref, wo_ref, g_ref, o_ref):
    h = x_ref[0]
    h = h + jnp.dot(mn_ref[0], wo_ref[0:NSA_WIDTH, :], preferred_element_type=F32)
    h = h + jnp.dot(mg_ref[0], wo_ref[NSA_WIDTH:, :], preferred_element_type=F32)
    ms = jnp.mean(h * h, axis=-1, keepdims=True)
    o_ref[0] = (h * lax.rsqrt(ms + NORM_EPS)) * g_ref[...]


def _out_call(x, mix_nsa, mix_gmlp, wo, g_f, tm):
    B, T, D = x.shape
    row = lambda w: pl.BlockSpec((1, tm, w), lambda b, i: (b, i, 0))
    return pl.pallas_call(
        _out_kernel,
        out_shape=jax.ShapeDtypeStruct((B, T, D), x.dtype),
        grid=(B, T // tm),
        in_specs=[row(D), row(NSA_WIDTH), row(GMLP_WIDTH),
                  pl.BlockSpec(wo.shape, lambda b, i: (0, 0)),
                  pl.BlockSpec((1, D), lambda b, i: (0, 0))],
        out_specs=row(D),
        name="out",
        compiler_params=pltpu.CompilerParams(
            dimension_semantics=("arbitrary", "arbitrary"), vmem_limit_bytes=VMEM_LIMIT),
    )(x, mix_nsa, mix_gmlp, wo, g_f)


def _split_w_in(w):
    sizes = (NSA_WIDTH, KV_WIDTH, KV_WIDTH, KV_WIDTH, KV_WIDTH, KV_WIDTH, KV_WIDTH, N_GATES,
             NSA_WIDTH, GMLP_WIDTH, GMLP_WIDTH, GMLP_WIDTH)
    parts, off = [], 0
    for s in sizes:
        parts.append(w[:, off:off + s])
        off += s
    return parts


def _gate_columns(w_gate):
    d = w_gate.shape[0]
    wg = w_gate.reshape(d, NSA_KV_HEADS, HEADS_PER_GROUP, 3).transpose(0, 1, 3, 2)
    wg = wg.reshape(d, NSA_KV_HEADS, 3 * HEADS_PER_GROUP)
    wg = jnp.pad(wg, ((0, 0), (0, 0), (0, GATE_ROWS - 3 * HEADS_PER_GROUP)))
    return wg.reshape(d, NSA_KV_HEADS * GATE_ROWS)


def _compress_w1(w1):
    half = CMP_BLOCK // 2
    w = w1.reshape(2, half, HEAD_DIM, CMP_HIDDEN)
    eye = jnp.eye(NSA_KV_HEADS, dtype=w1.dtype)
    full = jnp.einsum('hlde,gk->lgdhke', w, eye)
    return full.reshape(half * KV_WIDTH, 2 * NSA_KV_HEADS * CMP_HIDDEN)


def _compress_pe(pe):
    half = CMP_BLOCK // 2
    p = pe.reshape(2, half, 1, HEAD_DIM)
    p = jnp.broadcast_to(p, (2, half, NSA_KV_HEADS, HEAD_DIM))
    return p.reshape(2, half * KV_WIDTH)


def _block_diag2(w2):
    z = jnp.zeros_like(w2)
    return jnp.concatenate([jnp.concatenate([w2, z], axis=1),
                            jnp.concatenate([z, w2], axis=1)], axis=0)


def kernel(x, norm_in_g, w_in, w_cmp_k1, w_cmp_k2, pe_cmp_k, w_cmp_v1, w_cmp_v2, pe_cmp_v,
           gmlp_ln_g, gmlp_ln_b, w_spatial, b_spatial, w_out, norm_f_g):
    B, T, D = x.shape
    assert w_in.shape[0] == 1, "single-layer block"
    tm = min(PROJ_TILE, T)
    (wq, wkc, wvc, wks, wvs, wkw, wvw, wgt, wzn, wu, wv, wzg) = _split_w_in(w_in[0])
    wn = jnp.concatenate([wq, wkc, wvc, wks, wkw, wzn], axis=1).astype(BF16)
    wt = jnp.concatenate([wvs, wvw, _gate_columns(wgt), wu, wv, wzg], axis=1).T.astype(BF16)
    lng = gmlp_ln_g[0].reshape(GMLP_GROUPS, GMLP_GROUP_DIM, 1)
    lnb = gmlp_ln_b[0].reshape(GMLP_GROUPS, GMLP_GROUP_DIM, 1)
    ws_t = jnp.swapaxes(w_spatial[0], 1, 2)
    bs = b_spatial[0].reshape(GMLP_GROUPS, 1, GMLP_CHUNK)
    q, kc, vc, ksw, zn, vt, gt, o_gmlp = _proj_call(
        x, norm_in_g[0].reshape(1, D), wn, wt, lng, lnb, ws_t, bs, tm)

    n_c = T // CMP_STRIDE
    kcmp, vcmpt = _compress_call(
        kc, vc,
        _compress_pe(pe_cmp_k[0]), _compress_pe(pe_cmp_v[0]),
        _compress_w1(w_cmp_k1[0]).astype(BF16), _compress_w1(w_cmp_v1[0]).astype(BF16),
        _block_diag2(w_cmp_k2[0]).astype(BF16), _block_diag2(w_cmp_v2[0]).T.astype(BF16))

    o_nsa = _attn_call(q, kcmp, vcmpt, ksw, vt, gt, zn)
    return _out_call(x, o_nsa, o_gmlp, w_out[0].astype(BF16), norm_f_g.reshape(1, D),
                     min(OUT_TILE, T))
```

```python
import functools
import math

import jax
import jax.numpy as jnp
from jax import lax
from jax.experimental import pallas as pl
from jax.experimental.pallas import tpu as pltpu

F32 = jnp.float32
BF16 = jnp.bfloat16

HEAD_DIM = 64
NSA_HEADS = 8
NSA_KV_HEADS = 2
HEADS_PER_GROUP = NSA_HEADS // NSA_KV_HEADS
NSA_WIDTH = NSA_HEADS * HEAD_DIM
KV_WIDTH = NSA_KV_HEADS * HEAD_DIM
N_GATES = 3 * NSA_HEADS
GMLP_GROUPS = 8
GMLP_GROUP_DIM = 64
GMLP_WIDTH = GMLP_GROUPS * GMLP_GROUP_DIM
GMLP_CHUNK = 128
CMP_BLOCK = 32
CMP_STRIDE = 16
CMP_HIDDEN = 128
SEL_BLOCK = 64
SEL_TOP_N = 16
WINDOW = 512
NORM_EPS = 1e-6
NEG_INF = -1e30
FORCE_BONUS = 1e4

Q_TILE = 128
KEY_CHUNK = 128
LANES = HEADS_PER_GROUP * Q_TILE
SEL_KEYS = 512
SEL_GROUP = 4
SEL_UNROLL = 1
OUT_UNROLL = 4
WIN_KEYS = WINDOW + Q_TILE
ONES_ROWS = 16
LOG2_E = math.log2(math.e)
GATE_ROWS = 16
PROJ_PARTS = 1
OUT_TILE = 1024
PROJ_TILE = 1024
MXU_DEPTH = 256
VMEM_LIMIT = 48 * 1024 * 1024
ATTN_VMEM_LIMIT = 56 * 1024 * 1024


def _sigmoid(x):
    return 1.0 / (1.0 + jnp.exp(-x))


def _gelu_tanh(x):
    c = math.sqrt(2.0 / math.pi)
    return x * (0.5 * (1.0 + jnp.tanh(c * (x + 0.044715 * (x * x * x)))))


def _proj_kernel(x_ref, g_ref, wn_ref, wt_ref, lng_ref, lnb_ref, ws_ref, bs_ref,
                 q_ref, kc_ref, vc_ref, ksw_ref, zn_ref, vt_ref, gt_ref, og_ref, cmp_scr):
    s_i = lax.broadcasted_iota(jnp.int32, (GMLP_CHUNK, GMLP_CHUNK), 0)
    t_i = lax.broadcasted_iota(jnp.int32, (GMLP_CHUNK, GMLP_CHUNK), 1)
    causal = s_i <= t_i
    w_spatial = [jnp.where(causal, ws_ref[g], 0.0).astype(BF16) for g in range(GMLP_GROUPS)]
    tm = x_ref.shape[1] // PROJ_PARTS
    n_ck = tm // GMLP_CHUNK
    for part in range(PROJ_PARTS):
        rows = slice(part * tm, (part + 1) * tm)
        x = x_ref[0, rows, :]
        ms = jnp.mean(x * x, axis=-1, keepdims=True)
        hn = ((x * lax.rsqrt(ms + NORM_EPS)) * g_ref[...]).astype(BF16)
        def proj_t(lo, hi):
            return lax.dot_general(wt_ref[lo:hi, :], hn, (((1,), (1,)), ((), ())),
                                   preferred_element_type=F32)

        def proj_n(lo, hi):
            return jnp.dot(hn, wn_ref[:, lo:hi], preferred_element_type=F32)

        vg = proj_t(0, 288)
        vt = vg[0:256].astype(BF16)
        gt = _sigmoid(vg[256:288])
        for ck in range(n_ck):
            sl = slice(ck * GMLP_CHUNK, (ck + 1) * GMLP_CHUNK)
            vt_ref[0, part * n_ck + ck] = vt[:, sl]
            gt_ref[0, part * n_ck + ck] = gt[:, sl]
        gu = _gelu_tanh(proj_t(288, 800))
        gv = _gelu_tanh(proj_t(800, 1312)).reshape(GMLP_GROUPS, GMLP_GROUP_DIM, tm)
        zg = proj_t(1312, 1824)
        mu = jnp.mean(gv, axis=1, keepdims=True)
        dv = gv - mu
        var = jnp.mean(dv * dv, axis=1, keepdims=True)
        vn = (dv * lax.rsqrt(var + NORM_EPS)) * lng_ref[...] + lnb_ref[...]
        outs = []
        for g in range(GMLP_GROUPS):
            a = vn[g].astype(BF16)
            a_st = jnp.concatenate(
                [a[:, ck * GMLP_CHUNK:(ck + 1) * GMLP_CHUNK] for ck in range(n_ck)], axis=0)
            r = jnp.dot(a_st, w_spatial[g], preferred_element_type=F32) + bs_ref[g]
            outs.append(jnp.concatenate(
                [r[ck * GMLP_GROUP_DIM:(ck + 1) * GMLP_GROUP_DIM] for ck in range(n_ck)], axis=1))
        mixed = jnp.concatenate(outs, axis=0)
        og_t = (gu * mixed) * (zg * _sigmoid(zg))
        og_ref[0, rows, :] = og_t.T.astype(BF16)
        q_ref[0, rows, :] = (proj_n(0, 512) * (HEAD_DIM ** -0.5 * LOG2_E)).astype(BF16)
        kv = proj_n(512, 1024)
        cmp_rows = slice(part * (tm // CMP_STRIDE), (part + 1) * (tm // CMP_STRIDE))
        for src, dst_ref, lo in ((0, kc_ref, 0), (1, vc_ref, KV_WIDTH)):
            cmp_scr[part, src] = kv[:, lo:lo + KV_WIDTH]
            for pos in range(CMP_STRIDE):
                dst_ref[0, cmp_rows, pos * KV_WIDTH:(pos + 1) * KV_WIDTH] = cmp_scr[
                    part, src, pl.ds(pos, tm // CMP_STRIDE, stride=CMP_STRIDE), :].astype(BF16)
        ksw_ref[0, rows, :] = kv[:, 2 * KV_WIDTH:4 * KV_WIDTH].astype(BF16)
        z = proj_n(1024, 1536)
        zn_ref[0, rows, :] = (z * _sigmoid(z)).astype(BF16)


def _proj_call(x, g_in, wn, wt, lng, lnb, ws_t, bs, tm):
    B, T, D = x.shape
    n_ck = tm // GMLP_CHUNK
    grid = (B, T // tm)
    const = lambda *shape: pl.BlockSpec(shape, lambda b, i: (0,) * len(shape))
    row = lambda w: pl.BlockSpec((1, tm, w), lambda b, i: (b, i, 0))
    out_shape = (
        jax.ShapeDtypeStruct((B, T, NSA_WIDTH), BF16),
        jax.ShapeDtypeStruct((B, T // CMP_STRIDE, CMP_STRIDE * KV_WIDTH), BF16),
        jax.ShapeDtypeStruct((B, T // CMP_STRIDE, CMP_STRIDE * KV_WIDTH), BF16),
        jax.ShapeDtypeStruct((B, T, 2 * KV_WIDTH), BF16),
        jax.ShapeDtypeStruct((B, T, NSA_WIDTH), BF16),
        jax.ShapeDtypeStruct((B, T // GMLP_CHUNK, 256, GMLP_CHUNK), BF16),
        jax.ShapeDtypeStruct((B, T // GMLP_CHUNK, 2 * GATE_ROWS, GMLP_CHUNK), F32),
        jax.ShapeDtypeStruct((B, T, GMLP_WIDTH), BF16),
    )
    out_specs = (
        row(NSA_WIDTH),
        pl.BlockSpec((1, tm // CMP_STRIDE, CMP_STRIDE * KV_WIDTH), lambda b, i: (b, i, 0)),
        pl.BlockSpec((1, tm // CMP_STRIDE, CMP_STRIDE * KV_WIDTH), lambda b, i: (b, i, 0)),
        row(2 * KV_WIDTH), row(NSA_WIDTH),
        pl.BlockSpec((1, n_ck, 256, GMLP_CHUNK), lambda b, i: (b, i, 0, 0)),
        pl.BlockSpec((1, n_ck, 2 * GATE_ROWS, GMLP_CHUNK), lambda b, i: (b, i, 0, 0)),
        row(GMLP_WIDTH),
    )
    in_specs = [
        pl.BlockSpec((1, tm, D), lambda b, i: (b, i, 0)),
        const(1, D), const(*wn.shape), const(*wt.shape),
        const(*lng.shape), const(*lnb.shape), const(*ws_t.shape), const(*bs.shape),
    ]
    return pl.pallas_call(
        _proj_kernel, out_shape=out_shape, grid=grid, in_specs=in_specs, out_specs=out_specs,
        scratch_shapes=[pltpu.VMEM((PROJ_PARTS, 2, tm // PROJ_PARTS, KV_WIDTH), F32)],
        name="proj",
        compiler_params=pltpu.CompilerParams(
            dimension_semantics=("arbitrary", "arbitrary"), vmem_limit_bytes=VMEM_LIMIT),
    )(x, g_in, wn, wt, lng, lnb, ws_t, bs)


def _compress_kernel(kc_ref, vc_ref, pek_ref, pev_ref, wk1_ref, wv1_ref, wk2_ref, wv2t_ref,
                     kcmp_ref, vcmpt_ref):
    n_c = kc_ref.shape[1]
    half = 2 * CMP_HIDDEN

    def hidden(src_ref, pe_ref, w1_ref):
        xs = src_ref[0].astype(F32)
        xa = (xs + pe_ref[0:1, :]).astype(BF16)
        xb = (xs + pe_ref[1:2, :]).astype(BF16)
        ha = jnp.dot(xa, w1_ref[:, 0:half], preferred_element_type=F32)
        hb = jnp.dot(xb, w1_ref[:, half:2 * half], preferred_element_type=F32)
        hb_next = pltpu.roll(hb, shift=n_c - 1, axis=0)
        return _gelu_tanh(ha + hb_next).astype(BF16)

    hk = hidden(kc_ref, pek_ref, wk1_ref)
    kcmp_ref[0] = jnp.dot(hk, wk2_ref[...], preferred_element_type=F32).astype(BF16)
    hv = hidden(vc_ref, pev_ref, wv1_ref)
    vcmpt_ref[0] = lax.dot_general(wv2t_ref[...], hv, (((1,), (1,)), ((), ())),
                                   preferred_element_type=F32).astype(BF16)


def _compress_call(kc2, vc2, pek, pev, wk1, wv1, wk2, wv2t):
    B, n_c, W = kc2.shape
    const = lambda a: pl.BlockSpec(a.shape, lambda b: (0,) * a.ndim)
    return pl.pallas_call(
        _compress_kernel,
        out_shape=(jax.ShapeDtypeStruct((B, n_c, KV_WIDTH), BF16),
                   jax.ShapeDtypeStruct((B, KV_WIDTH, n_c), BF16)),
        grid=(B,),
        in_specs=[pl.BlockSpec((1, n_c, W), lambda b: (b, 0, 0)),
                  pl.BlockSpec((1, n_c, W), lambda b: (b, 0, 0)),
                  const(pek), const(pev), const(wk1), const(wv1), const(wk2), const(wv2t)],
        out_specs=(pl.BlockSpec((1, n_c, KV_WIDTH), lambda b: (b, 0, 0)),
                   pl.BlockSpec((1, KV_WIDTH, n_c), lambda b: (b, 0, 0))),
        name="compress",
        compiler_params=pltpu.CompilerParams(
            dimension_semantics=("arbitrary",), vmem_limit_bytes=VMEM_LIMIT),
    )(kc2, vc2, pek, pev, wk1, wv1, wk2, wv2t)


def _attn_kernel(q_ref, kcmp_ref, vcmpt_ref, ks_ref, kw_ref, vst_ref, vwt_ref, gt_ref, zn_ref,
                 o_ref, mask_scr, onehot_scr, qb_scr, m_scr, acc_scr, o_scr, s_scr, p_scr,
                 s1_scr, p1_scr, m1_scr):
    g = pl.program_id(1)
    T = q_ref.shape[1]
    n_c = kcmp_ref.shape[1]
    n_blk = T // SEL_BLOCK
    n_tiles = T // Q_TILE
    assert KV_WIDTH + n_blk <= MXU_DEPTH

    def with_ones(vt):
        return jnp.concatenate([vt, jnp.ones((ONES_ROWS, vt.shape[1]), vt.dtype)], axis=0)

    def normalized(acc):
        return acc[0:HEAD_DIM] / acc[HEAD_DIM:HEAD_DIM + 1]

    def gate_row(c, br):
        gt = gt_ref[0, c]
        return jnp.concatenate(
            [gt[4 * br + h:4 * br + h + 1] for h in range(HEADS_PER_GROUP)], axis=1)

    lane_q = lax.broadcasted_iota(jnp.int32, (1, LANES), 1) & (Q_TILE - 1)
    MASK_CAUSAL, MASK_WINDOW, MASK_NONE, MASK_ALL = range(4)

    @pl.when((pl.program_id(0) == 0) & (g == 0))
    def _():
        key_i = lax.broadcasted_iota(jnp.int32, (T, MXU_DEPTH - KV_WIDTH), 0)
        blk_i = lax.broadcasted_iota(jnp.int32, (T, MXU_DEPTH - KV_WIDTH), 1)
        onehot_scr[...] = jnp.where((key_i >> (SEL_BLOCK.bit_length() - 1)) == blk_i,
                                    1.0, 0.0).astype(BF16)
        if KV_WIDTH + n_blk < MXU_DEPTH:
            qb_scr[:, KV_WIDTH + n_blk:, :] = jnp.zeros(
                (n_tiles, MXU_DEPTH - KV_WIDTH - n_blk, LANES), BF16)
        key_row = lax.broadcasted_iota(jnp.int32, (KEY_CHUNK, LANES), 0)
        mask_scr[MASK_CAUSAL] = jnp.where(key_row <= lane_q, 0.0, NEG_INF)
        mask_scr[MASK_WINDOW] = jnp.where(key_row > lane_q, 0.0, NEG_INF)
        mask_scr[MASK_NONE] = jnp.zeros((KEY_CHUNK, LANES), F32)
        mask_scr[MASK_ALL] = jnp.full((KEY_CHUNK, LANES), NEG_INF, F32)

    kcmp = kcmp_ref[0]
    vct1 = with_ones(vcmpt_ref[0])
    jb_c = lax.broadcasted_iota(jnp.int32, (n_blk, n_c), 0)
    ic_c = lax.broadcasted_iota(jnp.int32, (n_blk, n_c), 1)
    overlap_t = jnp.where(ic_c * CMP_STRIDE < (jb_c + 1) * SEL_BLOCK,
                          jnp.where(ic_c * CMP_STRIDE + (CMP_BLOCK - 1) >= jb_c * SEL_BLOCK, 1.0, 0.0),
                          0.0).astype(BF16)
    row_group = lax.broadcasted_iota(jnp.int32, (KV_WIDTH, LANES), 0) >> (HEAD_DIM.bit_length() - 1)
    cmp_end = lax.broadcasted_iota(jnp.int32, (n_c, LANES), 0) * CMP_STRIDE + (CMP_BLOCK - 1)
    blk_row = lax.broadcasted_iota(jnp.int32, (n_blk, Q_TILE), 0)

    win_chunks = WIN_KEYS // KEY_CHUNK
    r_win, r_own, r_end = n_c, n_c + WIN_KEYS, n_c + WIN_KEYS + KEY_CHUNK

    def win_first_chunk(c):
        return jnp.maximum(c - WINDOW // KEY_CHUNK, 0)

    def tile_scores(c):
        t0 = c * Q_TILE if isinstance(c, int) else pl.multiple_of(c * Q_TILE, Q_TILE)
        qt = q_ref[0, pl.ds(t0, Q_TILE), :].astype(F32).T
        b64 = jnp.concatenate(
            [qt[h * HEAD_DIM:(h + 1) * HEAD_DIM] for h in range(HEADS_PER_GROUP)], axis=1)
        qmat = jnp.where(row_group == g, jnp.concatenate([b64, b64], axis=0), 0.0).astype(BF16)
        qb_scr[c, 0:KV_WIDTH, :] = qmat
        tq = t0 + lane_q
        s = jnp.where(cmp_end <= tq, jnp.dot(kcmp, qmat, preferred_element_type=F32), NEG_INF)
        s1_scr[0:r_win] = s
        m1_scr[0] = jnp.max(s, axis=0, keepdims=True)
        w0 = win_first_chunk(c)
        wk0 = pl.multiple_of(w0 * KEY_CHUNK, KEY_CHUNK)
        sw = jnp.dot(kw_ref[0, pl.ds(wk0, WIN_KEYS), :], qmat, preferred_element_type=F32)
        m_w = None
        for u in range(win_chunks):
            ck = w0 + u
            kind = jnp.where(ck == c, MASK_CAUSAL,
                             jnp.where(ck > c, MASK_ALL,
                                       jnp.where(ck == c - WINDOW // KEY_CHUNK, MASK_WINDOW, MASK_NONE)))
            su = sw[u * KEY_CHUNK:(u + 1) * KEY_CHUNK] + mask_scr[kind]
            s1_scr[r_win + u * KEY_CHUNK:r_win + (u + 1) * KEY_CHUNK] = su
            mu = jnp.max(su, axis=0, keepdims=True)
            m_w = mu if m_w is None else jnp.maximum(m_w, mu)
        m1_scr[1] = m_w
        sd = (jnp.dot(ks_ref[0, pl.ds(t0, KEY_CHUNK), :], qmat, preferred_element_type=F32)
              + mask_scr[MASK_CAUSAL])
        s1_scr[r_own:r_end] = sd
        m_scr[c] = jnp.max(sd, axis=0, keepdims=True)

    def tile_exponentials(c):
        p1_scr[0:r_win] = jnp.exp2(s1_scr[0:r_win] - m1_scr[0]).astype(BF16)
        p1_scr[r_win:r_own] = jnp.exp2(s1_scr[r_win:r_own] - m1_scr[1]).astype(BF16)
        p1_scr[r_own:r_end] = jnp.exp2(s1_scr[r_own:r_end] - m_scr[c]).astype(BF16)

    def tile_outputs(c, rank_rows):
        t0 = c * Q_TILE
        tq = t0 + lane_q
        e16 = p1_scr[0:r_win]
        acc_c = jnp.dot(vct1, e16, preferred_element_type=F32)
        linv_c = jnp.where(tq >= CMP_BLOCK - 1, 1.0 / acc_c[HEAD_DIM:HEAD_DIM + 1], 0.0)
        o_c = acc_c[0:HEAD_DIM] * linv_c
        imp4 = jnp.dot(overlap_t, e16, preferred_element_type=F32) * linv_c
        imp = imp4[:, 0:Q_TILE]
        for h in range(1, HEADS_PER_GROUP):
            imp = imp + imp4[:, h * Q_TILE:(h + 1) * Q_TILE]

        cur = (t0 + lax.broadcasted_iota(jnp.int32, (1, Q_TILE), 1)) >> (SEL_BLOCK.bit_length() - 1)
        valid_b = blk_row <= cur
        bonus = jnp.where(blk_row == 0, FORCE_BONUS,
                          jnp.where(blk_row == cur, FORCE_BONUS,
                                    jnp.where(blk_row == cur - 1, FORCE_BONUS, 0.0)))
        score = jnp.where(valid_b, imp + bonus, -1.0)
        before_tile = blk_row < c * (Q_TILE // SEL_BLOCK)
        sub_row = lax.broadcasted_iota(jnp.int32, (8, Q_TILE), 0)

        def keep_ranked(n_rows):
            n_rb = n_rows // 8
            sblk = [score[8 * r:8 * r + 8] for r in range(n_rb)]
            cnt = [jnp.zeros((8, Q_TILE), F32) for _ in range(n_rb)]
            for j in range(n_rows):
                rj = jnp.broadcast_to(score[j:j + 1], (8, Q_TILE))
                for r in range(n_rb):
                    if 8 * r + 7 <= j:
                        cnt[r] = cnt[r] + jnp.where(rj > sblk[r], 1.0, 0.0)
                    elif 8 * r > j:
                        cnt[r] = cnt[r] + jnp.where(rj >= sblk[r], 1.0, 0.0)
                    else:
                        cnt[r] = cnt[r] + jnp.where(sub_row + 8 * r > j,
                                                    jnp.where(rj >= sblk[r], 1.0, 0.0),
                                                    jnp.where(rj > sblk[r], 1.0, 0.0))
            in_top = jnp.concatenate(
                [jnp.where(cnt[r] < float(SEL_TOP_N), 0.0, NEG_INF) for r in range(n_rb)]
                + [jnp.full((n_blk - n_rows, Q_TILE), NEG_INF, F32)] * (n_rows < n_blk), axis=0)
            return jnp.where(before_tile, in_top, NEG_INF)

        if rank_rows == 0:
            keep = jnp.where(before_tile, 0.0, NEG_INF)
        else:
            keep = keep_ranked(rank_rows)
        qb_scr[c, KV_WIDTH:KV_WIDTH + n_blk, :] = jnp.concatenate(
            [keep] * HEADS_PER_GROUP, axis=1).astype(BF16)

        w0 = win_first_chunk(c)
        vw = jnp.concatenate([vwt_ref[0, w0 + u] for u in range(win_chunks)], axis=1)
        acc_w = jnp.dot(with_ones(vw), p1_scr[r_win:r_own], preferred_element_type=F32)
        acc_scr[c] = jnp.dot(with_ones(vst_ref[0, c]), p1_scr[r_own:r_end],
                             preferred_element_type=F32)
        o_scr[c] = gate_row(c, 0) * o_c + gate_row(c, 2) * normalized(acc_w)

    assert n_tiles >= 2
    tile_scores(0)
    tile_exponentials(0)
    tile_scores(1)

    blocks_per_tile = Q_TILE // SEL_BLOCK
    sizes = [0] + [r for r in (n_blk // 2,) if r % 8 == 0 and r > SEL_TOP_N] + [n_blk]
    limits = [SEL_TOP_N // blocks_per_tile] + [r // blocks_per_tile for r in sizes[1:]]
    start = 0
    for rank_rows, limit in zip(sizes, limits):
        stop = max(start, min(limit, n_tiles - 2))

        def body(c, rank_rows=rank_rows):
            tile_outputs(c, rank_rows)
            tile_exponentials(c + 1)
            tile_scores(c + 2)

        pl.loop(start, stop)(body)
        start = stop

    tile_outputs(n_tiles - 2, n_blk)
    tile_exponentials(n_tiles - 1)
    tile_outputs(n_tiles - 1, n_blk)

    chunks_per_step = SEL_KEYS // KEY_CHUNK
    n_sel_steps = T // SEL_KEYS
    n_groups = n_tiles // SEL_GROUP
    assert SEL_KEYS % (SEL_GROUP * Q_TILE) == 0 and n_tiles % SEL_GROUP == 0
    groups_per_step = SEL_KEYS // (SEL_GROUP * Q_TILE)
    n_items = sum(n_groups - groups_per_step * j for j in range(n_sel_steps))
    assert n_items >= 2

    def next_item(item):
        j, q = item
        wrap = q + 1 == n_groups
        return jnp.where(wrap, j + 1, j), jnp.where(wrap, groups_per_step * (j + 1), q + 1)

    def sel_scores(item):
        j, q = item
        k0 = pl.multiple_of(j * SEL_KEYS, SEL_KEYS)
        k_ext = jnp.concatenate([ks_ref[0, pl.ds(k0, SEL_KEYS), :], onehot_scr[pl.ds(k0, SEL_KEYS), :]],
                                axis=1)
        mxs = []
        for i in range(SEL_GROUP):
            sc = jnp.dot(k_ext, qb_scr[q * SEL_GROUP + i], preferred_element_type=F32)
            s_scr[i] = sc
            mxs.append(jnp.max(sc, axis=0, keepdims=True))
        return tuple(mxs)

    def sel_exponentials(mxs):
        for i in range(SEL_GROUP):
            p_scr[i] = jnp.exp2(s_scr[i] - mxs[i]).astype(BF16)

    def sel_accumulate(item, mxs):
        j, q = item
        vs = with_ones(jnp.concatenate(
            [vst_ref[0, chunks_per_step * j + u] for u in range(chunks_per_step)], axis=1))
        for i in range(SEL_GROUP):
            c = q * SEL_GROUP + i
            pv = jnp.dot(vs, p_scr[i], preferred_element_type=F32)
            m = m_scr[c]
            m_new = jnp.maximum(m, mxs[i])
            acc_scr[c] = jnp.exp2(m - m_new) * acc_scr[c] + jnp.exp2(mxs[i] - m_new) * pv
            m_scr[c] = m_new

    item_a = (jnp.int32(0), jnp.int32(0))
    item_b = next_item(item_a)
    mx_a = sel_scores(item_a)
    sel_exponentials(mx_a)
    mx_b = sel_scores(item_b)

    def sel_body(_, carry):
        item_a, item_b, mx_a, mx_b = carry
        item_c = next_item(item_b)
        sel_accumulate(item_a, mx_a)
        sel_exponentials(mx_b)
        return item_b, item_c, mx_b, sel_scores(item_c)

    item_a, item_b, mx_a, mx_b = lax.fori_loop(0, n_items - 2, sel_body, (item_a, item_b, mx_a, mx_b),
                                               unroll=SEL_UNROLL)
    sel_accumulate(item_a, mx_a)
    sel_exponentials(mx_b)
    sel_accumulate(item_b, mx_b)

    @pl.loop(0, n_tiles, unroll=OUT_UNROLL)
    def _(c):
        t0 = pl.multiple_of(c * Q_TILE, Q_TILE)
        o_t = o_scr[c] + gate_row(c, 1) * normalized(acc_scr[c])
        stacked = jnp.concatenate(
            [o_t[:, h * Q_TILE:(h + 1) * Q_TILE] for h in range(HEADS_PER_GROUP)], axis=0)
        o = stacked.T * zn_ref[0, pl.ds(t0, Q_TILE), :].astype(F32)
        o_ref[0, pl.ds(t0, Q_TILE), :] = o.astype(BF16)


def _attn_call(q, kcmp, vcmpt, ksw, vt, gt, zn):
    B, T, _ = q.shape
    n_c = kcmp.shape[1]
    n_ck = T // KEY_CHUNK
    n_tiles = T // Q_TILE
    G = NSA_KV_HEADS
    gw = HEADS_PER_GROUP * HEAD_DIM
    in_specs = [
        pl.BlockSpec((1, T, gw), lambda b, g: (b, 0, g)),
        pl.BlockSpec((1, n_c, KV_WIDTH), lambda b, g: (b, 0, 0)),
        pl.BlockSpec((1, HEAD_DIM, n_c), lambda b, g: (b, g, 0)),
        pl.BlockSpec((1, T, KV_WIDTH), lambda b, g: (b, 0, 0)),
        pl.BlockSpec((1, T, KV_WIDTH), lambda b, g: (b, 0, 1)),
        pl.BlockSpec((1, n_ck, HEAD_DIM, KEY_CHUNK), lambda b, g: (b, 0, g, 0)),
        pl.BlockSpec((1, n_ck, HEAD_DIM, KEY_CHUNK), lambda b, g: (b, 0, G + g, 0)),
        pl.BlockSpec((1, n_ck, GATE_ROWS, KEY_CHUNK), lambda b, g: (b, 0, g, 0)),
        pl.BlockSpec((1, T, gw), lambda b, g: (b, 0, g)),
    ]
    return pl.pallas_call(
        _attn_kernel,
        out_shape=jax.ShapeDtypeStruct((B, T, NSA_WIDTH), BF16),
        grid=(B, G),
        in_specs=in_specs,
        out_specs=pl.BlockSpec((1, T, gw), lambda b, g: (b, 0, g)),
        scratch_shapes=[pltpu.VMEM((4, KEY_CHUNK, LANES), F32),
                        pltpu.VMEM((T, MXU_DEPTH - KV_WIDTH), BF16),
                        pltpu.VMEM((n_tiles, MXU_DEPTH, LANES), BF16),
                        pltpu.VMEM((n_tiles, 1, LANES), F32),
                        pltpu.VMEM((n_tiles, HEAD_DIM + ONES_ROWS, LANES), F32),
                        pltpu.VMEM((n_tiles, HEAD_DIM, LANES), F32),
                        pltpu.VMEM((SEL_GROUP, SEL_KEYS, LANES), F32),
                        pltpu.VMEM((SEL_GROUP, SEL_KEYS, LANES), BF16),
                        pltpu.VMEM((n_c + WIN_KEYS + KEY_CHUNK, LANES), F32),
                        pltpu.VMEM((n_c + WIN_KEYS + KEY_CHUNK, LANES), BF16),
                        pltpu.VMEM((2, 1, LANES), F32)],
        name="attn",
        compiler_params=pltpu.CompilerParams(
            dimension_semantics=("arbitrary", "arbitrary"), vmem_limit_bytes=ATTN_VMEM_LIMIT),
    )(q, kcmp, vcmpt, ksw, ksw, vt, vt, gt, zn)


def _out_kernel(x_ref, mn_ref, mg_ref, wo_ref, g_ref, o_ref):
    h = x_ref[0]
    h = h + jnp.dot(mn_ref[0], wo_ref[0:NSA_WIDTH, :], preferred_element_type=F32)
    h = h + jnp.dot(mg_ref[0], wo_ref[NSA_WIDTH:, :], preferred_element_type=F32)
    ms = jnp.mean(h * h, axis=-1, keepdims=True)
    o_ref[0] = (h * lax.rsqrt(ms + NORM_EPS)) * g_ref[...]


def _out_call(x, mix_nsa, mix_gmlp, wo, g_f, tm):
    B, T, D = x.shape
    row = lambda w: pl.BlockSpec((1, tm, w), lambda b, i: (b, i, 0))
    return pl.pallas_call(
        _out_kernel,
        out_shape=jax.ShapeDtypeStruct((B, T, D), x.dtype),
        grid=(B, T // tm),
        in_specs=[row(D), row(NSA_WIDTH), row(GMLP_WIDTH),
                  pl.BlockSpec(wo.shape, lambda b, i: (0, 0)),
                  pl.BlockSpec((1, D), lambda b, i: (0, 0))],
        out_specs=row(D),
        name="out",
        compiler_params=pltpu.CompilerParams(
            dimension_semantics=("arbitrary", "arbitrary"), vmem_limit_bytes=VMEM_LIMIT),
    )(x, mix_nsa, mix_gmlp, wo, g_f)


def _split_w_in(w):
    sizes = (NSA_WIDTH, KV_WIDTH, KV_WIDTH, KV_WIDTH, KV_WIDTH, KV_WIDTH, KV_WIDTH, N_GATES,
             NSA_WIDTH, GMLP_WIDTH, GMLP_WIDTH, GMLP_WIDTH)
    parts, off = [], 0
    for s in sizes:
        parts.append(w[:, off:off + s])
        off += s
    return parts


def _gate_columns(w_gate):
    d = w_gate.shape[0]
    wg = w_gate.reshape(d, NSA_KV_HEADS, HEADS_PER_GROUP, 3).transpose(0, 1, 3, 2)
    wg = wg.reshape(d, NSA_KV_HEADS, 3 * HEADS_PER_GROUP)
    wg = jnp.pad(wg, ((0, 0), (0, 0), (0, GATE_ROWS - 3 * HEADS_PER_GROUP)))
    return wg.reshape(d, NSA_KV_HEADS * GATE_ROWS)


def _compress_w1(w1):
    half = CMP_BLOCK // 2
    w = w1.reshape(2, half, HEAD_DIM, CMP_HIDDEN)
    eye = jnp.eye(NSA_KV_HEADS, dtype=w1.dtype)
    full = jnp.einsum('hlde,gk->lgdhke', w, eye)
    return full.reshape(half * KV_WIDTH, 2 * NSA_KV_HEADS * CMP_HIDDEN)


def _compress_pe(pe):
    half = CMP_BLOCK // 2
    p = pe.reshape(2, half, 1, HEAD_DIM)
    p = jnp.broadcast_to(p, (2, half, NSA_KV_HEADS, HEAD_DIM))
    return p.reshape(2, half * KV_WIDTH)


def _block_diag2(w2):
    z = jnp.zeros_like(w2)
    return jnp.concatenate([jnp.concatenate([w2, z], axis=1),
                            jnp.concatenate([z, w2], axis=1)], axis=0)


def kernel(x, norm_in_g, w_in, w_cmp_k1, w_cmp_k2, pe_cmp_k, w_cmp_v1, w_cmp_v2, pe_cmp_v,
           gmlp_ln_g, gmlp_ln_b, w_spatial, b_spatial, w_out, norm_f_g):
    B, T, D = x.shape
    assert w_in.shape[0] == 1, "single-layer block"
    tm = min(PROJ_TILE, T)
    (wq, wkc, wvc, wks, wvs, wkw, wvw, wgt, wzn, wu, wv, wzg) = _split_w_in(w_in[0])
    wn = jnp.concatenate([wq, wkc, wvc, wks, wkw, wzn], axis=1).astype(BF16)
    wt = jnp.concatenate([wvs, wvw, _gate_columns(wgt), wu, wv, wzg], axis=1).T.astype(BF16)
    lng = gmlp_ln_g[0].reshape(GMLP_GROUPS, GMLP_GROUP_DIM, 1)
    lnb = gmlp_ln_b[0].reshape(GMLP_GROUPS, GMLP_GROUP_DIM, 1)
    ws_t = jnp.swapaxes(w_spatial[0], 1, 2)
    bs = b_spatial[0].reshape(GMLP_GROUPS, 1, GMLP_CHUNK)
    q, kc, vc, ksw, zn, vt, gt, o_gmlp = _proj_call(
        x, norm_in_g[0].reshape(1, D), wn, wt, lng, lnb, ws_t, bs, tm)

    n_c = T // CMP_STRIDE
    kcmp, vcmpt = _compress_call(
        kc, vc,
        _compress_pe(pe_cmp_k[0]), _compress_pe(pe_cmp_v[0]),
        _compress_w1(w_cmp_k1[0]).astype(BF16), _compress_w1(w_cmp_v1[0]).astype(BF16),
        _block_diag2(w_cmp_k2[0]).astype(BF16), _block_diag2(w_cmp_v2[0]).T.astype(BF16))

    o_nsa = _attn_call(q, kcmp, vcmpt, ksw, vt, gt, zn)
    return _out_call(x, o_nsa, o_gmlp, w_out[0].astype(BF16), norm_f_g.reshape(1, D),
                     min(OUT_TILE, T))
```

```python
import math

import jax
import jax.numpy as jnp
from jax import lax
from jax.experimental import pallas as pl
from jax.experimental.pallas import tpu as pltpu

F32 = jnp.float32
BF16 = jnp.bfloat16

HEAD_DIM = 64
NSA_HEADS = 8
NSA_KV_HEADS = 2
HEADS_PER_GROUP = NSA_HEADS // NSA_KV_HEADS
NSA_WIDTH = NSA_HEADS * HEAD_DIM
KV_WIDTH = NSA_KV_HEADS * HEAD_DIM
N_GATES = 3 * NSA_HEADS
GMLP_GROUPS = 8
GMLP_GROUP_DIM = 64
GMLP_WIDTH = GMLP_GROUPS * GMLP_GROUP_DIM
GMLP_CHUNK = 128
CMP_BLOCK = 32
CMP_STRIDE = 16
CMP_HIDDEN = 128
SEL_BLOCK = 64
SEL_TOP_N = 16
WINDOW = 512
NORM_EPS = 1e-6
NEG_INF = -1e30
FORCE_BONUS = 1e4

Q_TILE = 128
KEY_CHUNK = 128
LANES = HEADS_PER_GROUP * Q_TILE
SEL_KEYS = 512
SEL_GROUP = 4
OUT_UNROLL = 4
WIN_KEYS = WINDOW + Q_TILE
ONES_ROWS = 16
LOG2_E = math.log2(math.e)
GATE_ROWS = 16
OUT_TILE = 1024
PROJ_TILE = 1024
MXU_DEPTH = 256
V7X_VMEM_BYTES = 64 * 1024 * 1024
VMEM_LIMIT = V7X_VMEM_BYTES * 3 // 4
ATTN_VMEM_LIMIT = V7X_VMEM_BYTES * 7 // 8


def _sigmoid(x):
    return 1.0 / (1.0 + jnp.exp(-x))


def _gelu_tanh(x):
    c = math.sqrt(2.0 / math.pi)
    return x * (0.5 * (1.0 + jnp.tanh(c * (x + 0.044715 * (x * x * x)))))


def _proj_kernel(x_ref, g_ref, wn_ref, wt_ref, lng_ref, lnb_ref, ws_ref, bs_ref,
                 q_ref, kc_ref, vc_ref, ksw_ref, zn_ref, vt_ref, gt_ref, og_ref, cmp_scr):
    tm = x_ref.shape[1]
    n_ck = tm // GMLP_CHUNK
    x = x_ref[0]
    ms = jnp.mean(x * x, axis=-1, keepdims=True)
    hn = ((x * lax.rsqrt(ms + NORM_EPS)) * g_ref[...]).astype(BF16)

    def proj_t(lo, hi):
        return lax.dot_general(wt_ref[lo:hi, :], hn, (((1,), (1,)), ((), ())),
                               preferred_element_type=F32)

    def proj_n(lo, hi):
        return jnp.dot(hn, wn_ref[:, lo:hi], preferred_element_type=F32)

    vg = proj_t(0, 288)
    vt = vg[0:256].astype(BF16)
    gt = _sigmoid(vg[256:288])
    for ck in range(n_ck):
        sl = slice(ck * GMLP_CHUNK, (ck + 1) * GMLP_CHUNK)
        vt_ref[0, ck] = vt[:, sl]
        gt_ref[0, ck] = gt[:, sl]
    gu = _gelu_tanh(proj_t(288, 800))
    gv = _gelu_tanh(proj_t(800, 1312)).reshape(GMLP_GROUPS, GMLP_GROUP_DIM, tm)
    zg = proj_t(1312, 1824)
    mu = jnp.mean(gv, axis=1, keepdims=True)
    dv = gv - mu
    var = jnp.mean(dv * dv, axis=1, keepdims=True)
    vn = (dv * lax.rsqrt(var + NORM_EPS)) * lng_ref[...] + lnb_ref[...]
    s_i = lax.broadcasted_iota(jnp.int32, (GMLP_CHUNK, GMLP_CHUNK), 0)
    t_i = lax.broadcasted_iota(jnp.int32, (GMLP_CHUNK, GMLP_CHUNK), 1)
    causal = s_i <= t_i
    outs = []
    for g in range(GMLP_GROUPS):
        a = vn[g].astype(BF16)
        a_st = jnp.concatenate(
            [a[:, ck * GMLP_CHUNK:(ck + 1) * GMLP_CHUNK] for ck in range(n_ck)], axis=0)
        w = jnp.where(causal, ws_ref[g], 0.0).astype(BF16)
        r = jnp.dot(a_st, w, preferred_element_type=F32) + bs_ref[g]
        outs.append(jnp.concatenate(
            [r[ck * GMLP_GROUP_DIM:(ck + 1) * GMLP_GROUP_DIM] for ck in range(n_ck)], axis=1))
    mixed = jnp.concatenate(outs, axis=0)
    og_t = (gu * mixed) * (zg * _sigmoid(zg))
    og_ref[0] = og_t.T.astype(BF16)
    q_ref[0] = (proj_n(0, 512) * (HEAD_DIM ** -0.5 * LOG2_E)).astype(BF16)
    kv = proj_n(512, 1024)
    for src, dst_ref, lo in ((0, kc_ref, 0), (1, vc_ref, KV_WIDTH)):
        cmp_scr[src] = kv[:, lo:lo + KV_WIDTH]
        for pos in range(CMP_STRIDE):
            dst_ref[0, :, pos * KV_WIDTH:(pos + 1) * KV_WIDTH] = cmp_scr[
                src, pl.ds(pos, tm // CMP_STRIDE, stride=CMP_STRIDE), :].astype(BF16)
    ksw_ref[0] = kv[:, 2 * KV_WIDTH:4 * KV_WIDTH].astype(BF16)
    z = proj_n(1024, 1536)
    zn_ref[0] = (z * _sigmoid(z)).astype(BF16)


def _proj_call(x, g_in, wn, wt, lng, lnb, ws_t, bs, tm):
    B, T, D = x.shape
    n_ck = tm // GMLP_CHUNK
    grid = (B, T // tm)
    const = lambda *shape: pl.BlockSpec(shape, lambda b, i: (0,) * len(shape))
    row = lambda w: pl.BlockSpec((1, tm, w), lambda b, i: (b, i, 0))
    out_shape = (
        jax.ShapeDtypeStruct((B, T, NSA_WIDTH), BF16),
        jax.ShapeDtypeStruct((B, T // CMP_STRIDE, CMP_STRIDE * KV_WIDTH), BF16),
        jax.ShapeDtypeStruct((B, T // CMP_STRIDE, CMP_STRIDE * KV_WIDTH), BF16),
        jax.ShapeDtypeStruct((B, T, 2 * KV_WIDTH), BF16),
        jax.ShapeDtypeStruct((B, T, NSA_WIDTH), BF16),
        jax.ShapeDtypeStruct((B, T // GMLP_CHUNK, 256, GMLP_CHUNK), BF16),
        jax.ShapeDtypeStruct((B, T // GMLP_CHUNK, 2 * GATE_ROWS, GMLP_CHUNK), F32),
        jax.ShapeDtypeStruct((B, T, GMLP_WIDTH), BF16),
    )
    out_specs = (
        row(NSA_WIDTH),
        pl.BlockSpec((1, tm // CMP_STRIDE, CMP_STRIDE * KV_WIDTH), lambda b, i: (b, i, 0)),
        pl.BlockSpec((1, tm // CMP_STRIDE, CMP_STRIDE * KV_WIDTH), lambda b, i: (b, i, 0)),
        row(2 * KV_WIDTH), row(NSA_WIDTH),
        pl.BlockSpec((1, n_ck, 256, GMLP_CHUNK), lambda b, i: (b, i, 0, 0)),
        pl.BlockSpec((1, n_ck, 2 * GATE_ROWS, GMLP_CHUNK), lambda b, i: (b, i, 0, 0)),
        row(GMLP_WIDTH),
    )
    in_specs = [
        pl.BlockSpec((1, tm, D), lambda b, i: (b, i, 0)),
        const(1, D), const(*wn.shape), const(*wt.shape),
        const(*lng.shape), const(*lnb.shape), const(*ws_t.shape), const(*bs.shape),
    ]
    return pl.pallas_call(
        _proj_kernel, out_shape=out_shape, grid=grid, in_specs=in_specs, out_specs=out_specs,
        scratch_shapes=[pltpu.VMEM((2, tm, KV_WIDTH), F32)],
        name="proj",
        compiler_params=pltpu.CompilerParams(
            dimension_semantics=("arbitrary", "arbitrary"), vmem_limit_bytes=VMEM_LIMIT),
    )(x, g_in, wn, wt, lng, lnb, ws_t, bs)


def _compress_kernel(kc_ref, vc_ref, pek_ref, pev_ref, wk1_ref, wv1_ref, wk2_ref, wv2t_ref,
                     kcmp_ref, vcmpt_ref):
    n_c = kc_ref.shape[1]
    half = 2 * CMP_HIDDEN

    def hidden(src_ref, pe_ref, w1_ref):
        xs = src_ref[0].astype(F32)
        xa = (xs + pe_ref[0:1, :]).astype(BF16)
        xb = (xs + pe_ref[1:2, :]).astype(BF16)
        ha = jnp.dot(xa, w1_ref[:, 0:half], preferred_element_type=F32)
        hb = jnp.dot(xb, w1_ref[:, half:2 * half], preferred_element_type=F32)
        hb_next = pltpu.roll(hb, shift=n_c - 1, axis=0)
        return _gelu_tanh(ha + hb_next).astype(BF16)

    hk = hidden(kc_ref, pek_ref, wk1_ref)
    kcmp_ref[0] = jnp.dot(hk, wk2_ref[...], preferred_element_type=F32).astype(BF16)
    hv = hidden(vc_ref, pev_ref, wv1_ref)
    vcmpt_ref[0] = lax.dot_general(wv2t_ref[...], hv, (((1,), (1,)), ((), ())),
                                   preferred_element_type=F32).astype(BF16)


def _compress_call(kc2, vc2, pek, pev, wk1, wv1, wk2, wv2t):
    B, n_c, W = kc2.shape
    const = lambda a: pl.BlockSpec(a.shape, lambda b: (0,) * a.ndim)
    return pl.pallas_call(
        _compress_kernel,
        out_shape=(jax.ShapeDtypeStruct((B, n_c, KV_WIDTH), BF16),
                   jax.ShapeDtypeStruct((B, KV_WIDTH, n_c), BF16)),
        grid=(B,),
        in_specs=[pl.BlockSpec((1, n_c, W), lambda b: (b, 0, 0)),
                  pl.BlockSpec((1, n_c, W), lambda b: (b, 0, 0)),
                  const(pek), const(pev), const(wk1), const(wv1), const(wk2), const(wv2t)],
        out_specs=(pl.BlockSpec((1, n_c, KV_WIDTH), lambda b: (b, 0, 0)),
                   pl.BlockSpec((1, KV_WIDTH, n_c), lambda b: (b, 0, 0))),
        name="compress",
        compiler_params=pltpu.CompilerParams(
            dimension_semantics=("arbitrary",), vmem_limit_bytes=VMEM_LIMIT),
    )(kc2, vc2, pek, pev, wk1, wv1, wk2, wv2t)


def _attn_kernel(q_ref, kcmp_ref, vcmpt_ref, ks_ref, kw_ref, vst_ref, vwt_ref, gt_ref, zn_ref,
                 o_ref, mask_scr, onehot_scr, qb_scr, m_scr, acc_scr, o_scr, s_scr, p_scr,
                 s1_scr, p1_scr, m1_scr):
    g = pl.program_id(1)
    T = q_ref.shape[1]
    n_c = kcmp_ref.shape[1]
    n_blk = T // SEL_BLOCK
    n_tiles = T // Q_TILE
    assert KV_WIDTH + n_blk <= MXU_DEPTH

    def with_ones(vt):
        return jnp.concatenate([vt, jnp.ones((ONES_ROWS, vt.shape[1]), vt.dtype)], axis=0)

    def normalized(acc):
        return acc[0:HEAD_DIM] / acc[HEAD_DIM:HEAD_DIM + 1]

    def gate_row(c, br):
        gt = gt_ref[0, c]
        return jnp.concatenate(
            [gt[4 * br + h:4 * br + h + 1] for h in range(HEADS_PER_GROUP)], axis=1)

    lane_q = lax.broadcasted_iota(jnp.int32, (1, LANES), 1) & (Q_TILE - 1)
    MASK_CAUSAL, MASK_WINDOW, MASK_NONE, MASK_ALL = range(4)

    @pl.when((pl.program_id(0) == 0) & (g == 0))
    def _():
        key_i = lax.broadcasted_iota(jnp.int32, (T, MXU_DEPTH - KV_WIDTH), 0)
        blk_i = lax.broadcasted_iota(jnp.int32, (T, MXU_DEPTH - KV_WIDTH), 1)
        onehot_scr[...] = jnp.where((key_i >> (SEL_BLOCK.bit_length() - 1)) == blk_i,
                                    1.0, 0.0).astype(BF16)
        if KV_WIDTH + n_blk < MXU_DEPTH:
            qb_scr[:, KV_WIDTH + n_blk:, :] = jnp.zeros(
                (n_tiles, MXU_DEPTH - KV_WIDTH - n_blk, LANES), BF16)
        key_row = lax.broadcasted_iota(jnp.int32, (KEY_CHUNK, LANES), 0)
        mask_scr[MASK_CAUSAL] = jnp.where(key_row <= lane_q, 0.0, NEG_INF)
        mask_scr[MASK_WINDOW] = jnp.where(key_row > lane_q, 0.0, NEG_INF)
        mask_scr[MASK_NONE] = jnp.zeros((KEY_CHUNK, LANES), F32)
        mask_scr[MASK_ALL] = jnp.full((KEY_CHUNK, LANES), NEG_INF, F32)

    kcmp = kcmp_ref[0]
    vct1 = with_ones(vcmpt_ref[0])
    jb_c = lax.broadcasted_iota(jnp.int32, (n_blk, n_c), 0)
    ic_c = lax.broadcasted_iota(jnp.int32, (n_blk, n_c), 1)
    overlap_t = jnp.where(ic_c * CMP_STRIDE < (jb_c + 1) * SEL_BLOCK,
                          jnp.where(ic_c * CMP_STRIDE + (CMP_BLOCK - 1) >= jb_c * SEL_BLOCK, 1.0, 0.0),
                          0.0).astype(BF16)
    row_group = lax.broadcasted_iota(jnp.int32, (KV_WIDTH, LANES), 0) >> (HEAD_DIM.bit_length() - 1)
    cmp_end = lax.broadcasted_iota(jnp.int32, (n_c, LANES), 0) * CMP_STRIDE + (CMP_BLOCK - 1)
    blk_row = lax.broadcasted_iota(jnp.int32, (n_blk, Q_TILE), 0)

    win_chunks = WIN_KEYS // KEY_CHUNK
    r_win, r_own, r_end = n_c, n_c + WIN_KEYS, n_c + WIN_KEYS + KEY_CHUNK

    def win_first_chunk(c):
        return jnp.maximum(c - WINDOW // KEY_CHUNK, 0)

    def tile_scores(c):
        t0 = c * Q_TILE if isinstance(c, int) else pl.multiple_of(c * Q_TILE, Q_TILE)
        qt = q_ref[0, pl.ds(t0, Q_TILE), :].astype(F32).T
        b64 = jnp.concatenate(
            [qt[h * HEAD_DIM:(h + 1) * HEAD_DIM] for h in range(HEADS_PER_GROUP)], axis=1)
        qmat = jnp.where(row_group == g, jnp.concatenate([b64, b64], axis=0), 0.0).astype(BF16)
        qb_scr[c, 0:KV_WIDTH, :] = qmat
        tq = t0 + lane_q
        s = jnp.where(cmp_end <= tq, jnp.dot(kcmp, qmat, preferred_element_type=F32), NEG_INF)
        s1_scr[0:r_win] = s
        m1_scr[0] = jnp.max(s, axis=0, keepdims=True)
        w0 = win_first_chunk(c)
        wk0 = pl.multiple_of(w0 * KEY_CHUNK, KEY_CHUNK)
        sw = jnp.dot(kw_ref[0, pl.ds(wk0, WIN_KEYS), :], qmat, preferred_element_type=F32)
        m_w = None
        for u in range(win_chunks):
            ck = w0 + u
            kind = jnp.where(ck == c, MASK_CAUSAL,
                             jnp.where(ck > c, MASK_ALL,
                                       jnp.where(ck == c - WINDOW // KEY_CHUNK, MASK_WINDOW, MASK_NONE)))
            su = sw[u * KEY_CHUNK:(u + 1) * KEY_CHUNK] + mask_scr[kind]
            s1_scr[r_win + u * KEY_CHUNK:r_win + (u + 1) * KEY_CHUNK] = su
            mu = jnp.max(su, axis=0, keepdims=True)
            m_w = mu if m_w is None else jnp.maximum(m_w, mu)
        m1_scr[1] = m_w
        sd = (jnp.dot(ks_ref[0, pl.ds(t0, KEY_CHUNK), :], qmat, preferred_element_type=F32)
              + mask_scr[MASK_CAUSAL])
        s1_scr[r_own:r_end] = sd
        m_scr[c] = jnp.max(sd, axis=0, keepdims=True)

    def tile_exponentials(c):
        p1_scr[0:r_win] = jnp.exp2(s1_scr[0:r_win] - m1_scr[0]).astype(BF16)
        p1_scr[r_win:r_own] = jnp.exp2(s1_scr[r_win:r_own] - m1_scr[1]).astype(BF16)
        p1_scr[r_own:r_end] = jnp.exp2(s1_scr[r_own:r_end] - m_scr[c]).astype(BF16)

    def tile_outputs(c, rank_rows):
        t0 = c * Q_TILE
        tq = t0 + lane_q
        e16 = p1_scr[0:r_win]
        acc_c = jnp.dot(vct1, e16, preferred_element_type=F32)
        linv_c = jnp.where(tq >= CMP_BLOCK - 1, 1.0 / acc_c[HEAD_DIM:HEAD_DIM + 1], 0.0)
        o_c = acc_c[0:HEAD_DIM] * linv_c
        imp4 = jnp.dot(overlap_t, e16, preferred_element_type=F32) * linv_c
        imp = imp4[:, 0:Q_TILE]
        for h in range(1, HEADS_PER_GROUP):
            imp = imp + imp4[:, h * Q_TILE:(h + 1) * Q_TILE]

        cur = (t0 + lax.broadcasted_iota(jnp.int32, (1, Q_TILE), 1)) >> (SEL_BLOCK.bit_length() - 1)
        valid_b = blk_row <= cur
        bonus = jnp.where(blk_row == 0, FORCE_BONUS,
                          jnp.where(blk_row == cur, FORCE_BONUS,
                                    jnp.where(blk_row == cur - 1, FORCE_BONUS, 0.0)))
        score = jnp.where(valid_b, imp + bonus, -1.0)
        before_tile = blk_row < c * (Q_TILE // SEL_BLOCK)
        sub_row = lax.broadcasted_iota(jnp.int32, (8, Q_TILE), 0)

        def keep_ranked(n_rows):
            n_rb = n_rows // 8
            sblk = [score[8 * r:8 * r + 8] for r in range(n_rb)]
            cnt = [jnp.zeros((8, Q_TILE), F32) for _ in range(n_rb)]
            for j in range(n_rows):
                rj = jnp.broadcast_to(score[j:j + 1], (8, Q_TILE))
                for r in range(n_rb):
                    if 8 * r + 7 <= j:
                        cnt[r] = cnt[r] + jnp.where(rj > sblk[r], 1.0, 0.0)
                    elif 8 * r > j:
                        cnt[r] = cnt[r] + jnp.where(rj >= sblk[r], 1.0, 0.0)
                    else:
                        cnt[r] = cnt[r] + jnp.where(sub_row + 8 * r > j,
                                                    jnp.where(rj >= sblk[r], 1.0, 0.0),
                                                    jnp.where(rj > sblk[r], 1.0, 0.0))
            in_top = jnp.concatenate(
                [jnp.where(cnt[r] < float(SEL_TOP_N), 0.0, NEG_INF) for r in range(n_rb)]
                + [jnp.full((n_blk - n_rows, Q_TILE), NEG_INF, F32)] * (n_rows < n_blk), axis=0)
            return jnp.where(before_tile, in_top, NEG_INF)

        if rank_rows == 0:
            keep = jnp.where(before_tile, 0.0, NEG_INF)
        else:
            keep = keep_ranked(rank_rows)
        qb_scr[c, KV_WIDTH:KV_WIDTH + n_blk, :] = jnp.concatenate(
            [keep] * HEADS_PER_GROUP, axis=1).astype(BF16)

        w0 = win_first_chunk(c)
        vw = jnp.concatenate([vwt_ref[0, w0 + u] for u in range(win_chunks)], axis=1)
        acc_w = jnp.dot(with_ones(vw), p1_scr[r_win:r_own], preferred_element_type=F32)
        acc_scr[c] = jnp.dot(with_ones(vst_ref[0, c]), p1_scr[r_own:r_end],
                             preferred_element_type=F32)
        o_scr[c] = gate_row(c, 0) * o_c + gate_row(c, 2) * normalized(acc_w)

    assert n_tiles >= 2
    tile_scores(0)
    tile_exponentials(0)
    tile_scores(1)

    blocks_per_tile = Q_TILE // SEL_BLOCK
    sizes = [0] + [r for r in (n_blk // 2,) if r % 8 == 0 and r > SEL_TOP_N] + [n_blk]
    limits = [SEL_TOP_N // blocks_per_tile] + [r // blocks_per_tile for r in sizes[1:]]
    start = 0
    for rank_rows, limit in zip(sizes, limits):
        stop = max(start, min(limit, n_tiles - 2))

        def body(c, rank_rows=rank_rows):
            tile_outputs(c, rank_rows)
            tile_exponentials(c + 1)
            tile_scores(c + 2)

        pl.loop(start, stop)(body)
        start = stop

    tile_outputs(n_tiles - 2, n_blk)
    tile_exponentials(n_tiles - 1)
    tile_outputs(n_tiles - 1, n_blk)

    chunks_per_step = SEL_KEYS // KEY_CHUNK
    n_sel_steps = T // SEL_KEYS
    n_groups = n_tiles // SEL_GROUP
    assert SEL_KEYS % (SEL_GROUP * Q_TILE) == 0 and n_tiles % SEL_GROUP == 0
    groups_per_step = SEL_KEYS // (SEL_GROUP * Q_TILE)
    n_items = sum(n_groups - groups_per_step * j for j in range(n_sel_steps))
    assert n_items >= 2

    def next_item(item):
        j, q = item
        wrap = q + 1 == n_groups
        return jnp.where(wrap, j + 1, j), jnp.where(wrap, groups_per_step * (j + 1), q + 1)

    def sel_scores(item):
        j, q = item
        k0 = pl.multiple_of(j * SEL_KEYS, SEL_KEYS)
        k_ext = jnp.concatenate([ks_ref[0, pl.ds(k0, SEL_KEYS), :], onehot_scr[pl.ds(k0, SEL_KEYS), :]],
                                axis=1)
        mxs = []
        for i in range(SEL_GROUP):
            sc = jnp.dot(k_ext, qb_scr[q * SEL_GROUP + i], preferred_element_type=F32)
            s_scr[i] = sc
            mxs.append(jnp.max(sc, axis=0, keepdims=True))
        return tuple(mxs)

    def sel_exponentials(mxs):
        for i in range(SEL_GROUP):
            p_scr[i] = jnp.exp2(s_scr[i] - mxs[i]).astype(BF16)

    def sel_accumulate(item, mxs):
        j, q = item
        vs = with_ones(jnp.concatenate(
            [vst_ref[0, chunks_per_step * j + u] for u in range(chunks_per_step)], axis=1))
        for i in range(SEL_GROUP):
            c = q * SEL_GROUP + i
            pv = jnp.dot(vs, p_scr[i], preferred_element_type=F32)
            m = m_scr[c]
            m_new = jnp.maximum(m, mxs[i])
            acc_scr[c] = jnp.exp2(m - m_new) * acc_scr[c] + jnp.exp2(mxs[i] - m_new) * pv
            m_scr[c] = m_new

    item_a = (jnp.int32(0), jnp.int32(0))
    item_b = next_item(item_a)
    mx_a = sel_scores(item_a)
    sel_exponentials(mx_a)
    mx_b = sel_scores(item_b)

    def sel_body(_, carry):
        item_a, item_b, mx_a, mx_b = carry
        item_c = next_item(item_b)
        sel_accumulate(item_a, mx_a)
        sel_exponentials(mx_b)
        return item_b, item_c, mx_b, sel_scores(item_c)

    item_a, item_b, mx_a, mx_b = lax.fori_loop(0, n_items - 2, sel_body, (item_a, item_b, mx_a, mx_b))
    sel_accumulate(item_a, mx_a)
    sel_exponentials(mx_b)
    sel_accumulate(item_b, mx_b)

    @pl.loop(0, n_tiles, unroll=OUT_UNROLL)
    def _(c):
        t0 = pl.multiple_of(c * Q_TILE, Q_TILE)
        o_t = o_scr[c] + gate_row(c, 1) * normalized(acc_scr[c])
        stacked = jnp.concatenate(
            [o_t[:, h * Q_TILE:(h + 1) * Q_TILE] for h in range(HEADS_PER_GROUP)], axis=0)
        o = stacked.T * zn_ref[0, pl.ds(t0, Q_TILE), :].astype(F32)
        o_ref[0, pl.ds(t0, Q_TILE), :] = o.astype(BF16)


def _attn_call(q, kcmp, vcmpt, ksw, vt, gt, zn):
    B, T, _ = q.shape
    n_c = kcmp.shape[1]
    n_ck = T // KEY_CHUNK
    n_tiles = T // Q_TILE
    G = NSA_KV_HEADS
    gw = HEADS_PER_GROUP * HEAD_DIM
    in_specs = [
        pl.BlockSpec((1, T, gw), lambda b, g: (b, 0, g)),
        pl.BlockSpec((1, n_c, KV_WIDTH), lambda b, g: (b, 0, 0)),
        pl.BlockSpec((1, HEAD_DIM, n_c), lambda b, g: (b, g, 0)),
        pl.BlockSpec((1, T, KV_WIDTH), lambda b, g: (b, 0, 0)),
        pl.BlockSpec((1, T, KV_WIDTH), lambda b, g: (b, 0, 1)),
        pl.BlockSpec((1, n_ck, HEAD_DIM, KEY_CHUNK), lambda b, g: (b, 0, g, 0)),
        pl.BlockSpec((1, n_ck, HEAD_DIM, KEY_CHUNK), lambda b, g: (b, 0, G + g, 0)),
        pl.BlockSpec((1, n_ck, GATE_ROWS, KEY_CHUNK), lambda b, g: (b, 0, g, 0)),
        pl.BlockSpec((1, T, gw), lambda b, g: (b, 0, g)),
    ]
    return pl.pallas_call(
        _attn_kernel,
        out_shape=jax.ShapeDtypeStruct((B, T, NSA_WIDTH), BF16),
        grid=(B, G),
        in_specs=in_specs,
        out_specs=pl.BlockSpec((1, T, gw), lambda b, g: (b, 0, g)),
        scratch_shapes=[pltpu.VMEM((4, KEY_CHUNK, LANES), F32),
                        pltpu.VMEM((T, MXU_DEPTH - KV_WIDTH), BF16),
                        pltpu.VMEM((n_tiles, MXU_DEPTH, LANES), BF16),
                        pltpu.VMEM((n_tiles, 1, LANES), F32),
                        pltpu.VMEM((n_tiles, HEAD_DIM + ONES_ROWS, LANES), F32),
                        pltpu.VMEM((n_tiles, HEAD_DIM, LANES), F32),
                        pltpu.VMEM((SEL_GROUP, SEL_KEYS, LANES), F32),
                        pltpu.VMEM((SEL_GROUP, SEL_KEYS, LANES), BF16),
                        pltpu.VMEM((n_c + WIN_KEYS + KEY_CHUNK, LANES), F32),
                        pltpu.VMEM((n_c + WIN_KEYS + KEY_CHUNK, LANES), BF16),
                        pltpu.VMEM((2, 1, LANES), F32)],
        name="attn",
        compiler_params=pltpu.CompilerParams(
            dimension_semantics=("arbitrary", "arbitrary"), vmem_limit_bytes=ATTN_VMEM_LIMIT),
    )(q, kcmp, vcmpt, ksw, ksw, vt, vt, gt, zn)


def _out_kernel(x_ref, mn_ref, mg_ref, wo_ref, g_ref, o_ref):
    h = x_ref[0]
    h = h + jnp.dot(mn_ref[0], wo_ref[0:NSA_WIDTH, :], preferred_element_type=F32)
    h = h + jnp.dot(mg_ref[0], wo_ref[NSA_WIDTH:, :], preferred_element_type=F32)
    ms = jnp.mean(h * h, axis=-1, keepdims=True)
    o_ref[0] = (h * lax.rsqrt(ms + NORM_EPS)) * g_ref[...]


def _out_call(x, mix_nsa, mix_gmlp, wo, g_f, tm):
    B, T, D = x.shape
    row = lambda w: pl.BlockSpec((1, tm, w), lambda b, i: (b, i, 0))
    return pl.pallas_call(
        _out_kernel,
        out_shape=jax.ShapeDtypeStruct((B, T, D), x.dtype),
        grid=(B, T // tm),
        in_specs=[row(D), row(NSA_WIDTH), row(GMLP_WIDTH),
                  pl.BlockSpec(wo.shape, lambda b, i: (0, 0)),
                  pl.BlockSpec((1, D), lambda b, i: (0, 0))],
        out_specs=row(D),
        name="out",
        compiler_params=pltpu.CompilerParams(
            dimension_semantics=("arbitrary", "arbitrary"), vmem_limit_bytes=VMEM_LIMIT),
    )(x, mix_nsa, mix_gmlp, wo, g_f)


def _split_w_in(w):
    sizes = (NSA_WIDTH, KV_WIDTH, KV_WIDTH, KV_WIDTH, KV_WIDTH, KV_WIDTH, KV_WIDTH, N_GATES,
             NSA_WIDTH, GMLP_WIDTH, GMLP_WIDTH, GMLP_WIDTH)
    parts, off = [], 0
    for s in sizes:
        parts.append(w[:, off:off + s])
        off += s
    return parts


def _gate_columns(w_gate):
    d = w_gate.shape[0]
    wg = w_gate.reshape(d, NSA_KV_HEADS, HEADS_PER_GROUP, 3).transpose(0, 1, 3, 2)
    wg = wg.reshape(d, NSA_KV_HEADS, 3 * HEADS_PER_GROUP)
    wg = jnp.pad(wg, ((0, 0), (0, 0), (0, GATE_ROWS - 3 * HEADS_PER_GROUP)))
    return wg.reshape(d, NSA_KV_HEADS * GATE_ROWS)


def _compress_w1(w1):
    half = CMP_BLOCK // 2
    w = w1.reshape(2, half, HEAD_DIM, CMP_HIDDEN)
    eye = jnp.eye(NSA_KV_HEADS, dtype=w1.dtype)
    full = jnp.einsum('hlde,gk->lgdhke', w, eye)
    return full.reshape(half * KV_WIDTH, 2 * NSA_KV_HEADS * CMP_HIDDEN)


def _compress_pe(pe):
    half = CMP_BLOCK // 2
    p = pe.reshape(2, half, 1, HEAD_DIM)
    p = jnp.broadcast_to(p, (2, half, NSA_KV_HEADS, HEAD_DIM))
    return p.reshape(2, half * KV_WIDTH)


def _block_diag2(w2):
    z = jnp.zeros_like(w2)
    return jnp.concatenate([jnp.concatenate([w2, z], axis=1),
                            jnp.concatenate([z, w2], axis=1)], axis=0)


def kernel(x, norm_in_g, w_in, w_cmp_k1, w_cmp_k2, pe_cmp_k, w_cmp_v1, w_cmp_v2, pe_cmp_v,
           gmlp_ln_g, gmlp_ln_b, w_spatial, b_spatial, w_out, norm_f_g):
    B, T, D = x.shape
    assert w_in.shape[0] == 1, "single-layer block"
    tm = min(PROJ_TILE, T)
    (wq, wkc, wvc, wks, wvs, wkw, wvw, wgt, wzn, wu, wv, wzg) = _split_w_in(w_in[0])
    wn = jnp.concatenate([wq, wkc, wvc, wks, wkw, wzn], axis=1).astype(BF16)
    wt = jnp.concatenate([wvs, wvw, _gate_columns(wgt), wu, wv, wzg], axis=1).T.astype(BF16)
    lng = gmlp_ln_g[0].reshape(GMLP_GROUPS, GMLP_GROUP_DIM, 1)
    lnb = gmlp_ln_b[0].reshape(GMLP_GROUPS, GMLP_GROUP_DIM, 1)
    ws_t = jnp.swapaxes(w_spatial[0], 1, 2)
    bs = b_spatial[0].reshape(GMLP_GROUPS, 1, GMLP_CHUNK)
    q, kc, vc, ksw, zn, vt, gt, o_gmlp = _proj_call(
        x, norm_in_g[0].reshape(1, D), wn, wt, lng, lnb, ws_t, bs, tm)

    n_c = T // CMP_STRIDE
    kcmp, vcmpt = _compress_call(
        kc, vc,
        _compress_pe(pe_cmp_k[0]), _compress_pe(pe_cmp_v[0]),
        _compress_w1(w_cmp_k1[0]).astype(BF16), _compress_w1(w_cmp_v1[0]).astype(BF16),
        _block_diag2(w_cmp_k2[0]).astype(BF16), _block_diag2(w_cmp_v2[0]).T.astype(BF16))

    o_nsa = _attn_call(q, kcmp, vcmpt, ksw, vt, gt, zn)
    return _out_call(x, o_nsa, o_gmlp, w_out[0].astype(BF16), norm_f_g.reshape(1, D),
                     min(OUT_TILE, T))
```

```python
import math

import jax
import jax.numpy as jnp
from jax import lax
from jax.experimental import pallas as pl
from jax.experimental.pallas import tpu as pltpu

F32 = jnp.float32
BF16 = jnp.bfloat16

HEAD_DIM = 64
NSA_HEADS = 8
NSA_KV_HEADS = 2
HEADS_PER_GROUP = NSA_HEADS // NSA_KV_HEADS
NSA_WIDTH = NSA_HEADS * HEAD_DIM
KV_WIDTH = NSA_KV_HEADS * HEAD_DIM
N_GATES = 3 * NSA_HEADS
GMLP_GROUPS = 8
GMLP_GROUP_DIM = 64
GMLP_WIDTH = GMLP_GROUPS * GMLP_GROUP_DIM
GMLP_CHUNK = 128
CMP_BLOCK = 32
CMP_STRIDE = 16
CMP_HIDDEN = 128
SEL_BLOCK = 64
SEL_TOP_N = 16
WINDOW = 512
NORM_EPS = 1e-6
NEG_INF = -1e30
FORCE_BONUS = 1e4

Q_TILE = 128
KEY_CHUNK = 128
LANES = HEADS_PER_GROUP * Q_TILE
SEL_KEYS = 512
SEL_GROUP = 4
OUT_UNROLL = 4
WIN_KEYS = WINDOW + Q_TILE
ONES_ROWS = 16
LOG2_E = math.log2(math.e)
GATE_ROWS = 16
OUT_TILE = 1024
PROJ_TILE = 1024
MXU_DEPTH = 256
V7X_VMEM_BYTES = 64 * 1024 * 1024
VMEM_LIMIT = V7X_VMEM_BYTES * 3 // 4
ATTN_VMEM_LIMIT = V7X_VMEM_BYTES * 7 // 8


def _sigmoid(x):
    return 1.0 / (1.0 + jnp.exp(-x))


def _gelu_tanh(x):
    c = math.sqrt(2.0 / math.pi)
    return x * (0.5 * (1.0 + jnp.tanh(c * (x + 0.044715 * (x * x * x)))))


def _proj_kernel(x_ref, g_ref, wn_ref, wt_ref, lng_ref, lnb_ref, ws_ref, bs_ref,
                 q_ref, kc_ref, vc_ref, ksw_ref, zn_ref, vt_ref, gt_ref, og_ref, cmp_scr):
    tm = x_ref.shape[1]
    n_ck = tm // GMLP_CHUNK
    x = x_ref[0]
    ms = jnp.mean(x * x, axis=-1, keepdims=True)
    hn = ((x * lax.rsqrt(ms + NORM_EPS)) * g_ref[...]).astype(BF16)

    def proj_t(lo, hi):
        return lax.dot_general(wt_ref[lo:hi, :], hn, (((1,), (1,)), ((), ())),
                               preferred_element_type=F32)

    def proj_n(lo, hi):
        return jnp.dot(hn, wn_ref[:, lo:hi], preferred_element_type=F32)

    vg = proj_t(0, 288)
    vt = vg[0:256].astype(BF16)
    gt = _sigmoid(vg[256:288])
    for ck in range(n_ck):
        sl = slice(ck * GMLP_CHUNK, (ck + 1) * GMLP_CHUNK)
        vt_ref[0, ck] = vt[:, sl]
        gt_ref[0, ck] = gt[:, sl]
    gu = _gelu_tanh(proj_t(288, 800))
    gv = _gelu_tanh(proj_t(800, 1312)).reshape(GMLP_GROUPS, GMLP_GROUP_DIM, tm)
    zg = proj_t(1312, 1824)
    mu = jnp.mean(gv, axis=1, keepdims=True)
    dv = gv - mu
    var = jnp.mean(dv * dv, axis=1, keepdims=True)
    vn = (dv * lax.rsqrt(var + NORM_EPS)) * lng_ref[...] + lnb_ref[...]
    s_i = lax.broadcasted_iota(jnp.int32, (GMLP_CHUNK, GMLP_CHUNK), 0)
    t_i = lax.broadcasted_iota(jnp.int32, (GMLP_CHUNK, GMLP_CHUNK), 1)
    causal = s_i <= t_i
    outs = []
    for g in range(GMLP_GROUPS):
        a = vn[g].astype(BF16)
        a_st = jnp.concatenate(
            [a[:, ck * GMLP_CHUNK:(ck + 1) * GMLP_CHUNK] for ck in range(n_ck)], axis=0)
        w = jnp.where(causal, ws_ref[g], 0.0).astype(BF16)
        r = jnp.dot(a_st, w, preferred_element_type=F32) + bs_ref[g]
        outs.append(jnp.concatenate(
            [r[ck * GMLP_GROUP_DIM:(ck + 1) * GMLP_GROUP_DIM] for ck in range(n_ck)], axis=1))
    mixed = jnp.concatenate(outs, axis=0)
    og_t = (gu * mixed) * (zg * _sigmoid(zg))
    og_ref[0] = og_t.T.astype(BF16)
    q_ref[0] = (proj_n(0, 512) * (HEAD_DIM ** -0.5 * LOG2_E)).astype(BF16)
    kv = proj_n(512, 1024)
    for src, dst_ref, lo in ((0, kc_ref, 0), (1, vc_ref, KV_WIDTH)):
        cmp_scr[src] = kv[:, lo:lo + KV_WIDTH]
        for pos in range(CMP_STRIDE):
            dst_ref[0, :, pos * KV_WIDTH:(pos + 1) * KV_WIDTH] = cmp_scr[
                src, pl.ds(pos, tm // CMP_STRIDE, stride=CMP_STRIDE), :].astype(BF16)
    ksw_ref[0] = kv[:, 2 * KV_WIDTH:4 * KV_WIDTH].astype(BF16)
    z = proj_n(1024, 1536)
    zn_ref[0] = (z * _sigmoid(z)).astype(BF16)


def _proj_call(x, g_in, wn, wt, lng, lnb, ws_t, bs, tm):
    B, T, D = x.shape
    n_ck = tm // GMLP_CHUNK
    grid = (B, T // tm)
    const = lambda *shape: pl.BlockSpec(shape, lambda b, i: (0,) * len(shape))
    row = lambda w: pl.BlockSpec((1, tm, w), lambda b, i: (b, i, 0))
    out_shape = (
        jax.ShapeDtypeStruct((B, T, NSA_WIDTH), BF16),
        jax.ShapeDtypeStruct((B, T // CMP_STRIDE, CMP_STRIDE * KV_WIDTH), BF16),
        jax.ShapeDtypeStruct((B, T // CMP_STRIDE, CMP_STRIDE * KV_WIDTH), BF16),
        jax.ShapeDtypeStruct((B, T, 2 * KV_WIDTH), BF16),
        jax.ShapeDtypeStruct((B, T, NSA_WIDTH), BF16),
        jax.ShapeDtypeStruct((B, T // GMLP_CHUNK, 256, GMLP_CHUNK), BF16),
        jax.ShapeDtypeStruct((B, T // GMLP_CHUNK, 2 * GATE_ROWS, GMLP_CHUNK), F32),
        jax.ShapeDtypeStruct((B, T, GMLP_WIDTH), BF16),
    )
    out_specs = (
        row(NSA_WIDTH),
        pl.BlockSpec((1, tm // CMP_STRIDE, CMP_STRIDE * KV_WIDTH), lambda b, i: (b, i, 0)),
        pl.BlockSpec((1, tm // CMP_STRIDE, CMP_STRIDE * KV_WIDTH), lambda b, i: (b, i, 0)),
        row(2 * KV_WIDTH), row(NSA_WIDTH),
        pl.BlockSpec((1, n_ck, 256, GMLP_CHUNK), lambda b, i: (b, i, 0, 0)),
        pl.BlockSpec((1, n_ck, 2 * GATE_ROWS, GMLP_CHUNK), lambda b, i: (b, i, 0, 0)),
        row(GMLP_WIDTH),
    )
    in_specs = [
        pl.BlockSpec((1, tm, D), lambda b, i: (b, i, 0)),
        const(1, D), const(*wn.shape), const(*wt.shape),
        const(*lng.shape), const(*lnb.shape), const(*ws_t.shape), const(*bs.shape),
    ]
    return pl.pallas_call(
        _proj_kernel, out_shape=out_shape, grid=grid, in_specs=in_specs, out_specs=out_specs,
        scratch_shapes=[pltpu.VMEM((2, tm, KV_WIDTH), F32)],
        name="proj",
        compiler_params=pltpu.CompilerParams(
            dimension_semantics=("arbitrary", "arbitrary"), vmem_limit_bytes=VMEM_LIMIT),
    )(x, g_in, wn, wt, lng, lnb, ws_t, bs)


def _compress_kernel(kc_ref, vc_ref, pek_ref, pev_ref, wk1_ref, wv1_ref, wk2_ref, wv2t_ref,
                     kcmp_ref, vcmpt_ref):
    n_c = kc_ref.shape[1]
    half = 2 * CMP_HIDDEN

    def hidden(src_ref, pe_ref, w1_ref):
        xs = src_ref[0].astype(F32)
        xa = (xs + pe_ref[0:1, :]).astype(BF16)
        xb = (xs + pe_ref[1:2, :]).astype(BF16)
        ha = jnp.dot(xa, w1_ref[:, 0:half], preferred_element_type=F32)
        hb = jnp.dot(xb, w1_ref[:, half:2 * half], preferred_element_type=F32)
        hb_next = pltpu.roll(hb, shift=n_c - 1, axis=0)
        return _gelu_tanh(ha + hb_next).astype(BF16)

    hk = hidden(kc_ref, pek_ref, wk1_ref)
    kcmp_ref[0] = jnp.dot(hk, wk2_ref[...], preferred_element_type=F32).astype(BF16)
    hv = hidden(vc_ref, pev_ref, wv1_ref)
    vcmpt_ref[0] = lax.dot_general(wv2t_ref[...], hv, (((1,), (1,)), ((), ())),
                                   preferred_element_type=F32).astype(BF16)


def _compress_call(kc2, vc2, pek, pev, wk1, wv1, wk2, wv2t):
    B, n_c, W = kc2.shape
    const = lambda a: pl.BlockSpec(a.shape, lambda b: (0,) * a.ndim)
    return pl.pallas_call(
        _compress_kernel,
        out_shape=(jax.ShapeDtypeStruct((B, n_c, KV_WIDTH), BF16),
                   jax.ShapeDtypeStruct((B, KV_WIDTH, n_c), BF16)),
        grid=(B,),
        in_specs=[pl.BlockSpec((1, n_c, W), lambda b: (b, 0, 0)),
                  pl.BlockSpec((1, n_c, W), lambda b: (b, 0, 0)),
                  const(pek), const(pev), const(wk1), const(wv1), const(wk2), const(wv2t)],
        out_specs=(pl.BlockSpec((1, n_c, KV_WIDTH), lambda b: (b, 0, 0)),
                   pl.BlockSpec((1, KV_WIDTH, n_c), lambda b: (b, 0, 0))),
        name="compress",
        compiler_params=pltpu.CompilerParams(
            dimension_semantics=("arbitrary",), vmem_limit_bytes=VMEM_LIMIT),
    )(kc2, vc2, pek, pev, wk1, wv1, wk2, wv2t)


def _attn_kernel(q_ref, kcmp_ref, vcmpt_ref, ks_ref, kw_ref, vst_ref, vwt_ref, gt_ref, zn_ref,
                 o_ref, mask_scr, onehot_scr, qb_scr, m_scr, acc_scr, o_scr, s_scr, p_scr,
                 s1_scr, p1_scr, m1_scr):
    g = pl.program_id(1)
    T = q_ref.shape[1]
    n_c = kcmp_ref.shape[1]
    n_blk = T // SEL_BLOCK
    n_tiles = T // Q_TILE
    assert KV_WIDTH + n_blk <= MXU_DEPTH

    def with_ones(vt):
        return jnp.concatenate([vt, jnp.ones((ONES_ROWS, vt.shape[1]), vt.dtype)], axis=0)

    def normalized(acc):
        return acc[0:HEAD_DIM] / acc[HEAD_DIM:HEAD_DIM + 1]

    def gate_row(c, br):
        gt = gt_ref[0, c]
        return jnp.concatenate(
            [gt[4 * br + h:4 * br + h + 1] for h in range(HEADS_PER_GROUP)], axis=1)

    lane_q = lax.broadcasted_iota(jnp.int32, (1, LANES), 1) & (Q_TILE - 1)
    MASK_CAUSAL, MASK_WINDOW, MASK_NONE, MASK_ALL = range(4)

    @pl.when((pl.program_id(0) == 0) & (g == 0))
    def _():
        key_i = lax.broadcasted_iota(jnp.int32, (T, MXU_DEPTH - KV_WIDTH), 0)
        blk_i = lax.broadcasted_iota(jnp.int32, (T, MXU_DEPTH - KV_WIDTH), 1)
        onehot_scr[...] = jnp.where((key_i >> (SEL_BLOCK.bit_length() - 1)) == blk_i,
                                    1.0, 0.0).astype(BF16)
        if KV_WIDTH + n_blk < MXU_DEPTH:
            qb_scr[:, KV_WIDTH + n_blk:, :] = jnp.zeros(
                (n_tiles, MXU_DEPTH - KV_WIDTH - n_blk, LANES), BF16)
        key_row = lax.broadcasted_iota(jnp.int32, (KEY_CHUNK, LANES), 0)
        mask_scr[MASK_CAUSAL] = jnp.where(key_row <= lane_q, 0.0, NEG_INF)
        mask_scr[MASK_WINDOW] = jnp.where(key_row > lane_q, 0.0, NEG_INF)
        mask_scr[MASK_NONE] = jnp.zeros((KEY_CHUNK, LANES), F32)
        mask_scr[MASK_ALL] = jnp.full((KEY_CHUNK, LANES), NEG_INF, F32)

    kcmp = kcmp_ref[0]
    vct1 = with_ones(vcmpt_ref[0])
    jb_c = lax.broadcasted_iota(jnp.int32, (n_blk, n_c), 0)
    ic_c = lax.broadcasted_iota(jnp.int32, (n_blk, n_c), 1)
    overlap_t = jnp.where(ic_c * CMP_STRIDE < (jb_c + 1) * SEL_BLOCK,
                          jnp.where(ic_c * CMP_STRIDE + (CMP_BLOCK - 1) >= jb_c * SEL_BLOCK, 1.0, 0.0),
                          0.0).astype(BF16)
    row_group = lax.broadcasted_iota(jnp.int32, (KV_WIDTH, LANES), 0) >> (HEAD_DIM.bit_length() - 1)
    cmp_end = lax.broadcasted_iota(jnp.int32, (n_c, LANES), 0) * CMP_STRIDE + (CMP_BLOCK - 1)
    blk_row = lax.broadcasted_iota(jnp.int32, (n_blk, Q_TILE), 0)

    win_chunks = WIN_KEYS // KEY_CHUNK
    r_win, r_own, r_end = n_c, n_c + WIN_KEYS, n_c + WIN_KEYS + KEY_CHUNK

    def win_first_chunk(c):
        return jnp.maximum(c - WINDOW // KEY_CHUNK, 0)

    def tile_scores(c):
        t0 = c * Q_TILE if isinstance(c, int) else pl.multiple_of(c * Q_TILE, Q_TILE)
        qt = q_ref[0, pl.ds(t0, Q_TILE), :].astype(F32).T
        b64 = jnp.concatenate(
            [qt[h * HEAD_DIM:(h + 1) * HEAD_DIM] for h in range(HEADS_PER_GROUP)], axis=1)
        qmat = jnp.where(row_group == g, jnp.concatenate([b64, b64], axis=0), 0.0).astype(BF16)
        qb_scr[c, 0:KV_WIDTH, :] = qmat
        tq = t0 + lane_q
        s = jnp.where(cmp_end <= tq, jnp.dot(kcmp, qmat, preferred_element_type=F32), NEG_INF)
        s1_scr[0:r_win] = s
        m1_scr[0] = jnp.max(s, axis=0, keepdims=True)
        w0 = win_first_chunk(c)
        wk0 = pl.multiple_of(w0 * KEY_CHUNK, KEY_CHUNK)
        sw = jnp.dot(kw_ref[0, pl.ds(wk0, WIN_KEYS), :], qmat, preferred_element_type=F32)
        m_w = None
        for u in range(win_chunks):
            ck = w0 + u
            kind = jnp.where(ck == c, MASK_CAUSAL,
                             jnp.where(ck > c, MASK_ALL,
                                       jnp.where(ck == c - WINDOW // KEY_CHUNK, MASK_WINDOW, MASK_NONE)))
            su = sw[u * KEY_CHUNK:(u + 1) * KEY_CHUNK] + mask_scr[kind]
            s1_scr[r_win + u * KEY_CHUNK:r_win + (u + 1) * KEY_CHUNK] = su
            mu = jnp.max(su, axis=0, keepdims=True)
            m_w = mu if m_w is None else jnp.maximum(m_w, mu)
        m1_scr[1] = m_w
        sd = (jnp.dot(ks_ref[0, pl.ds(t0, KEY_CHUNK), :], qmat, preferred_element_type=F32)
              + mask_scr[MASK_CAUSAL])
        s1_scr[r_own:r_end] = sd
        m_scr[c] = jnp.max(sd, axis=0, keepdims=True)

    def tile_exponentials(c):
        p1_scr[0:r_win] = jnp.exp2(s1_scr[0:r_win] - m1_scr[0]).astype(BF16)
        p1_scr[r_win:r_own] = jnp.exp2(s1_scr[r_win:r_own] - m1_scr[1]).astype(BF16)
        p1_scr[r_own:r_end] = jnp.exp2(s1_scr[r_own:r_end] - m_scr[c]).astype(BF16)

    def tile_outputs(c, rank_rows):
        t0 = c * Q_TILE
        tq = t0 + lane_q
        e16 = p1_scr[0:r_win]
        acc_c = jnp.dot(vct1, e16, preferred_element_type=F32)
        linv_c = jnp.where(tq >= CMP_BLOCK - 1, 1.0 / acc_c[HEAD_DIM:HEAD_DIM + 1], 0.0)
        o_c = acc_c[0:HEAD_DIM] * linv_c
        imp4 = jnp.dot(overlap_t, e16, preferred_element_type=F32) * linv_c
        imp = imp4[:, 0:Q_TILE]
        for h in range(1, HEADS_PER_GROUP):
            imp = imp + imp4[:, h * Q_TILE:(h + 1) * Q_TILE]

        cur = (t0 + lax.broadcasted_iota(jnp.int32, (1, Q_TILE), 1)) >> (SEL_BLOCK.bit_length() - 1)
        valid_b = blk_row <= cur
        bonus = jnp.where(blk_row == 0, FORCE_BONUS,
                          jnp.where(blk_row == cur, FORCE_BONUS,
                                    jnp.where(blk_row == cur - 1, FORCE_BONUS, 0.0)))
        score = jnp.where(valid_b, imp + bonus, -1.0)
        before_tile = blk_row < c * (Q_TILE // SEL_BLOCK)
        sub_row = lax.broadcasted_iota(jnp.int32, (8, Q_TILE), 0)

        def keep_ranked(n_rows):
            n_rb = n_rows // 8
            sblk = [score[8 * r:8 * r + 8] for r in range(n_rb)]
            cnt = [jnp.zeros((8, Q_TILE), F32) for _ in range(n_rb)]
            for j in range(n_rows):
                rj = jnp.broadcast_to(score[j:j + 1], (8, Q_TILE))
                for r in range(n_rb):
                    if 8 * r + 7 <= j:
                        cnt[r] = cnt[r] + jnp.where(rj > sblk[r], 1.0, 0.0)
                    elif 8 * r > j:
                        cnt[r] = cnt[r] + jnp.where(rj >= sblk[r], 1.0, 0.0)
                    else:
                        cnt[r] = cnt[r] + jnp.where(sub_row + 8 * r > j,
                                                    jnp.where(rj >= sblk[r], 1.0, 0.0),
                                                    jnp.where(rj > sblk[r], 1.0, 0.0))
            in_top = jnp.concatenate(
                [jnp.where(cnt[r] < float(SEL_TOP_N), 0.0, NEG_INF) for r in range(n_rb)]
                + [jnp.full((n_blk - n_rows, Q_TILE), NEG_INF, F32)] * (n_rows < n_blk), axis=0)
            return jnp.where(before_tile, in_top, NEG_INF)

        if rank_rows == 0:
            keep = jnp.where(before_tile, 0.0, NEG_INF)
        else:
            keep = keep_ranked(rank_rows)
        qb_scr[c, KV_WIDTH:KV_WIDTH + n_blk, :] = jnp.concatenate(
            [keep] * HEADS_PER_GROUP, axis=1).astype(BF16)

        w0 = win_first_chunk(c)
        vw = jnp.concatenate([vwt_ref[0, w0 + u] for u in range(win_chunks)], axis=1)
        acc_w = jnp.dot(with_ones(vw), p1_scr[r_win:r_own], preferred_element_type=F32)
        acc_scr[c] = jnp.dot(with_ones(vst_ref[0, c]), p1_scr[r_own:r_end],
                             preferred_element_type=F32)
        o_scr[c] = gate_row(c, 0) * o_c + gate_row(c, 2) * normalized(acc_w)

    assert n_tiles >= 2
    tile_scores(0)
    tile_exponentials(0)
    tile_scores(1)

    blocks_per_tile = Q_TILE // SEL_BLOCK
    sizes = [0] + [r for r in (n_blk // 2, 3 * n_blk // 4) if r % 8 == 0 and r > SEL_TOP_N] + [n_blk]
    limits = [SEL_TOP_N // blocks_per_tile] + [r // blocks_per_tile for r in sizes[1:]]
    start = 0
    for rank_rows, limit in zip(sizes, limits):
        stop = max(start, min(limit, n_tiles - 2))

        def body(c, rank_rows=rank_rows):
            tile_outputs(c, rank_rows)
            tile_exponentials(c + 1)
            tile_scores(c + 2)

        pl.loop(start, stop)(body)
        start = stop

    tile_outputs(n_tiles - 2, n_blk)
    tile_exponentials(n_tiles - 1)
    tile_outputs(n_tiles - 1, n_blk)

    chunks_per_step = SEL_KEYS // KEY_CHUNK
    n_sel_steps = T // SEL_KEYS
    n_groups = n_tiles // SEL_GROUP
    assert SEL_KEYS % (SEL_GROUP * Q_TILE) == 0 and n_tiles % SEL_GROUP == 0
    groups_per_step = SEL_KEYS // (SEL_GROUP * Q_TILE)
    n_items = sum(n_groups - groups_per_step * j for j in range(n_sel_steps))
    assert n_items >= 2

    def next_item(item):
        j, q = item
        wrap = q + 1 == n_groups
        return jnp.where(wrap, j + 1, j), jnp.where(wrap, groups_per_step * (j + 1), q + 1)

    def sel_scores(item):
        j, q = item
        k0 = pl.multiple_of(j * SEL_KEYS, SEL_KEYS)
        k_ext = jnp.concatenate([ks_ref[0, pl.ds(k0, SEL_KEYS), :], onehot_scr[pl.ds(k0, SEL_KEYS), :]],
                                axis=1)
        mxs = []
        for i in range(SEL_GROUP):
            sc = jnp.dot(k_ext, qb_scr[q * SEL_GROUP + i], preferred_element_type=F32)
            s_scr[i] = sc
            mxs.append(jnp.max(sc, axis=0, keepdims=True))
        return tuple(mxs)

    def sel_exponentials(mxs):
        for i in range(SEL_GROUP):
            p_scr[i] = jnp.exp2(s_scr[i] - mxs[i]).astype(BF16)

    def sel_accumulate(item, mxs):
        j, q = item
        vs = with_ones(jnp.concatenate(
            [vst_ref[0, chunks_per_step * j + u] for u in range(chunks_per_step)], axis=1))
        for i in range(SEL_GROUP):
            c = q * SEL_GROUP + i
            pv = jnp.dot(vs, p_scr[i], preferred_element_type=F32)
            m = m_scr[c]
            m_new = jnp.maximum(m, mxs[i])
            acc_scr[c] = jnp.exp2(m - m_new) * acc_scr[c] + jnp.exp2(mxs[i] - m_new) * pv
            m_scr[c] = m_new

    item_a = (jnp.int32(0), jnp.int32(0))
    item_b = next_item(item_a)
    mx_a = sel_scores(item_a)
    sel_exponentials(mx_a)
    mx_b = sel_scores(item_b)

    def sel_body(_, carry):
        item_a, item_b, mx_a, mx_b = carry
        item_c = next_item(item_b)
        sel_accumulate(item_a, mx_a)
        sel_exponentials(mx_b)
        return item_b, item_c, mx_b, sel_scores(item_c)

    item_a, item_b, mx_a, mx_b = lax.fori_loop(0, n_items - 2, sel_body, (item_a, item_b, mx_a, mx_b))
    sel_accumulate(item_a, mx_a)
    sel_exponentials(mx_b)
    sel_accumulate(item_b, mx_b)

    @pl.loop(0, n_tiles, unroll=OUT_UNROLL)
    def _(c):
        t0 = pl.multiple_of(c * Q_TILE, Q_TILE)
        o_t = o_scr[c] + gate_row(c, 1) * normalized(acc_scr[c])
        stacked = jnp.concatenate(
            [o_t[:, h * Q_TILE:(h + 1) * Q_TILE] for h in range(HEADS_PER_GROUP)], axis=0)
        o = stacked.T * zn_ref[0, pl.ds(t0, Q_TILE), :].astype(F32)
        o_ref[0, pl.ds(t0, Q_TILE), :] = o.astype(BF16)


def _attn_call(q, kcmp, vcmpt, ksw, vt, gt, zn):
    B, T, _ = q.shape
    n_c = kcmp.shape[1]
    n_ck = T // KEY_CHUNK
    n_tiles = T // Q_TILE
    G = NSA_KV_HEADS
    gw = HEADS_PER_GROUP * HEAD_DIM
    in_specs = [
        pl.BlockSpec((1, T, gw), lambda b, g: (b, 0, g)),
        pl.BlockSpec((1, n_c, KV_WIDTH), lambda b, g: (b, 0, 0)),
        pl.BlockSpec((1, HEAD_DIM, n_c), lambda b, g: (b, g, 0)),
        pl.BlockSpec((1, T, KV_WIDTH), lambda b, g: (b, 0, 0)),
        pl.BlockSpec((1, T, KV_WIDTH), lambda b, g: (b, 0, 1)),
        pl.BlockSpec((1, n_ck, HEAD_DIM, KEY_CHUNK), lambda b, g: (b, 0, g, 0)),
        pl.BlockSpec((1, n_ck, HEAD_DIM, KEY_CHUNK), lambda b, g: (b, 0, G + g, 0)),
        pl.BlockSpec((1, n_ck, GATE_ROWS, KEY_CHUNK), lambda b, g: (b, 0, g, 0)),
        pl.BlockSpec((1, T, gw), lambda b, g: (b, 0, g)),
    ]
    return pl.pallas_call(
        _attn_kernel,
        out_shape=jax.ShapeDtypeStruct((B, T, NSA_WIDTH), BF16),
        grid=(B, G),
        in_specs=in_specs,
        out_specs=pl.BlockSpec((1, T, gw), lambda b, g: (b, 0, g)),
        scratch_shapes=[pltpu.VMEM((4, KEY_CHUNK, LANES), F32),
                        pltpu.VMEM((T, MXU_DEPTH - KV_WIDTH), BF16),
                        pltpu.VMEM((n_tiles, MXU_DEPTH, LANES), BF16),
                        pltpu.VMEM((n_tiles, 1, LANES), F32),
                        pltpu.VMEM((n_tiles, HEAD_DIM + ONES_ROWS, LANES), F32),
                        pltpu.VMEM((n_tiles, HEAD_DIM, LANES), F32),
                        pltpu.VMEM((SEL_GROUP, SEL_KEYS, LANES), F32),
                        pltpu.VMEM((SEL_GROUP, SEL_KEYS, LANES), BF16),
                        pltpu.VMEM((n_c + WIN_KEYS + KEY_CHUNK, LANES), F32),
                        pltpu.VMEM((n_c + WIN_KEYS + KEY_CHUNK, LANES), BF16),
                        pltpu.VMEM((2, 1, LANES), F32)],
        name="attn",
        compiler_params=pltpu.CompilerParams(
            dimension_semantics=("arbitrary", "arbitrary"), vmem_limit_bytes=ATTN_VMEM_LIMIT),
    )(q, kcmp, vcmpt, ksw, ksw, vt, vt, gt, zn)


def _out_kernel(x_ref, mn_ref, mg_ref, wo_ref, g_ref, o_ref):
    h = x_ref[0]
    h = h + jnp.dot(mn_ref[0], wo_ref[0:NSA_WIDTH, :], preferred_element_type=F32)
    h = h + jnp.dot(mg_ref[0], wo_ref[NSA_WIDTH:, :], preferred_element_type=F32)
    ms = jnp.mean(h * h, axis=-1, keepdims=True)
    o_ref[0] = (h * lax.rsqrt(ms + NORM_EPS)) * g_ref[...]


def _out_call(x, mix_nsa, mix_gmlp, wo, g_f, tm):
    B, T, D = x.shape
    row = lambda w: pl.BlockSpec((1, tm, w), lambda b, i: (b, i, 0))
    return pl.pallas_call(
        _out_kernel,
        out_shape=jax.ShapeDtypeStruct((B, T, D), x.dtype),
        grid=(B, T // tm),
        in_specs=[row(D), row(NSA_WIDTH), row(GMLP_WIDTH),
                  pl.BlockSpec(wo.shape, lambda b, i: (0, 0)),
                  pl.BlockSpec((1, D), lambda b, i: (0, 0))],
        out_specs=row(D),
        name="out",
        compiler_params=pltpu.CompilerParams(
            dimension_semantics=("arbitrary", "arbitrary"), vmem_limit_bytes=VMEM_LIMIT),
    )(x, mix_nsa, mix_gmlp, wo, g_f)


def _split_w_in(w):
    sizes = (NSA_WIDTH, KV_WIDTH, KV_WIDTH, KV_WIDTH, KV_WIDTH, KV_WIDTH, KV_WIDTH, N_GATES,
             NSA_WIDTH, GMLP_WIDTH, GMLP_WIDTH, GMLP_WIDTH)
    parts, off = [], 0
    for s in sizes:
        parts.append(w[:, off:off + s])
        off += s
    return parts


def _gate_columns(w_gate):
    d = w_gate.shape[0]
    wg = w_gate.reshape(d, NSA_KV_HEADS, HEADS_PER_GROUP, 3).transpose(0, 1, 3, 2)
    wg = wg.reshape(d, NSA_KV_HEADS, 3 * HEADS_PER_GROUP)
    wg = jnp.pad(wg, ((0, 0), (0, 0), (0, GATE_ROWS - 3 * HEADS_PER_GROUP)))
    return wg.reshape(d, NSA_KV_HEADS * GATE_ROWS)


def _compress_w1(w1):
    half = CMP_BLOCK // 2
    w = w1.reshape(2, half, HEAD_DIM, CMP_HIDDEN)
    eye = jnp.eye(NSA_KV_HEADS, dtype=w1.dtype)
    full = jnp.einsum('hlde,gk->lgdhke', w, eye)
    return full.reshape(half * KV_WIDTH, 2 * NSA_KV_HEADS * CMP_HIDDEN)


def _compress_pe(pe):
    half = CMP_BLOCK // 2
    p = pe.reshape(2, half, 1, HEAD_DIM)
    p = jnp.broadcast_to(p, (2, half, NSA_KV_HEADS, HEAD_DIM))
    return p.reshape(2, half * KV_WIDTH)


def _block_diag2(w2):
    z = jnp.zeros_like(w2)
    return jnp.concatenate([jnp.concatenate([w2, z], axis=1),
                            jnp.concatenate([z, w2], axis=1)], axis=0)


def kernel(x, norm_in_g, w_in, w_cmp_k1, w_cmp_k2, pe_cmp_k, w_cmp_v1, w_cmp_v2, pe_cmp_v,
           gmlp_ln_g, gmlp_ln_b, w_spatial, b_spatial, w_out, norm_f_g):
    B, T, D = x.shape
    assert w_in.shape[0] == 1, "single-layer block"
    tm = min(PROJ_TILE, T)
    (wq, wkc, wvc, wks, wvs, wkw, wvw, wgt, wzn, wu, wv, wzg) = _split_w_in(w_in[0])
    wn = jnp.concatenate([wq, wkc, wvc, wks, wkw, wzn], axis=1).astype(BF16)
    wt = jnp.concatenate([wvs, wvw, _gate_columns(wgt), wu, wv, wzg], axis=1).T.astype(BF16)
    lng = gmlp_ln_g[0].reshape(GMLP_GROUPS, GMLP_GROUP_DIM, 1)
    lnb = gmlp_ln_b[0].reshape(GMLP_GROUPS, GMLP_GROUP_DIM, 1)
    ws_t = jnp.swapaxes(w_spatial[0], 1, 2)
    bs = b_spatial[0].reshape(GMLP_GROUPS, 1, GMLP_CHUNK)
    q, kc, vc, ksw, zn, vt, gt, o_gmlp = _proj_call(
        x, norm_in_g[0].reshape(1, D), wn, wt, lng, lnb, ws_t, bs, tm)

    n_c = T // CMP_STRIDE
    kcmp, vcmpt = _compress_call(
        kc, vc,
        _compress_pe(pe_cmp_k[0]), _compress_pe(pe_cmp_v[0]),
        _compress_w1(w_cmp_k1[0]).astype(BF16), _compress_w1(w_cmp_v1[0]).astype(BF16),
        _block_diag2(w_cmp_k2[0]).astype(BF16), _block_diag2(w_cmp_v2[0]).T.astype(BF16))

    o_nsa = _attn_call(q, kcmp, vcmpt, ksw, vt, gt, zn)
    return _out_call(x, o_nsa, o_gmlp, w_out[0].astype(BF16), norm_f_g.reshape(1, D),
                     min(OUT_TILE, T))
```

```python
import math

import jax
import jax.numpy as jnp
from jax import lax
from jax.experimental import pallas as pl
from jax.experimental.pallas import tpu as pltpu

F32 = jnp.float32
BF16 = jnp.bfloat16

HEAD_DIM = 64
NSA_HEADS = 8
NSA_KV_HEADS = 2
HEADS_PER_GROUP = NSA_HEADS // NSA_KV_HEADS
NSA_WIDTH = NSA_HEADS * HEAD_DIM
KV_WIDTH = NSA_KV_HEADS * HEAD_DIM
N_GATES = 3 * NSA_HEADS
GMLP_GROUPS = 8
GMLP_GROUP_DIM = 64
GMLP_WIDTH = GMLP_GROUPS * GMLP_GROUP_DIM
GMLP_CHUNK = 128
CMP_BLOCK = 32
CMP_STRIDE = 16
CMP_HIDDEN = 128
SEL_BLOCK = 64
SEL_TOP_N = 16
WINDOW = 512
NORM_EPS = 1e-6
NEG_INF = -1e30
FORCE_BONUS = 1e4

Q_TILE = 128
KEY_CHUNK = 128
LANES = HEADS_PER_GROUP * Q_TILE
SEL_KEYS = 512
SEL_GROUP = 4
OUT_UNROLL = 4
WIN_KEYS = WINDOW + Q_TILE
ONES_ROWS = 16
LOG2_E = math.log2(math.e)
GATE_ROWS = 16
OUT_TILE = 1024
PROJ_TILE = 1024
MXU_DEPTH = 256
V7X_VMEM_BYTES = 64 * 1024 * 1024
VMEM_LIMIT = V7X_VMEM_BYTES * 3 // 4
ATTN_VMEM_LIMIT = V7X_VMEM_BYTES * 7 // 8


def _sigmoid(x):
    return 1.0 / (1.0 + jnp.exp(-x))


def _gelu_tanh(x):
    c = math.sqrt(2.0 / math.pi)
    return x * (0.5 * (1.0 + jnp.tanh(c * (x + 0.044715 * (x * x * x)))))


def _proj_kernel(x_ref, g_ref, wn_ref, wt_ref, lng_ref, lnb_ref, ws_ref, bs_ref,
                 q_ref, kc_ref, vc_ref, ksw_ref, zn_ref, vt_ref, gt_ref, og_ref, cmp_scr):
    tm = x_ref.shape[1]
    n_ck = tm // GMLP_CHUNK
    x = x_ref[0]
    ms = jnp.mean(x * x, axis=-1, keepdims=True)
    hn = ((x * lax.rsqrt(ms + NORM_EPS)) * g_ref[...]).astype(BF16)

    def proj_t(lo, hi):
        return lax.dot_general(wt_ref[lo:hi, :], hn, (((1,), (1,)), ((), ())),
                               preferred_element_type=F32)

    def proj_n(lo, hi):
        return jnp.dot(hn, wn_ref[:, lo:hi], preferred_element_type=F32)

    vg = proj_t(0, 288)
    vt = vg[0:256].astype(BF16)
    gt = _sigmoid(vg[256:288])
    for ck in range(n_ck):
        sl = slice(ck * GMLP_CHUNK, (ck + 1) * GMLP_CHUNK)
        vt_ref[0, ck] = vt[:, sl]
        gt_ref[0, ck] = gt[:, sl]
    gu = _gelu_tanh(proj_t(288, 800))
    gv = _gelu_tanh(proj_t(800, 1312)).reshape(GMLP_GROUPS, GMLP_GROUP_DIM, tm)
    zg = proj_t(1312, 1824)
    mu = jnp.mean(gv, axis=1, keepdims=True)
    dv = gv - mu
    var = jnp.mean(dv * dv, axis=1, keepdims=True)
    vn = (dv * lax.rsqrt(var + NORM_EPS)) * lng_ref[...] + lnb_ref[...]
    s_i = lax.broadcasted_iota(jnp.int32, (GMLP_CHUNK, GMLP_CHUNK), 0)
    t_i = lax.broadcasted_iota(jnp.int32, (GMLP_CHUNK, GMLP_CHUNK), 1)
    causal = s_i <= t_i
    outs = []
    for g in range(GMLP_GROUPS):
        a = vn[g].astype(BF16)
        a_st = jnp.concatenate(
            [a[:, ck * GMLP_CHUNK:(ck + 1) * GMLP_CHUNK] for ck in range(n_ck)], axis=0)
        w = jnp.where(causal, ws_ref[g], 0.0).astype(BF16)
        r = jnp.dot(a_st, w, preferred_element_type=F32) + bs_ref[g]
        outs.append(jnp.concatenate(
            [r[ck * GMLP_GROUP_DIM:(ck + 1) * GMLP_GROUP_DIM] for ck in range(n_ck)], axis=1))
    mixed = jnp.concatenate(outs, axis=0)
    og_t = (gu * mixed) * (zg * _sigmoid(zg))
    og_ref[0] = og_t.T.astype(BF16)
    q_ref[0] = (proj_n(0, 512) * (HEAD_DIM ** -0.5 * LOG2_E)).astype(BF16)
    kv = proj_n(512, 1024)
    for src, dst_ref, lo in ((0, kc_ref, 0), (1, vc_ref, KV_WIDTH)):
        cmp_scr[src] = kv[:, lo:lo + KV_WIDTH]
        for pos in range(CMP_STRIDE):
            dst_ref[0, :, pos * KV_WIDTH:(pos + 1) * KV_WIDTH] = cmp_scr[
                src, pl.ds(pos, tm // CMP_STRIDE, stride=CMP_STRIDE), :].astype(BF16)
    ksw_ref[0] = kv[:, 2 * KV_WIDTH:4 * KV_WIDTH].astype(BF16)
    z = proj_n(1024, 1536)
    zn_ref[0] = (z * _sigmoid(z)).astype(BF16)


def _proj_call(x, g_in, wn, wt, lng, lnb, ws_t, bs, tm):
    B, T, D = x.shape
    n_ck = tm // GMLP_CHUNK
    grid = (B, T // tm)
    const = lambda *shape: pl.BlockSpec(shape, lambda b, i: (0,) * len(shape))
    row = lambda w: pl.BlockSpec((1, tm, w), lambda b, i: (b, i, 0))
    out_shape = (
        jax.ShapeDtypeStruct((B, T, NSA_WIDTH), BF16),
        jax.ShapeDtypeStruct((B, T // CMP_STRIDE, CMP_STRIDE * KV_WIDTH), BF16),
        jax.ShapeDtypeStruct((B, T // CMP_STRIDE, CMP_STRIDE * KV_WIDTH), BF16),
        jax.ShapeDtypeStruct((B, T, 2 * KV_WIDTH), BF16),
        jax.ShapeDtypeStruct((B, T, NSA_WIDTH), BF16),
        jax.ShapeDtypeStruct((B, T // GMLP_CHUNK, 256, GMLP_CHUNK), BF16),
        jax.ShapeDtypeStruct((B, T // GMLP_CHUNK, 2 * GATE_ROWS, GMLP_CHUNK), F32),
        jax.ShapeDtypeStruct((B, T, GMLP_WIDTH), BF16),
    )
    out_specs = (
        row(NSA_WIDTH),
        pl.BlockSpec((1, tm // CMP_STRIDE, CMP_STRIDE * KV_WIDTH), lambda b, i: (b, i, 0)),
        pl.BlockSpec((1, tm // CMP_STRIDE, CMP_STRIDE * KV_WIDTH), lambda b, i: (b, i, 0)),
        row(2 * KV_WIDTH), row(NSA_WIDTH),
        pl.BlockSpec((1, n_ck, 256, GMLP_CHUNK), lambda b, i: (b, i, 0, 0)),
        pl.BlockSpec((1, n_ck, 2 * GATE_ROWS, GMLP_CHUNK), lambda b, i: (b, i, 0, 0)),
        row(GMLP_WIDTH),
    )
    in_specs = [
        pl.BlockSpec((1, tm, D), lambda b, i: (b, i, 0)),
        const(1, D), const(*wn.shape), const(*wt.shape),
        const(*lng.shape), const(*lnb.shape), const(*ws_t.shape), const(*bs.shape),
    ]
    return pl.pallas_call(
        _proj_kernel, out_shape=out_shape, grid=grid, in_specs=in_specs, out_specs=out_specs,
        scratch_shapes=[pltpu.VMEM((2, tm, KV_WIDTH), F32)],
        name="proj",
        compiler_params=pltpu.CompilerParams(
            dimension_semantics=("arbitrary", "arbitrary"), vmem_limit_bytes=VMEM_LIMIT),
    )(x, g_in, wn, wt, lng, lnb, ws_t, bs)


def _compress_kernel(kc_ref, vc_ref, pek_ref, pev_ref, wk1_ref, wv1_ref, wk2_ref, wv2t_ref,
                     kcmp_ref, vcmpt_ref):
    n_c = kc_ref.shape[1]
    half = 2 * CMP_HIDDEN

    def hidden(src_ref, pe_ref, w1_ref):
        xs = src_ref[0].astype(F32)
        xa = (xs + pe_ref[0:1, :]).astype(BF16)
        xb = (xs + pe_ref[1:2, :]).astype(BF16)
        ha = jnp.dot(xa, w1_ref[:, 0:half], preferred_element_type=F32)
        hb = jnp.dot(xb, w1_ref[:, half:2 * half], preferred_element_type=F32)
        hb_next = pltpu.roll(hb, shift=n_c - 1, axis=0)
        return _gelu_tanh(ha + hb_next).astype(BF16)

    hk = hidden(kc_ref, pek_ref, wk1_ref)
    kcmp_ref[0] = jnp.dot(hk, wk2_ref[...], preferred_element_type=F32).astype(BF16)
    hv = hidden(vc_ref, pev_ref, wv1_ref)
    vcmpt_ref[0] = lax.dot_general(wv2t_ref[...], hv, (((1,), (1,)), ((), ())),
                                   preferred_element_type=F32).astype(BF16)


def _compress_call(kc2, vc2, pek, pev, wk1, wv1, wk2, wv2t):
    B, n_c, W = kc2.shape
    const = lambda a: pl.BlockSpec(a.shape, lambda b: (0,) * a.ndim)
    return pl.pallas_call(
        _compress_kernel,
        out_shape=(jax.ShapeDtypeStruct((B, n_c, KV_WIDTH), BF16),
                   jax.ShapeDtypeStruct((B, KV_WIDTH, n_c), BF16)),
        grid=(B,),
        in_specs=[pl.BlockSpec((1, n_c, W), lambda b: (b, 0, 0)),
                  pl.BlockSpec((1, n_c, W), lambda b: (b, 0, 0)),
                  const(pek), const(pev), const(wk1), const(wv1), const(wk2), const(wv2t)],
        out_specs=(pl.BlockSpec((1, n_c, KV_WIDTH), lambda b: (b, 0, 0)),
                   pl.BlockSpec((1, KV_WIDTH, n_c), lambda b: (b, 0, 0))),
        name="compress",
        compiler_params=pltpu.CompilerParams(
            dimension_semantics=("arbitrary",), vmem_limit_bytes=VMEM_LIMIT),
    )(kc2, vc2, pek, pev, wk1, wv1, wk2, wv2t)


def _attn_kernel(q_ref, kcmp_ref, vcmpt_ref, ks_ref, kw_ref, vst_ref, vwt_ref, gt_ref, zn_ref,
                 o_ref, mask_scr, onehot_scr, qb_scr, m_scr, acc_scr, o_scr, s_scr, p_scr,
                 s1_scr, p1_scr, m1_scr):
    g = pl.program_id(1)
    T = q_ref.shape[1]
    n_c = kcmp_ref.shape[1]
    n_blk = T // SEL_BLOCK
    n_tiles = T // Q_TILE
    assert KV_WIDTH + n_blk <= MXU_DEPTH

    def with_ones(vt):
        return jnp.concatenate([vt, jnp.ones((ONES_ROWS, vt.shape[1]), vt.dtype)], axis=0)

    def normalized(acc):
        return acc[0:HEAD_DIM] / acc[HEAD_DIM:HEAD_DIM + 1]

    def gate_row(c, br):
        gt = gt_ref[0, c]
        return jnp.concatenate(
            [gt[4 * br + h:4 * br + h + 1] for h in range(HEADS_PER_GROUP)], axis=1)

    lane_q = lax.broadcasted_iota(jnp.int32, (1, LANES), 1) & (Q_TILE - 1)
    MASK_CAUSAL, MASK_WINDOW, MASK_NONE, MASK_ALL = range(4)

    @pl.when((pl.program_id(0) == 0) & (g == 0))
    def _():
        key_i = lax.broadcasted_iota(jnp.int32, (T, MXU_DEPTH - KV_WIDTH), 0)
        blk_i = lax.broadcasted_iota(jnp.int32, (T, MXU_DEPTH - KV_WIDTH), 1)
        onehot_scr[...] = jnp.where((key_i >> (SEL_BLOCK.bit_length() - 1)) == blk_i,
                                    1.0, 0.0).astype(BF16)
        if KV_WIDTH + n_blk < MXU_DEPTH:
            qb_scr[:, KV_WIDTH + n_blk:, :] = jnp.zeros(
                (n_tiles, MXU_DEPTH - KV_WIDTH - n_blk, LANES), BF16)
        key_row = lax.broadcasted_iota(jnp.int32, (KEY_CHUNK, LANES), 0)
        mask_scr[MASK_CAUSAL] = jnp.where(key_row <= lane_q, 0.0, NEG_INF)
        mask_scr[MASK_WINDOW] = jnp.where(key_row > lane_q, 0.0, NEG_INF)
        mask_scr[MASK_NONE] = jnp.zeros((KEY_CHUNK, LANES), F32)
        mask_scr[MASK_ALL] = jnp.full((KEY_CHUNK, LANES), NEG_INF, F32)

    kcmp = kcmp_ref[0]
    vct1 = with_ones(vcmpt_ref[0])
    jb_c = lax.broadcasted_iota(jnp.int32, (n_blk, n_c), 0)
    ic_c = lax.broadcasted_iota(jnp.int32, (n_blk, n_c), 1)
    overlap_t = jnp.where(ic_c * CMP_STRIDE < (jb_c + 1) * SEL_BLOCK,
                          jnp.where(ic_c * CMP_STRIDE + (CMP_BLOCK - 1) >= jb_c * SEL_BLOCK, 1.0, 0.0),
                          0.0).astype(BF16)
    row_group = lax.broadcasted_iota(jnp.int32, (KV_WIDTH, LANES), 0) >> (HEAD_DIM.bit_length() - 1)
    cmp_end = lax.broadcasted_iota(jnp.int32, (n_c, LANES), 0) * CMP_STRIDE + (CMP_BLOCK - 1)
    blk_row = lax.broadcasted_iota(jnp.int32, (n_blk, Q_TILE), 0)

    win_chunks = WIN_KEYS // KEY_CHUNK
    r_win, r_own, r_end = n_c, n_c + WIN_KEYS, n_c + WIN_KEYS + KEY_CHUNK

    def win_first_chunk(c):
        return jnp.maximum(c - WINDOW // KEY_CHUNK, 0)

    def tile_scores(c):
        t0 = c * Q_TILE if isinstance(c, int) else pl.multiple_of(c * Q_TILE, Q_TILE)
        qt = q_ref[0, pl.ds(t0, Q_TILE), :].astype(F32).T
        b64 = jnp.concatenate(
            [qt[h * HEAD_DIM:(h + 1) * HEAD_DIM] for h in range(HEADS_PER_GROUP)], axis=1)
        qmat = jnp.where(row_group == g, jnp.concatenate([b64, b64], axis=0), 0.0).astype(BF16)
        qb_scr[c, 0:KV_WIDTH, :] = qmat
        tq = t0 + lane_q
        s = jnp.where(cmp_end <= tq, jnp.dot(kcmp, qmat, preferred_element_type=F32), NEG_INF)
        s1_scr[0:r_win] = s
        m1_scr[0] = jnp.max(s, axis=0, keepdims=True)
        w0 = win_first_chunk(c)
        wk0 = pl.multiple_of(w0 * KEY_CHUNK, KEY_CHUNK)
        sw = jnp.dot(kw_ref[0, pl.ds(wk0, WIN_KEYS), :], qmat, preferred_element_type=F32)
        m_w = None
        for u in range(win_chunks):
            ck = w0 + u
            kind = jnp.where(ck == c, MASK_CAUSAL,
                             jnp.where(ck > c, MASK_ALL,
                                       jnp.where(ck == c - WINDOW // KEY_CHUNK, MASK_WINDOW, MASK_NONE)))
            su = sw[u * KEY_CHUNK:(u + 1) * KEY_CHUNK] + mask_scr[kind]
            s1_scr[r_win + u * KEY_CHUNK:r_win + (u + 1) * KEY_CHUNK] = su
            mu = jnp.max(su, axis=0, keepdims=True)
            m_w = mu if m_w is None else jnp.maximum(m_w, mu)
        m1_scr[1] = m_w
        sd = (jnp.dot(ks_ref[0, pl.ds(t0, KEY_CHUNK), :], qmat, preferred_element_type=F32)
              + mask_scr[MASK_CAUSAL])
        s1_scr[r_own:r_end] = sd
        m_scr[c] = jnp.max(sd, axis=0, keepdims=True)

    def tile_exponentials(c):
        p1_scr[0:r_win] = jnp.exp2(s1_scr[0:r_win] - m1_scr[0]).astype(BF16)
        p1_scr[r_win:r_own] = jnp.exp2(s1_scr[r_win:r_own] - m1_scr[1]).astype(BF16)
        p1_scr[r_own:r_end] = jnp.exp2(s1_scr[r_own:r_end] - m_scr[c]).astype(BF16)

    def tile_outputs(c, rank_rows):
        t0 = c * Q_TILE
        tq = t0 + lane_q
        e16 = p1_scr[0:r_win]
        acc_c = jnp.dot(vct1, e16, preferred_element_type=F32)
        linv_c = jnp.where(tq >= CMP_BLOCK - 1, 1.0 / acc_c[HEAD_DIM:HEAD_DIM + 1], 0.0)
        o_c = acc_c[0:HEAD_DIM] * linv_c
        imp4 = jnp.dot(overlap_t, e16, preferred_element_type=F32) * linv_c
        imp = imp4[:, 0:Q_TILE]
        for h in range(1, HEADS_PER_GROUP):
            imp = imp + imp4[:, h * Q_TILE:(h + 1) * Q_TILE]

        cur = (t0 + lax.broadcasted_iota(jnp.int32, (1, Q_TILE), 1)) >> (SEL_BLOCK.bit_length() - 1)
        valid_b = blk_row <= cur
        bonus = jnp.where(blk_row == 0, FORCE_BONUS,
                          jnp.where(blk_row == cur, FORCE_BONUS,
                                    jnp.where(blk_row == cur - 1, FORCE_BONUS, 0.0)))
        score = jnp.where(valid_b, imp + bonus, -1.0)
        before_tile = blk_row < c * (Q_TILE // SEL_BLOCK)
        sub_row = lax.broadcasted_iota(jnp.int32, (8, Q_TILE), 0)

        def keep_ranked(n_rows):
            n_rb = n_rows // 8
            sblk = [score[8 * r:8 * r + 8] for r in range(n_rb)]
            cnt = [jnp.zeros((8, Q_TILE), F32) for _ in range(n_rb)]
            for j in range(n_rows):
                rj = jnp.broadcast_to(score[j:j + 1], (8, Q_TILE))
                for r in range(n_rb):
                    if 8 * r + 7 <= j:
                        cnt[r] = cnt[r] + jnp.where(rj > sblk[r], 1.0, 0.0)
                    elif 8 * r > j:
                        cnt[r] = cnt[r] + jnp.where(rj >= sblk[r], 1.0, 0.0)
                    else:
                        cnt[r] = cnt[r] + jnp.where(sub_row + 8 * r > j,
                                                    jnp.where(rj >= sblk[r], 1.0, 0.0),
                                                    jnp.where(rj > sblk[r], 1.0, 0.0))
            in_top = jnp.concatenate(
                [jnp.where(cnt[r] < float(SEL_TOP_N), 0.0, NEG_INF) for r in range(n_rb)]
                + [jnp.full((n_blk - n_rows, Q_TILE), NEG_INF, F32)] * (n_rows < n_blk), axis=0)
            return jnp.where(before_tile, in_top, NEG_INF)

        if rank_rows == 0:
            keep = jnp.where(before_tile, 0.0, NEG_INF)
        else:
            keep = keep_ranked(rank_rows)
        qb_scr[c, KV_WIDTH:KV_WIDTH + n_blk, :] = jnp.concatenate(
            [keep] * HEADS_PER_GROUP, axis=1).astype(BF16)

        w0 = win_first_chunk(c)
        vw = jnp.concatenate([vwt_ref[0, w0 + u] for u in range(win_chunks)], axis=1)
        acc_w = jnp.dot(with_ones(vw), p1_scr[r_win:r_own], preferred_element_type=F32)
        acc_scr[c] = jnp.dot(with_ones(vst_ref[0, c]), p1_scr[r_own:r_end],
                             preferred_element_type=F32)
        o_scr[c] = gate_row(c, 0) * o_c + gate_row(c, 2) * normalized(acc_w)

    assert n_tiles >= 2
    tile_scores(0)
    tile_exponentials(0)
    tile_scores(1)

    blocks_per_tile = Q_TILE // SEL_BLOCK
    sizes = [0] + [r for r in (n_blk // 2, 3 * n_blk // 4) if r % 8 == 0 and r > SEL_TOP_N] + [n_blk]
    limits = [SEL_TOP_N // blocks_per_tile] + [r // blocks_per_tile for r in sizes[1:]]
    start = 0
    for rank_rows, limit in zip(sizes, limits):
        stop = max(start, min(limit, n_tiles - 2))

        def body(c, rank_rows=rank_rows):
            tile_outputs(c, rank_rows)
            tile_exponentials(c + 1)
            tile_scores(c + 2)

        pl.loop(start, stop)(body)
        start = stop

    tile_outputs(n_tiles - 2, n_blk)
    tile_exponentials(n_tiles - 1)
    tile_outputs(n_tiles - 1, n_blk)

    chunks_per_step = SEL_KEYS // KEY_CHUNK
    n_sel_steps = T // SEL_KEYS
    n_groups = n_tiles // SEL_GROUP
    assert SEL_KEYS % (SEL_GROUP * Q_TILE) == 0 and n_tiles % SEL_GROUP == 0
    groups_per_step = SEL_KEYS // (SEL_GROUP * Q_TILE)
    n_items = sum(n_groups - groups_per_step * j for j in range(n_sel_steps))
    assert n_items >= 2

    def next_item(item):
        j, q = item
        wrap = q + 1 == n_groups
        return jnp.where(wrap, j + 1, j), jnp.where(wrap, groups_per_step * (j + 1), q + 1)

    def sel_scores(item):
        j, q = item
        k0 = pl.multiple_of(j * SEL_KEYS, SEL_KEYS)
        k_ext = jnp.concatenate([ks_ref[0, pl.ds(k0, SEL_KEYS), :], onehot_scr[pl.ds(k0, SEL_KEYS), :]],
                                axis=1)
        mxs = []
        for i in range(SEL_GROUP):
            sc = jnp.dot(k_ext, qb_scr[q * SEL_GROUP + i], preferred_element_type=F32)
            s_scr[i] = sc
            mxs.append(jnp.max(sc, axis=0, keepdims=True))
        return tuple(mxs)

    def sel_exponentials(mxs):
        for i in range(SEL_GROUP):
            p_scr[i] = jnp.exp2(s_scr[i] - mxs[i]).astype(BF16)

    def sel_accumulate(item, mxs):
        j, q = item
        vs = with_ones(jnp.concatenate(
            [vst_ref[0, chunks_per_step * j + u] for u in range(chunks_per_step)], axis=1))
        for i in range(SEL_GROUP):
            c = q * SEL_GROUP + i
            pv = jnp.dot(vs, p_scr[i], preferred_element_type=F32)
            m = m_scr[c]
            m_new = jnp.maximum(m, mxs[i])
            acc_scr[c] = jnp.exp2(m - m_new) * acc_scr[c] + jnp.exp2(mxs[i] - m_new) * pv
            m_scr[c] = m_new

    item_a = (jnp.int32(0), jnp.int32(0))
    item_b = next_item(item_a)
    mx_a = sel_scores(item_a)
    sel_exponentials(mx_a)
    mx_b = sel_scores(item_b)

    def sel_body(_, carry):
        item_a, item_b, mx_a, mx_b = carry
        item_c = next_item(item_b)
        sel_accumulate(item_a, mx_a)
        sel_exponentials(mx_b)
        return item_b, item_c, mx_b, sel_scores(item_c)

    item_a, item_b, mx_a, mx_b = lax.fori_loop(0, n_items - 2, sel_body, (item_a, item_b, mx_a, mx_b))
    sel_accumulate(item_a, mx_a)
    sel_exponentials(mx_b)
    sel_accumulate(item_b, mx_b)

    @pl.loop(0, n_tiles, unroll=OUT_UNROLL)
    def _(c):
        t0 = pl.multiple_of(c * Q_TILE, Q_TILE)
        o_t = o_scr[c] + gate_row(c, 1) * normalized(acc_scr[c])
        stacked = jnp.concatenate(
            [o_t[:, h * Q_TILE:(h + 1) * Q_TILE] for h in range(HEADS_PER_GROUP)], axis=0)
        o = stacked.T * zn_ref[0, pl.ds(t0, Q_TILE), :].astype(F32)
        o_ref[0, pl.ds(t0, Q_TILE), :] = o.astype(BF16)


def _attn_call(q, kcmp, vcmpt, ksw, vt, gt, zn):
    B, T, _ = q.shape
    n_c = kcmp.shape[1]
    n_ck = T // KEY_CHUNK
    n_tiles = T // Q_TILE
    G = NSA_KV_HEADS
    gw = HEADS_PER_GROUP * HEAD_DIM
    in_specs = [
        pl.BlockSpec((1, T, gw), lambda b, g: (b, 0, g)),
        pl.BlockSpec((1, n_c, KV_WIDTH), lambda b, g: (b, 0, 0)),
        pl.BlockSpec((1, HEAD_DIM, n_c), lambda b, g: (b, g, 0)),
        pl.BlockSpec((1, T, KV_WIDTH), lambda b, g: (b, 0, 0)),
        pl.BlockSpec((1, T, KV_WIDTH), lambda b, g: (b, 0, 1)),
        pl.BlockSpec((1, n_ck, HEAD_DIM, KEY_CHUNK), lambda b, g: (b, 0, g, 0)),
        pl.BlockSpec((1, n_ck, HEAD_DIM, KEY_CHUNK), lambda b, g: (b, 0, G + g, 0)),
        pl.BlockSpec((1, n_ck, GATE_ROWS, KEY_CHUNK), lambda b, g: (b, 0, g, 0)),
        pl.BlockSpec((1, T, gw), lambda b, g: (b, 0, g)),
    ]
    return pl.pallas_call(
        _attn_kernel,
        out_shape=jax.ShapeDtypeStruct((B, T, NSA_WIDTH), BF16),
        grid=(B, G),
        in_specs=in_specs,
        out_specs=pl.BlockSpec((1, T, gw), lambda b, g: (b, 0, g)),
        scratch_shapes=[pltpu.VMEM((4, KEY_CHUNK, LANES), F32),
                        pltpu.VMEM((T, MXU_DEPTH - KV_WIDTH), BF16),
                        pltpu.VMEM((n_tiles, MXU_DEPTH, LANES), BF16),
                        pltpu.VMEM((n_tiles, 1, LANES), F32),
                        pltpu.VMEM((n_tiles, HEAD_DIM + ONES_ROWS, LANES), F32),
                        pltpu.VMEM((n_tiles, HEAD_DIM, LANES), F32),
                        pltpu.VMEM((SEL_GROUP, SEL_KEYS, LANES), F32),
                        pltpu.VMEM((SEL_GROUP, SEL_KEYS, LANES), BF16),
                        pltpu.VMEM((n_c + WIN_KEYS + KEY_CHUNK, LANES), F32),
                        pltpu.VMEM((n_c + WIN_KEYS + KEY_CHUNK, LANES), BF16),
                        pltpu.VMEM((2, 1, LANES), F32)],
        name="attn",
        compiler_params=pltpu.CompilerParams(
            dimension_semantics=("arbitrary", "arbitrary"), vmem_limit_bytes=ATTN_VMEM_LIMIT),
    )(q, kcmp, vcmpt, ksw, ksw, vt, vt, gt, zn)


def _out_kernel(x_ref, mn_ref, mg_ref, wo_ref, g_ref, o_ref):
    mix = jnp.concatenate([mn_ref[0], mg_ref[0]], axis=1)
    h = x_ref[0] + jnp.dot(mix, wo_ref[...], preferred_element_type=F32)
    ms = jnp.mean(h * h, axis=-1, keepdims=True)
    o_ref[0] = (h * lax.rsqrt(ms + NORM_EPS)) * g_ref[...]


def _out_call(x, mix_nsa, mix_gmlp, wo, g_f, tm):
    B, T, D = x.shape
    row = lambda w: pl.BlockSpec((1, tm, w), lambda b, i: (b, i, 0))
    return pl.pallas_call(
        _out_kernel,
        out_shape=jax.ShapeDtypeStruct((B, T, D), x.dtype),
        grid=(B, T // tm),
        in_specs=[row(D), row(NSA_WIDTH), row(GMLP_WIDTH),
                  pl.BlockSpec(wo.shape, lambda b, i: (0, 0)),
                  pl.BlockSpec((1, D), lambda b, i: (0, 0))],
        out_specs=row(D),
        name="out",
        compiler_params=pltpu.CompilerParams(
            dimension_semantics=("arbitrary", "arbitrary"), vmem_limit_bytes=VMEM_LIMIT),
    )(x, mix_nsa, mix_gmlp, wo, g_f)


def _split_w_in(w):
    sizes = (NSA_WIDTH, KV_WIDTH, KV_WIDTH, KV_WIDTH, KV_WIDTH, KV_WIDTH, KV_WIDTH, N_GATES,
             NSA_WIDTH, GMLP_WIDTH, GMLP_WIDTH, GMLP_WIDTH)
    parts, off = [], 0
    for s in sizes:
        parts.append(w[:, off:off + s])
        off += s
    return parts


def _gate_columns(w_gate):
    d = w_gate.shape[0]
    wg = w_gate.reshape(d, NSA_KV_HEADS, HEADS_PER_GROUP, 3).transpose(0, 1, 3, 2)
    wg = wg.reshape(d, NSA_KV_HEADS, 3 * HEADS_PER_GROUP)
    wg = jnp.pad(wg, ((0, 0), (0, 0), (0, GATE_ROWS - 3 * HEADS_PER_GROUP)))
    return wg.reshape(d, NSA_KV_HEADS * GATE_ROWS)


def _compress_w1(w1):
    half = CMP_BLOCK // 2
    w = w1.reshape(2, half, HEAD_DIM, CMP_HIDDEN)
    eye = jnp.eye(NSA_KV_HEADS, dtype=w1.dtype)
    full = jnp.einsum('hlde,gk->lgdhke', w, eye)
    return full.reshape(half * KV_WIDTH, 2 * NSA_KV_HEADS * CMP_HIDDEN)


def _compress_pe(pe):
    half = CMP_BLOCK // 2
    p = pe.reshape(2, half, 1, HEAD_DIM)
    p = jnp.broadcast_to(p, (2, half, NSA_KV_HEADS, HEAD_DIM))
    return p.reshape(2, half * KV_WIDTH)


def _block_diag2(w2):
    z = jnp.zeros_like(w2)
    return jnp.concatenate([jnp.concatenate([w2, z], axis=1),
                            jnp.concatenate([z, w2], axis=1)], axis=0)


def kernel(x, norm_in_g, w_in, w_cmp_k1, w_cmp_k2, pe_cmp_k, w_cmp_v1, w_cmp_v2, pe_cmp_v,
           gmlp_ln_g, gmlp_ln_b, w_spatial, b_spatial, w_out, norm_f_g):
    B, T, D = x.shape
    assert w_in.shape[0] == 1, "single-layer block"
    tm = min(PROJ_TILE, T)
    (wq, wkc, wvc, wks, wvs, wkw, wvw, wgt, wzn, wu, wv, wzg) = _split_w_in(w_in[0])
    wn = jnp.concatenate([wq, wkc, wvc, wks, wkw, wzn], axis=1).astype(BF16)
    wt = jnp.concatenate([wvs, wvw, _gate_columns(wgt), wu, wv, wzg], axis=1).T.astype(BF16)
    lng = gmlp_ln_g[0].reshape(GMLP_GROUPS, GMLP_GROUP_DIM, 1)
    lnb = gmlp_ln_b[0].reshape(GMLP_GROUPS, GMLP_GROUP_DIM, 1)
    ws_t = jnp.swapaxes(w_spatial[0], 1, 2)
    bs = b_spatial[0].reshape(GMLP_GROUPS, 1, GMLP_CHUNK)
    q, kc, vc, ksw, zn, vt, gt, o_gmlp = _proj_call(
        x, norm_in_g[0].reshape(1, D), wn, wt, lng, lnb, ws_t, bs, tm)

    n_c = T // CMP_STRIDE
    kcmp, vcmpt = _compress_call(
        kc, vc,
        _compress_pe(pe_cmp_k[0]), _compress_pe(pe_cmp_v[0]),
        _compress_w1(w_cmp_k1[0]).astype(BF16), _compress_w1(w_cmp_v1[0]).astype(BF16),
        _block_diag2(w_cmp_k2[0]).astype(BF16), _block_diag2(w_cmp_v2[0]).T.astype(BF16))

    o_nsa = _attn_call(q, kcmp, vcmpt, ksw, vt, gt, zn)
    return _out_call(x, o_nsa, o_gmlp, w_out[0].astype(BF16), norm_f_g.reshape(1, D),
                     min(OUT_TILE, T))
```
